```python
import math
import jax
import jax.numpy as jnp
from jax import lax
import numpy as np

D_MODEL = 4096
BATCH = 1
SEQ = 16384
DEPTH = 2

HEAD_DIM = 128
N_MIX_HEADS = D_MODEL // HEAD_DIM
NAT_HEADS = N_MIX_HEADS // 2
DIFF_HEADS = N_MIX_HEADS // 4
MLSTM_HEADS = N_MIX_HEADS - NAT_HEADS - DIFF_HEADS
NAT_WIDTH = NAT_HEADS * HEAD_DIM
DIFF_WIDTH = DIFF_HEADS * HEAD_DIM
MLSTM_WIDTH = MLSTM_HEADS * HEAD_DIM
MIX_WIDTH = NAT_WIDTH + DIFF_WIDTH + MLSTM_WIDTH
DIFF_QK_DIM = HEAD_DIM // 2
IN_WIDTH = 3 * NAT_WIDTH + 3 * DIFF_WIDTH + 4 * MLSTM_WIDTH + 4 * MLSTM_HEADS

GRID_W = 64
NAT_WIN_ROWS = 8
NAT_WIN_COLS = 16
NAT_Q_ROWS = 2

REL_BUCKETS = 32
REL_MAX_DIST = 128
Q_BLOCK = 128

MLSTM_CHUNK = 64
MLSTM_CONV = 5

N_EXPERTS = 32
TOP_K = 4
EXPERT_FF = 768
SWIGLU_ALPHA = 1.702
SWIGLU_LIMIT = 7.0
MOE_BLOCK = 512

LN_EPS = 1e-5

kernel_name = 'hybrid_nat_diffattn_mlstm_moe_encoder'


def layer_norm(x, g, b):
    xf = x.astype(jnp.float32)
    mu = jnp.mean(xf, -1, keepdims=True)
    var = jnp.mean(jnp.square(xf - mu), -1, keepdims=True)
    y = (xf - mu) * lax.rsqrt(var + LN_EPS) * g.astype(jnp.float32) + b.astype(jnp.float32)
    return y.astype(x.dtype)


def head_rms_norm(x, g):
    xf = x.astype(jnp.float32)
    return xf * lax.rsqrt(jnp.mean(jnp.square(xf), -1, keepdims=True) + LN_EPS) * g.astype(jnp.float32)


def head_layer_norm(x, g):
    xf = x.astype(jnp.float32)
    xc = xf - jnp.mean(xf, -1, keepdims=True)
    return xc * lax.rsqrt(jnp.mean(jnp.square(xc), -1, keepdims=True) + LN_EPS) * g.astype(jnp.float32)


def neighbourhood_attention(q, k, v, rpb):
    B, T, H, Dh = q.shape
    rows = T // GRID_W
    win_r = min(NAT_WIN_ROWS, rows)
    kb_rows = min(win_r + NAT_Q_ROWS - 1, rows)
    n_blk = rows // NAT_Q_ROWS
    qg = (q * Dh ** -0.5).reshape(B, rows, GRID_W, H, Dh)
    kg = k.reshape(B, rows, GRID_W, H, Dh)
    vg = v.reshape(B, rows, GRID_W, H, Dh)
    col = jnp.arange(GRID_W)
    c0 = jnp.clip(col - NAT_WIN_COLS // 2, 0, GRID_W - NAT_WIN_COLS)
    q_col = jnp.tile(col, NAT_Q_ROWS)
    q_c0 = jnp.tile(c0, NAT_Q_ROWS)
    k_col = jnp.tile(col, kb_rows)
    col_ok = (k_col[None, :] >= q_c0[:, None]) & (k_col[None, :] < q_c0[:, None] + NAT_WIN_COLS)
    dc = jnp.clip(k_col[None, :] - q_col[:, None] + NAT_WIN_COLS - 1, 0, 2 * NAT_WIN_COLS - 2)

    def block(i):
        r_start = i * NAT_Q_ROWS
        qr = r_start + jnp.arange(NAT_Q_ROWS)
        r0 = jnp.clip(qr - win_r // 2, 0, rows - win_r)
        k_start = jnp.minimum(r0[0], rows - kb_rows)
        kr = k_start + jnp.arange(kb_rows)
        qb = lax.dynamic_slice_in_dim(qg, r_start, NAT_Q_ROWS, axis=1).reshape(B, NAT_Q_ROWS * GRID_W, H, Dh)
        kb = lax.dynamic_slice_in_dim(kg, k_start, kb_rows, axis=1).reshape(B, kb_rows * GRID_W, H, Dh)
        vb = lax.dynamic_slice_in_dim(vg, k_start, kb_rows, axis=1).reshape(B, kb_rows * GRID_W, H, Dh)
        q_row = jnp.repeat(qr, GRID_W)
        q_r0 = jnp.repeat(r0, GRID_W)
        k_row = jnp.repeat(kr, GRID_W)
        row_ok = (k_row[None, :] >= q_r0[:, None]) & (k_row[None, :] < q_r0[:, None] + win_r)
        dr = jnp.clip(k_row[None, :] - q_row[:, None] + NAT_WIN_ROWS - 1, 0, 2 * NAT_WIN_ROWS - 2)
        bias = rpb[:, dr, dc].astype(jnp.float32)
        s = jnp.einsum('bqhd,bkhd->bhqk', qb, kb).astype(jnp.float32) + bias[None]
        s = jnp.where((row_ok & col_ok)[None, None], s, -jnp.inf)
        p = jax.nn.softmax(s, axis=-1).astype(vb.dtype)
        return jnp.einsum('bhqk,bkhd->bqhd', p, vb)

    out = lax.map(block, jnp.arange(n_blk))
    return jnp.moveaxis(out, 0, 1).reshape(B, T, H, Dh)


def t5_bucket(rel):
    half = REL_BUCKETS // 2
    max_exact = half // 2
    ret = jnp.where(rel > 0, half, 0)
    n = jnp.abs(rel)
    n_f = jnp.maximum(n, 1).astype(jnp.float32)
    large = max_exact + (jnp.log(n_f / max_exact) / math.log(REL_MAX_DIST / max_exact)
                         * (half - max_exact)).astype(jnp.int32)
    large = jnp.minimum(large, half - 1)
    return ret + jnp.where(n < max_exact, n, large)


def differential_attention(q, k, v, t5_table, lam):
    B, T, H, _, dq = q.shape
    n_blk = T // Q_BLOCK
    qb_all = (q * dq ** -0.5).reshape(B, n_blk, Q_BLOCK, H, 2, dq)
    k_pos = jnp.arange(T)

    def block(i):
        qi = lax.dynamic_index_in_dim(qb_all, i, axis=1, keepdims=False)
        q_pos = i * Q_BLOCK + jnp.arange(Q_BLOCK)
        bias = jnp.take(t5_table, t5_bucket(k_pos[None, :] - q_pos[:, None]), axis=0)
        bias = jnp.transpose(bias, (2, 0, 1)).astype(jnp.float32)
        s = jnp.einsum('bqhmd,bkhmd->bhmqk', qi, k).astype(jnp.float32) + bias[None, :, None]
        p = jax.nn.softmax(s, axis=-1)
        a = p[:, :, 0] - lam * p[:, :, 1]
        return jnp.einsum('bhqk,bkhd->bqhd', a.astype(v.dtype), v)

    out = lax.map(block, jnp.arange(n_blk))
    return jnp.moveaxis(out, 0, 1).reshape(B, T, H, 2 * dq)


def mlstm_chunkwise(q, k, v, ig, lf):
    B, H, T, Dh = q.shape
    L = MLSTM_CHUNK
    nc = T // L
    q = q.reshape(B, H, nc, L, Dh)
    k = k.reshape(B, H, nc, L, Dh)
    v = v.reshape(B, H, nc, L, Dh)
    ig = ig.reshape(B, H, nc, L)
    b = jnp.cumsum(lf.reshape(B, H, nc, L), axis=-1)
    g = b[..., -1]
    tri = jnp.tril(jnp.ones((L, L), dtype=bool))
    D = jnp.where(tri, b[..., :, None] - b[..., None, :] + ig[..., None, :], -jnp.inf)
    a = g[..., None] - b + ig
    m_loc = jnp.max(a, axis=-1)
    w = jnp.exp(a - m_loc[..., None])
    C_loc = jnp.einsum('bhcsk,bhcsv->bhckv', k * w[..., None], v)
    n_loc = jnp.einsum('bhcs,bhcsk->bhck', w, k)

    def step(carry, inp):
        C, n, m = carry
        C_l, n_l, m_l, g_c = inp
        m_new = jnp.maximum(g_c + m, m_l)
        s_old = jnp.exp(g_c + m - m_new)
        s_loc = jnp.exp(m_l - m_new)
        C_new = s_old[..., None, None] * C + s_loc[..., None, None] * C_l
        n_new = s_old[..., None] * n + s_loc[..., None] * n_l
        return (C_new, n_new, m_new), (C, n, m)

    init = (jnp.zeros((B, H, Dh, Dh), q.dtype), jnp.zeros((B, H, Dh), q.dtype), jnp.zeros((B, H), q.dtype))
    xs = (jnp.moveaxis(C_loc, 2, 0), jnp.moveaxis(n_loc, 2, 0), jnp.moveaxis(m_loc, 2, 0), jnp.moveaxis(g, 2, 0))
    _, (C_prev, n_prev, m_prev) = lax.scan(step, init, xs)
    C_prev = jnp.moveaxis(C_prev, 0, 2)
    n_prev = jnp.moveaxis(n_prev, 0, 2)
    m_prev = jnp.moveaxis(m_prev, 0, 2)
    inter_log = b + m_prev[..., None]
    m_t = jnp.maximum(inter_log, jnp.max(D, axis=-1))
    w_inter = jnp.exp(inter_log - m_t)
    P = jnp.exp(D - m_t[..., None]) * jnp.einsum('bhctd,bhcsd->bhcts', q, k)
    num = w_inter[..., None] * jnp.einsum('bhctk,bhckv->bhctv', q, C_prev) + jnp.einsum('bhcts,bhcsv->bhctv', P, v)
    den = w_inter * jnp.einsum('bhctk,bhck->bhct', q, n_prev) + jnp.sum(P, axis=-1)
    h = num / jnp.maximum(jnp.abs(den), jnp.exp(-m_t))[..., None]
    return h.reshape(B, H, T, Dh)


def centred_depthwise_conv(x, w):
    C = x.shape[-1]
    W = w.shape[0]
    return lax.conv_general_dilated(x, w[:, None, :].astype(x.dtype), window_strides=(1,),
                                    padding=[(W // 2, W // 2)], dimension_numbers=('NWC', 'WIO', 'NWC'),
                                    feature_group_count=C)


def mlstm_mixer(q, k, v, o, gates, conv_w, gate_b, norm_g):
    B, T, _ = q.shape
    qk = jax.nn.silu(centred_depthwise_conv(jnp.concatenate([q, k], axis=-1), conv_w))
    q, k = jnp.split(qk, 2, axis=-1)
    heads = lambda t: jnp.transpose(t.astype(jnp.float32).reshape(B, T, MLSTM_HEADS, HEAD_DIM), (0, 2, 1, 3))
    qh = heads(q)
    kh = heads(k) * HEAD_DIM ** -0.5
    vh = heads(v)
    g = jnp.transpose((gates + gate_b).astype(jnp.float32).reshape(B, T, 4, MLSTM_HEADS), (2, 0, 3, 1))
    h_f = mlstm_chunkwise(qh, kh, vh, g[0], jax.nn.log_sigmoid(g[1]))
    flip = lambda t: jnp.flip(t, axis=2)
    h_b = flip(mlstm_chunkwise(flip(qh), flip(kh), flip(vh), flip(g[2]), flip(jax.nn.log_sigmoid(g[3]))))
    h = jnp.transpose(h_f + h_b, (0, 2, 1, 3))
    h = jax.nn.sigmoid(o.astype(jnp.float32)).reshape(B, T, MLSTM_HEADS, HEAD_DIM) * h
    return head_layer_norm(h, norm_g.reshape(MLSTM_HEADS, HEAD_DIM))


def moe_ffn(x, w_router, b_router, w_glu, b_glu, w_lin, b_lin, w_down, b_down):
    B, T, D = x.shape
    N = B * T
    xt = x.reshape(N, D)
    logits = (xt @ w_router + b_router).astype(jnp.float32)
    top_v, top_e = lax.top_k(logits, TOP_K)
    gate = jax.nn.softmax(top_v, axis=-1).astype(x.dtype)
    slot_e = top_e.reshape(-1)
    slot_tok = jnp.repeat(jnp.arange(N, dtype=jnp.int32), TOP_K)
    slot_gate = gate.reshape(-1)
    order = jnp.argsort(slot_e)
    e_sorted = slot_e[order]
    tok_sorted = slot_tok[order]
    gate_sorted = slot_gate[order]
    counts = jnp.bincount(slot_e, length=N_EXPERTS)
    padded = (counts + MOE_BLOCK - 1) // MOE_BLOCK * MOE_BLOCK
    start = jnp.cumsum(counts) - counts
    pad_end = jnp.cumsum(padded)
    pad_start = pad_end - padded
    dest = pad_start[e_sorted] + jnp.arange(N * TOP_K) - start[e_sorted]
    n_blocks = -(-(N * TOP_K) // MOE_BLOCK) + N_EXPERTS
    n_rows = n_blocks * MOE_BLOCK
    row_tok = jnp.zeros((n_rows,), jnp.int32).at[dest].set(tok_sorted)
    row_gate = jnp.zeros((n_rows,), x.dtype).at[dest].set(gate_sorted)
    blk_expert = jnp.minimum(jnp.searchsorted(pad_end, jnp.arange(n_blocks) * MOE_BLOCK, side='right'),
                             N_EXPERTS - 1)

    def body(y, inp):
        e, toks, gt = inp
        xb = xt[toks]
        hg = jnp.minimum(xb @ w_glu[e] + b_glu[e], SWIGLU_LIMIT)
        hl = jnp.clip(xb @ w_lin[e] + b_lin[e], -SWIGLU_LIMIT, SWIGLU_LIMIT)
        h = hg * jax.nn.sigmoid(SWIGLU_ALPHA * hg) * (hl + 1.0)
        out = (h @ w_down[e] + b_down[e]) * gt[:, None]
        return y.at[toks].add(out), None

    y, _ = lax.scan(body, jnp.zeros_like(xt),
                    (blk_expert, row_tok.reshape(n_blocks, MOE_BLOCK), row_gate.reshape(n_blocks, MOE_BLOCK)))
    return y.reshape(B, T, D)


def setup_inputs(seed: int = 0) -> dict:
    key = jax.random.key(seed)
    ks = jax.random.split(key, 24)
    f32 = jnp.float32
    beta = (8.0 * DEPTH) ** -0.25
    nrm = lambda k, shape, scale: jax.random.normal(k, shape, f32) * scale
    H = MLSTM_HEADS
    gk = jax.random.split(ks[8], 4)
    f_bias = jnp.linspace(3.0, 6.0, H, dtype=f32)
    ml_gate_b = jnp.concatenate([nrm(gk[0], (DEPTH, H), 0.1), f_bias + nrm(gk[1], (DEPTH, H), 0.01),
                                 nrm(gk[2], (DEPTH, H), 0.1), f_bias + nrm(gk[3], (DEPTH, H), 0.01)], axis=-1)
    return {
        'x': nrm(ks[0], (BATCH, SEQ, D_MODEL), 1.0),
        'w_in': nrm(ks[1], (DEPTH, D_MODEL, IN_WIDTH), D_MODEL ** -0.5),
        'nat_rpb': nrm(ks[2], (DEPTH, NAT_HEADS, 2 * NAT_WIN_ROWS - 1, 2 * NAT_WIN_COLS - 1), 0.02),
        'nat_norm_g': 1.0 + nrm(ks[3], (DEPTH, NAT_WIDTH), 0.01),
        'diff_lambda': nrm(ks[4], (DEPTH, 4, DIFF_QK_DIM), 0.1),
        'diff_subln_g': 1.0 + nrm(ks[5], (DEPTH, HEAD_DIM), 0.01),
        't5_table': nrm(ks[6], (REL_BUCKETS, DIFF_HEADS), 0.1),
        'ml_conv_w': nrm(ks[7], (DEPTH, MLSTM_CONV, 2 * MLSTM_WIDTH), MLSTM_CONV ** -0.5),
        'ml_gate_b': ml_gate_b,
        'ml_norm_g': 1.0 + nrm(ks[9], (DEPTH, MLSTM_WIDTH), 0.01),
        'w_out': nrm(ks[10], (DEPTH, MIX_WIDTH, D_MODEL), beta * MIX_WIDTH ** -0.5),
        'ln_mix_g': 1.0 + nrm(ks[11], (DEPTH, D_MODEL), 0.01),
        'ln_mix_b': nrm(ks[12], (DEPTH, D_MODEL), 0.01),
        'router_w': nrm(ks[13], (DEPTH, D_MODEL, N_EXPERTS), D_MODEL ** -0.5),
        'router_b': nrm(ks[14], (DEPTH, N_EXPERTS), 0.01),
        'exp_w_glu': nrm(ks[15], (DEPTH, N_EXPERTS, D_MODEL, EXPERT_FF), D_MODEL ** -0.5),
        'exp_b_glu': nrm(ks[16], (DEPTH, N_EXPERTS, EXPERT_FF), 0.01),
        'exp_w_lin': nrm(ks[17], (DEPTH, N_EXPERTS, D_MODEL, EXPERT_FF), D_MODEL ** -0.5),
        'exp_b_lin': nrm(ks[18], (DEPTH, N_EXPERTS, EXPERT_FF), 0.01),
        'exp_w_down': nrm(ks[19], (DEPTH, N_EXPERTS, EXPERT_FF, D_MODEL), beta * EXPERT_FF ** -0.5),
        'exp_b_down': nrm(ks[20], (DEPTH, N_EXPERTS, D_MODEL), 0.01),
        'ln_ffn_g': 1.0 + nrm(ks[21], (DEPTH, D_MODEL), 0.01),
        'ln_ffn_b': nrm(ks[22], (DEPTH, D_MODEL), 0.01),
    }


def reference(x, w_in, nat_rpb, nat_norm_g, diff_lambda, diff_subln_g, t5_table, ml_conv_w, ml_gate_b,
              ml_norm_g, w_out, ln_mix_g, ln_mix_b, router_w, router_b, exp_w_glu, exp_b_glu, exp_w_lin,
              exp_b_lin, exp_w_down, exp_b_down, ln_ffn_g, ln_ffn_b):
    B, T, _ = x.shape
    alpha = (2.0 * DEPTH) ** 0.25
    sizes = (NAT_WIDTH,) * 3 + (DIFF_WIDTH,) * 3 + (MLSTM_WIDTH,) * 4
    cuts = []
    off = 0
    for s in sizes:
        off += s
        cuts.append(off)
    for l in range(DEPTH):
        u = jnp.einsum('btd,de->bte', x, w_in[l])
        nq, nk, nv, dq, dk, dv, mq, mk, mv, mo, mg = jnp.split(u, cuts, axis=-1)
        nat_h = lambda t: t.reshape(B, T, NAT_HEADS, HEAD_DIM)
        nat = neighbourhood_attention(nat_h(nq), nat_h(nk), nat_h(nv), nat_rpb[l])
        nat = head_rms_norm(nat, nat_norm_g[l].reshape(NAT_HEADS, HEAD_DIM))
        lam_init = 0.8 - 0.6 * math.exp(-0.3 * l)
        lp = diff_lambda[l].astype(jnp.float32)
        lam = jnp.exp(jnp.sum(lp[0] * lp[1])) - jnp.exp(jnp.sum(lp[2] * lp[3])) + lam_init
        dif = differential_attention(dq.reshape(B, T, DIFF_HEADS, 2, DIFF_QK_DIM),
                                     dk.reshape(B, T, DIFF_HEADS, 2, DIFF_QK_DIM),
                                     dv.reshape(B, T, DIFF_HEADS, HEAD_DIM), t5_table, lam)
        dif = head_rms_norm(dif, diff_subln_g[l]) * (1.0 - lam_init)
        ml = mlstm_mixer(mq, mk, mv, mo, mg, ml_conv_w[l], ml_gate_b[l], ml_norm_g[l])
        mix = jnp.concatenate([nat.reshape(B, T, NAT_WIDTH), dif.reshape(B, T, DIFF_WIDTH),
                               ml.reshape(B, T, MLSTM_WIDTH)], axis=-1).astype(x.dtype)
        y = jnp.einsum('bte,ed->btd', mix, w_out[l])
        x = layer_norm(alpha * x + y, ln_mix_g[l], ln_mix_b[l])
        f = moe_ffn(x, router_w[l], router_b[l], exp_w_glu[l], exp_b_glu[l], exp_w_lin[l], exp_b_lin[l],
                    exp_w_down[l], exp_b_down[l])
        x = layer_norm(alpha * x + f, ln_ffn_g[l], ln_ffn_b[l])
    return x
```

```python
import functools
import math

import numpy as np
import jax
import jax.numpy as jnp
from jax import lax
from jax.experimental import pallas as pl
from jax.experimental.pallas import tpu as pltpu

DEPTH = 2
HEAD_DIM = 128
NAT_HEADS = 16
DIFF_HEADS = 8
MLSTM_HEADS = 8
NAT_WIDTH = NAT_HEADS * HEAD_DIM
DIFF_WIDTH = DIFF_HEADS * HEAD_DIM
MLSTM_WIDTH = MLSTM_HEADS * HEAD_DIM
DIFF_QK_DIM = HEAD_DIM // 2
MAIN_WIDTH = 3 * NAT_WIDTH + 3 * DIFF_WIDTH + 4 * MLSTM_WIDTH
GRID_W = 64
NAT_WIN_ROWS = 8
NAT_WIN_COLS = 16
NAT_Q_ROWS = 2
NAT_KEY_ROWS = 10
REL_BUCKETS = 32
REL_MAX_DIST = 128
MLSTM_CHUNK = 64
MLSTM_CONV = 5
N_EXPERTS = 32
TOP_K = 4
SWIGLU_ALPHA = 1.702
SWIGLU_LIMIT = 7.0
MOE_BLOCK = 512
LN_EPS = 1e-5
ALPHA = (2.0 * DEPTH) ** 0.25
NEG_BIG = -1e30

LANES = 128
VMEM_LIMIT = 56 * 1024 * 1024

BF16 = jnp.bfloat16
F32 = jnp.float32


def _cparams(sem):
    return pltpu.CompilerParams(dimension_semantics=sem, vmem_limit_bytes=VMEM_LIMIT)


def _mm_kernel(x_ref, w_ref, o_ref):
    o_ref[...] = jnp.dot(x_ref[...], w_ref[...], preferred_element_type=F32).astype(o_ref.dtype)


def _matmul(x, w, *, bm, bn, out_dtype, name):
    m, k = x.shape
    n = w.shape[1]
    return pl.pallas_call(
        _mm_kernel,
        grid=(m // bm, n // bn),
        in_specs=[pl.BlockSpec((bm, k), lambda i, j: (i, 0)),
                  pl.BlockSpec((k, bn), lambda i, j: (0, j))],
        out_specs=pl.BlockSpec((bm, bn), lambda i, j: (i, j)),
        out_shape=jax.ShapeDtypeStruct((m, n), out_dtype),
        compiler_params=_cparams(("parallel", "parallel")),
        name=name,
    )(x, w)


def _proj_ln_kernel(*refs, n_parts, widths, bn, nj):
    part_refs = refs[:n_parts]
    w_ref, x_ref, g_ref, b_ref, o32_ref, o16_ref, mix_ref = refs[n_parts:]
    j = pl.program_id(1)

    @pl.when(j == 0)
    def _():
        off = 0
        for p_ref, wd in zip(part_refs, widths):
            mix_ref[:, off:off + wd] = p_ref[...]
            off += wd

    z = ALPHA * x_ref[...] + jnp.dot(mix_ref[...], w_ref[...], preferred_element_type=F32)
    for jj in range(nj):
        @pl.when(j == jj)
        def _(jj=jj):
            o32_ref[:, jj * bn:(jj + 1) * bn] = z

    @pl.when(j == nj - 1)
    def _():
        zz = o32_ref[...]
        mu = jnp.mean(zz, axis=-1, keepdims=True)
        zc = zz - mu
        var = jnp.mean(zc * zc, axis=-1, keepdims=True)
        y = zc * lax.rsqrt(var + LN_EPS) * g_ref[...] + b_ref[...]
        o32_ref[...] = y
        o16_ref[...] = y.astype(BF16)


def _proj_residual_ln(parts, w, x_res, g, b, *, bm, bn):
    t, d = x_res.shape
    widths = tuple(p.shape[1] for p in parts)
    kdim = sum(widths)
    nj = d // bn
    kern = functools.partial(_proj_ln_kernel, n_parts=len(parts), widths=widths, bn=bn, nj=nj)
    in_specs = [pl.BlockSpec((bm, wd), lambda i, j: (i, 0)) for wd in widths]
    in_specs += [pl.BlockSpec((kdim, bn), lambda i, j: (0, j)),
                 pl.BlockSpec((bm, bn), lambda i, j: (i, j)),
                 pl.BlockSpec((1, d), lambda i, j: (0, 0)),
                 pl.BlockSpec((1, d), lambda i, j: (0, 0))]
    return pl.pallas_call(
        kern,
        grid=(t // bm, nj),
        in_specs=in_specs,
        out_specs=[pl.BlockSpec((bm, d), lambda i, j: (i, 0)),
                   pl.BlockSpec((bm, d), lambda i, j: (i, 0))],
        out_shape=[jax.ShapeDtypeStruct((t, d), F32), jax.ShapeDtypeStruct((t, d), BF16)],
        scratch_shapes=[pltpu.VMEM((bm, kdim), BF16)],
        compiler_params=_cparams(("parallel", "arbitrary")),
        name="proj_ln",
    )(*parts, w, x_res, g.reshape(1, d), b.reshape(1, d))


def _nat_geometry(t):
    rows = t // GRID_W
    win_r = min(NAT_WIN_ROWS, rows)
    kr = NAT_KEY_ROWS
    assert rows >= kr and rows % NAT_Q_ROWS == 0
    n_blk = rows // NAT_Q_ROWS
    col = np.arange(GRID_W)
    c0 = np.clip(col - NAT_WIN_COLS // 2, 0, GRID_W - NAT_WIN_COLS)
    keys, ks_list, ty_list, tables = {}, [], [], []
    for i in range(n_blk):
        qr = i * NAT_Q_ROWS + np.arange(NAT_Q_ROWS)
        r0 = np.clip(qr - win_r // 2, 0, rows - win_r)
        ks = int(np.clip(r0[0], 0, rows - kr))
        key = (tuple(qr - ks), tuple(r0 - ks))
        if key not in keys:
            keys[key] = len(tables)
            q_row = np.repeat(qr, GRID_W)
            q_r0 = np.repeat(r0, GRID_W)
            q_col = np.tile(col, NAT_Q_ROWS)
            q_c0 = np.tile(c0, NAT_Q_ROWS)
            k_row = np.repeat(ks + np.arange(kr), GRID_W)
            k_col = np.tile(col, kr)
            row_ok = (k_row[None, :] >= q_r0[:, None]) & (k_row[None, :] < q_r0[:, None] + win_r)
            col_ok = (k_col[None, :] >= q_c0[:, None]) & (k_col[None, :] < q_c0[:, None] + NAT_WIN_COLS)
            dr = np.clip(k_row[None, :] - q_row[:, None] + NAT_WIN_ROWS - 1, 0, 2 * NAT_WIN_ROWS - 2)
            dc = np.clip(k_col[None, :] - q_col[:, None] + NAT_WIN_COLS - 1, 0, 2 * NAT_WIN_COLS - 2)
            tables.append((dr, dc, row_ok & col_ok))
        ks_list.append(ks)
        ty_list.append(keys[key])
    dr = np.stack([tb[0] for tb in tables]).astype(np.int32)
    dc = np.stack([tb[1] for tb in tables]).astype(np.int32)
    ok = np.stack([tb[2] for tb in tables])
    return np.asarray(ks_list, np.int32), np.asarray(ty_list, np.int32), dr, dc, ok


def _nat_kernel(ks_ref, ty_ref, q_ref, k_ref, v_ref, bias_ref, g_ref, o_ref, *, blocks_per_step, kw):
    nq = NAT_Q_ROWS * GRID_W
    base = pl.program_id(1) * blocks_per_step
    scale = HEAD_DIM ** -0.5
    for jj in range(blocks_per_step):
        i = base + jj
        k0 = pl.multiple_of(ks_ref[i] * GRID_W, GRID_W)
        q = q_ref[jj * nq:(jj + 1) * nq, :]
        k = k_ref[pl.ds(k0, kw), :]
        v = v_ref[pl.ds(k0, kw), :]
        s = lax.dot_general(q, k, (((1,), (1,)), ((), ())), preferred_element_type=F32)
        s = s * scale + bias_ref[ty_ref[i], 0]
        m = jnp.max(s, axis=-1, keepdims=True)
        p = jnp.exp(s - m)
        l = jnp.sum(p, axis=-1, keepdims=True)
        o = jnp.dot(p.astype(BF16), v, preferred_element_type=F32) / l
        ms = jnp.mean(o * o, axis=-1, keepdims=True)
        o_ref[jj * nq:(jj + 1) * nq, :] = (o * lax.rsqrt(ms + LN_EPS) * g_ref[0]).astype(o_ref.dtype)


def _nat_attention(u, rpb, norm_g, *, blocks_per_step):
    t = u.shape[0]
    ks, ty, dr, dc, ok = _nat_geometry(t)
    n_blk = ks.shape[0]
    kw = NAT_KEY_ROWS * GRID_W
    nq = NAT_Q_ROWS * GRID_W
    bias = jnp.where(ok[:, None], jnp.transpose(rpb.astype(F32)[:, dr, dc], (1, 0, 2, 3)), NEG_BIG)
    n_type = bias.shape[0]
    g = norm_g.astype(F32).reshape(NAT_HEADS, 1, HEAD_DIM)
    rows_per_step = blocks_per_step * nq
    kern = functools.partial(_nat_kernel, blocks_per_step=blocks_per_step, kw=kw)
    grid_spec = pltpu.PrefetchScalarGridSpec(
        num_scalar_prefetch=2,
        grid=(NAT_HEADS, n_blk // blocks_per_step),
        in_specs=[pl.BlockSpec((rows_per_step, HEAD_DIM), lambda h, g_, *_: (g_, h)),
                  pl.BlockSpec((t, HEAD_DIM), lambda h, g_, *_: (0, NAT_HEADS + h)),
                  pl.BlockSpec((t, HEAD_DIM), lambda h, g_, *_: (0, 2 * NAT_HEADS + h)),
                  pl.BlockSpec((n_type, 1, nq, kw), lambda h, g_, *_: (0, h, 0, 0)),
                  pl.BlockSpec((1, 1, HEAD_DIM), lambda h, g_, *_: (h, 0, 0))],
        out_specs=pl.BlockSpec((rows_per_step, HEAD_DIM), lambda h, g_, *_: (g_, h)),
    )
    return pl.pallas_call(
        kern,
        grid_spec=grid_spec,
        out_shape=jax.ShapeDtypeStruct((t, NAT_WIDTH), BF16),
        compiler_params=_cparams(("parallel", "arbitrary")),
        name="nat_attn",
    )(jnp.asarray(ks), jnp.asarray(ty), u, u, u, bias, g)


def _t5_bucket(rel):
    half = REL_BUCKETS // 2
    max_exact = half // 2
    ret = jnp.where(rel > 0, half, 0)
    n = jnp.abs(rel)
    n_f = jnp.maximum(n, 1).astype(jnp.float32)
    large = max_exact + (jnp.log(n_f / max_exact) / math.log(REL_MAX_DIST / max_exact)
                         * (half - max_exact)).astype(jnp.int32)
    large = jnp.minimum(large, half - 1)
    return ret + jnp.where(n < max_exact, n, large)


def _diff_kernel(far_ref, q_ref, k_ref, v_ref, bias_ref, lp_ref, g_ref, o_ref,
                 qs_ref, m_ref, l_ref, acc_ref, *, bk, ratio, nk, lam_init):
    h = pl.program_id(0)
    qi = pl.program_id(1)
    bq = q_ref.shape[0]

    q = q_ref[...]
    lane = lax.broadcasted_iota(jnp.int32, q.shape, 1)
    qsc = q * jnp.asarray(DIFF_QK_DIM ** -0.5, q.dtype)
    zero = jnp.zeros_like(qsc)
    qs_ref[0:bq, :] = jnp.where(lane < DIFF_QK_DIM, qsc, zero)
    qs_ref[bq:2 * bq, :] = jnp.where(lane >= DIFF_QK_DIM, qsc, zero)
    m_ref[...] = jnp.full(m_ref.shape, -jnp.inf, F32)
    l_ref[...] = jnp.zeros(l_ref.shape, F32)
    acc_ref[...] = jnp.zeros(acc_ref.shape, F32)

    def step(ki, bias_tile, const):
        k0 = pl.multiple_of(ki * bk, bk)
        k = k_ref[pl.ds(k0, bk), :]
        v = v_ref[pl.ds(k0, bk), :]
        s = lax.dot_general(qs_ref[...], k, (((1,), (1,)), ((), ())), preferred_element_type=F32)
        if bias_tile is not None:
            s = (s.reshape(2, bq, bk) + bias_tile[None]).reshape(2 * bq, bk)
        mt = jnp.max(s, axis=-1, keepdims=True)
        if const is not None:
            mt = mt + const
        m_prev = m_ref[...]
        m_new = jnp.maximum(m_prev, mt)
        a = jnp.exp(m_prev - m_new)
        shift = m_new if const is None else m_new - const
        p = jnp.exp(s - shift)
        l_ref[...] = a * l_ref[...] + jnp.sum(p, axis=-1, keepdims=True)
        acc_ref[...] = a * acc_ref[...] + jnp.dot(p.astype(BF16), v, preferred_element_type=F32)
        m_ref[...] = m_new

    c_left = far_ref[h, 0]
    c_right = far_ref[h, 1]
    first_near = qi * ratio - 1
    n_near = ratio + 2

    def left_body(ki, carry):
        step(ki, None, c_left)
        return carry

    lax.fori_loop(0, jnp.clip(first_near, 0, nk), left_body, 0)
    for tt in range(n_near):
        ki = first_near + tt

        @pl.when((ki >= 0) & (ki < nk))
        def _(ki=ki, tt=tt):
            step(ki, bias_ref[0, tt], None)

    def right_body(ki, carry):
        step(ki, None, c_right)
        return carry

    lax.fori_loop(jnp.clip(first_near + n_near, 0, nk), nk, right_body, 0)

    o = acc_ref[...] / l_ref[...]
    lp = lp_ref[...]
    lam = (jnp.exp(jnp.sum(lp[0:1] * lp[1:2], axis=-1, keepdims=True))
           - jnp.exp(jnp.sum(lp[2:3] * lp[3:4], axis=-1, keepdims=True)) + lam_init)
    d = o[0:bq] - lam * o[bq:2 * bq]
    ms = jnp.mean(d * d, axis=-1, keepdims=True)
    o_ref[...] = (d * lax.rsqrt(ms + LN_EPS) * g_ref[...] * (1.0 - lam_init)).astype(o_ref.dtype)


def _diff_bias_tiles(t5_table, bq, bk):
    ratio = bq // bk
    span = bq + bk + 1
    rel = jnp.arange(-span, span + 1, dtype=jnp.int32)
    vec = jnp.take(t5_table.astype(F32), _t5_bucket(rel), axis=0)
    ii = np.arange(bq)[:, None]
    jj = np.arange(bk)[None, :]
    idx = np.stack([(tt - 1) * bk + jj - ii + span for tt in range(ratio + 2)]).astype(np.int32)
    tiles = jnp.transpose(vec[idx], (3, 0, 1, 2))
    far = jnp.stack([vec[0], vec[-1]], axis=1)
    return tiles, far


def _diff_attention(u, bias_tiles, far, lam_params, subln_g, lam_init, *, bq, bk, col0):
    t = u.shape[0]
    assert bk >= REL_MAX_DIST and bq % bk == 0
    ratio = bq // bk
    nk = t // bk
    n_near = ratio + 2
    kern = functools.partial(_diff_kernel, bk=bk, ratio=ratio, nk=nk, lam_init=lam_init)
    grid_spec = pltpu.PrefetchScalarGridSpec(
        num_scalar_prefetch=1,
        grid=(DIFF_HEADS, t // bq),
        in_specs=[pl.BlockSpec((bq, HEAD_DIM), lambda h, i, *_: (i, col0 + h)),
                  pl.BlockSpec((t, HEAD_DIM), lambda h, i, *_: (0, col0 + DIFF_HEADS + h)),
                  pl.BlockSpec((t, HEAD_DIM), lambda h, i, *_: (0, col0 + 2 * DIFF_HEADS + h)),
                  pl.BlockSpec((1, n_near, bq, bk), lambda h, i, *_: (h, 0, 0, 0)),
                  pl.BlockSpec((4, DIFF_QK_DIM), lambda h, i, *_: (0, 0)),
                  pl.BlockSpec((1, HEAD_DIM), lambda h, i, *_: (0, 0))],
        out_specs=pl.BlockSpec((bq, HEAD_DIM), lambda h, i, *_: (i, h)),
        scratch_shapes=[pltpu.VMEM((2 * bq, HEAD_DIM), BF16),
                        pltpu.VMEM((2 * bq, 1), F32),
                        pltpu.VMEM((2 * bq, 1), F32),
                        pltpu.VMEM((2 * bq, HEAD_DIM), F32)],
    )
    return pl.pallas_call(
        kern,
        grid_spec=grid_spec,
        out_shape=jax.ShapeDtypeStruct((t, DIFF_WIDTH), BF16),
        compiler_params=_cparams(("parallel", "arbitrary")),
        name="diff_attn",
    )(far, u, u, u, bias_tiles, lam_params.astype(F32), subln_g.astype(F32).reshape(1, HEAD_DIM))


def _head_layer_norm(x, g):
    xf = x.astype(jnp.float32)
    xc = xf - jnp.mean(xf, -1, keepdims=True)
    return xc * lax.rsqrt(jnp.mean(jnp.square(xc), -1, keepdims=True) + LN_EPS) * g.astype(jnp.float32)


def _mlstm_chunkwise(q, k, v, ig, lf):
    B, H, T, Dh = q.shape
    L = MLSTM_CHUNK
    nc = T // L
    q = q.reshape(B, H, nc, L, Dh)
    k = k.reshape(B, H, nc, L, Dh)
    v = v.reshape(B, H, nc, L, Dh)
    ig = ig.reshape(B, H, nc, L)
    b = jnp.cumsum(lf.reshape(B, H, nc, L), axis=-1)
    g = b[..., -1]
    tri = jnp.tril(jnp.ones((L, L), dtype=bool))
    D = jnp.where(tri, b[..., :, None] - b[..., None, :] + ig[..., None, :], -jnp.inf)
    a = g[..., None] - b + ig
    m_loc = jnp.max(a, axis=-1)
    w = jnp.exp(a - m_loc[..., None])
    C_loc = jnp.einsum('bhcsk,bhcsv->bhckv', k * w[..., None], v)
    n_loc = jnp.einsum('bhcs,bhcsk->bhck', w, k)

    def step(carry, inp):
        C, n, m = carry
        C_l, n_l, m_l, g_c = inp
        m_new = jnp.maximum(g_c + m, m_l)
        s_old = jnp.exp(g_c + m - m_new)
        s_loc = jnp.exp(m_l - m_new)
        C_new = s_old[..., None, None] * C + s_loc[..., None, None] * C_l
        n_new = s_old[..., None] * n + s_loc[..., None] * n_l
        return (C_new, n_new, m_new), (C, n, m)

    init = (jnp.zeros((B, H, Dh, Dh), q.dtype), jnp.zeros((B, H, Dh), q.dtype), jnp.zeros((B, H), q.dtype))
    xs = (jnp.moveaxis(C_loc, 2, 0), jnp.moveaxis(n_loc, 2, 0), jnp.moveaxis(m_loc, 2, 0), jnp.moveaxis(g, 2, 0))
    _, (C_prev, n_prev, m_prev) = lax.scan(step, init, xs)
    C_prev = jnp.moveaxis(C_prev, 0, 2)
    n_prev = jnp.moveaxis(n_prev, 0, 2)
    m_prev = jnp.moveaxis(m_prev, 0, 2)
    inter_log = b + m_prev[..., None]
    m_t = jnp.maximum(inter_log, jnp.max(D, axis=-1))
    w_inter = jnp.exp(inter_log - m_t)
    P = jnp.exp(D - m_t[..., None]) * jnp.einsum('bhctd,bhcsd->bhcts', q, k)
    num = w_inter[..., None] * jnp.einsum('bhctk,bhckv->bhctv', q, C_prev) + jnp.einsum('bhcts,bhcsv->bhctv', P, v)
    den = w_inter * jnp.einsum('bhctk,bhck->bhct', q, n_prev) + jnp.sum(P, axis=-1)
    h = num / jnp.maximum(jnp.abs(den), jnp.exp(-m_t))[..., None]
    return h.reshape(B, H, T, Dh)


def _centred_depthwise_conv(x, w):
    C = x.shape[-1]
    W = w.shape[0]
    return lax.conv_general_dilated(x, w[:, None, :].astype(x.dtype), window_strides=(1,),
                                    padding=[(W // 2, W // 2)], dimension_numbers=('NWC', 'WIO', 'NWC'),
                                    feature_group_count=C)


def _mlstm_mixer(q, k, v, o, gates, conv_w, gate_b, norm_g):
    B, T, _ = q.shape
    qk = jax.nn.silu(_centred_depthwise_conv(jnp.concatenate([q, k], axis=-1), conv_w))
    q, k = jnp.split(qk, 2, axis=-1)
    heads = lambda t: jnp.transpose(t.astype(jnp.float32).reshape(B, T, MLSTM_HEADS, HEAD_DIM), (0, 2, 1, 3))
    qh = heads(q)
    kh = heads(k) * HEAD_DIM ** -0.5
    vh = heads(v)
    g = jnp.transpose((gates + gate_b).astype(jnp.float32).reshape(B, T, 4, MLSTM_HEADS), (2, 0, 3, 1))
    h_f = _mlstm_chunkwise(qh, kh, vh, g[0], jax.nn.log_sigmoid(g[1]))
    flip = lambda t: jnp.flip(t, axis=2)
    h_b = flip(_mlstm_chunkwise(flip(qh), flip(kh), flip(vh), flip(g[2]), flip(jax.nn.log_sigmoid(g[3]))))
    h = jnp.transpose(h_f + h_b, (0, 2, 1, 3))
    h = jax.nn.sigmoid(o.astype(jnp.float32)).reshape(B, T, MLSTM_HEADS, HEAD_DIM) * h
    return _head_layer_norm(h, norm_g.reshape(MLSTM_HEADS, HEAD_DIM))


def _moe_ffn(x, w_router, b_router, w_glu, b_glu, w_lin, b_lin, w_down, b_down):
    B, T, D = x.shape
    N = B * T
    xt = x.reshape(N, D)
    logits = (xt @ w_router + b_router).astype(jnp.float32)
    top_v, top_e = lax.top_k(logits, TOP_K)
    gate = jax.nn.softmax(top_v, axis=-1).astype(x.dtype)
    slot_e = top_e.reshape(-1)
    slot_tok = jnp.repeat(jnp.arange(N, dtype=jnp.int32), TOP_K)
    slot_gate = gate.reshape(-1)
    order = jnp.argsort(slot_e)
    e_sorted = slot_e[order]
    tok_sorted = slot_tok[order]
    gate_sorted = slot_gate[order]
    counts = jnp.bincount(slot_e, length=N_EXPERTS)
    padded = (counts + MOE_BLOCK - 1) // MOE_BLOCK * MOE_BLOCK
    start = jnp.cumsum(counts) - counts
    pad_end = jnp.cumsum(padded)
    pad_start = pad_end - padded
    dest = pad_start[e_sorted] + jnp.arange(N * TOP_K) - start[e_sorted]
    n_blocks = -(-(N * TOP_K) // MOE_BLOCK) + N_EXPERTS
    n_rows = n_blocks * MOE_BLOCK
    row_tok = jnp.zeros((n_rows,), jnp.int32).at[dest].set(tok_sorted)
    row_gate = jnp.zeros((n_rows,), x.dtype).at[dest].set(gate_sorted)
    blk_expert = jnp.minimum(jnp.searchsorted(pad_end, jnp.arange(n_blocks) * MOE_BLOCK, side='right'),
                             N_EXPERTS - 1)

    def body(y, inp):
        e, toks, gt = inp
        xb = xt[toks]
        hg = jnp.minimum(xb @ w_glu[e] + b_glu[e], SWIGLU_LIMIT)
        hl = jnp.clip(xb @ w_lin[e] + b_lin[e], -SWIGLU_LIMIT, SWIGLU_LIMIT)
        h = hg * jax.nn.sigmoid(SWIGLU_ALPHA * hg) * (hl + 1.0)
        out = (h @ w_down[e] + b_down[e]) * gt[:, None]
        return y.at[toks].add(out), None

    y, _ = lax.scan(body, jnp.zeros_like(xt),
                    (blk_expert, row_tok.reshape(n_blocks, MOE_BLOCK), row_gate.reshape(n_blocks, MOE_BLOCK)))
    return y.reshape(B, T, D)


def _layer_norm(x, g, b):
    xf = x.astype(jnp.float32)
    mu = jnp.mean(xf, -1, keepdims=True)
    var = jnp.mean(jnp.square(xf - mu), -1, keepdims=True)
    y = (xf - mu) * lax.rsqrt(var + LN_EPS) * g.astype(jnp.float32) + b.astype(jnp.float32)
    return y.astype(x.dtype)


def kernel(x, w_in, nat_rpb, nat_norm_g, diff_lambda, diff_subln_g, t5_table, ml_conv_w, ml_gate_b, ml_norm_g,
           w_out, ln_mix_g, ln_mix_b, router_w, router_b, exp_w_glu, exp_b_glu, exp_w_lin, exp_b_lin,
           exp_w_down, exp_b_down, ln_ffn_g, ln_ffn_b):
    bsz, t, d = x.shape
    assert bsz == 1
    x32 = x.reshape(t, d)
    x16 = x32.astype(BF16)
    diff_bq, diff_bk = 1024, 512
    bias_tiles, far = _diff_bias_tiles(t5_table, diff_bq, diff_bk)
    mo = 3 * NAT_WIDTH + 3 * DIFF_WIDTH
    for l in range(DEPTH):
        w_main = w_in[l][:, :MAIN_WIDTH].astype(BF16)
        w_gate = jnp.pad(w_in[l][:, MAIN_WIDTH:], ((0, 0), (0, LANES - 4 * MLSTM_HEADS))).astype(BF16)
        u = _matmul(x16, w_main, bm=1024, bn=1024, out_dtype=BF16, name="inproj")
        gates = _matmul(x16, w_gate, bm=1024, bn=LANES, out_dtype=F32, name="inproj_gates")[:, :4 * MLSTM_HEADS]
        nat = _nat_attention(u, nat_rpb[l], nat_norm_g[l], blocks_per_step=8)
        lam_init = 0.8 - 0.6 * math.exp(-0.3 * l)
        dif = _diff_attention(u, bias_tiles, far, diff_lambda[l], diff_subln_g[l], lam_init,
                              bq=diff_bq, bk=diff_bk, col0=3 * NAT_WIDTH // HEAD_DIM)
        um = u[:, mo:].astype(F32)[None]
        mq, mk, mv, mo_ = jnp.split(um, 4, axis=-1)
        ml = _mlstm_mixer(mq, mk, mv, mo_, gates[None], ml_conv_w[l], ml_gate_b[l], ml_norm_g[l])
        ml = ml.reshape(t, MLSTM_WIDTH).astype(BF16)
        x1_32, _ = _proj_residual_ln([nat, dif, ml], w_out[l].astype(BF16), x32, ln_mix_g[l], ln_mix_b[l],
                                     bm=256, bn=1024)
        f = _moe_ffn(x1_32[None], router_w[l], router_b[l], exp_w_glu[l], exp_b_glu[l], exp_w_lin[l],
                     exp_b_lin[l], exp_w_down[l], exp_b_down[l])
        x32 = _layer_norm(ALPHA * x1_32 + f[0], ln_ffn_g[l], ln_ffn_b[l])
        x16 = x32.astype(BF16)
    return x32.reshape(bsz, t, d)
```

```python
import functools
import math

import numpy as np
import jax
import jax.numpy as jnp
from jax import lax
from jax.experimental import pallas as pl
from jax.experimental.pallas import tpu as pltpu

DEPTH = 2
HEAD_DIM = 128
NAT_HEADS = 16
DIFF_HEADS = 8
MLSTM_HEADS = 8
NAT_WIDTH = NAT_HEADS * HEAD_DIM
DIFF_WIDTH = DIFF_HEADS * HEAD_DIM
MLSTM_WIDTH = MLSTM_HEADS * HEAD_DIM
DIFF_QK_DIM = HEAD_DIM // 2
MAIN_WIDTH = 3 * NAT_WIDTH + 3 * DIFF_WIDTH + 4 * MLSTM_WIDTH
GRID_W = 64
NAT_WIN_ROWS = 8
NAT_WIN_COLS = 16
NAT_Q_ROWS = 2
NAT_KEY_ROWS = 10
REL_BUCKETS = 32
REL_MAX_DIST = 128
MLSTM_CHUNK = 64
MLSTM_CONV = 5
N_EXPERTS = 32
TOP_K = 4
SWIGLU_ALPHA = 1.702
SWIGLU_LIMIT = 7.0
MOE_BLOCK = 512
LN_EPS = 1e-5
ALPHA = (2.0 * DEPTH) ** 0.25
NEG_BIG = -1e30

LANES = 128
VMEM_LIMIT = 56 * 1024 * 1024

BF16 = jnp.bfloat16
F32 = jnp.float32


def _cparams(sem):
    return pltpu.CompilerParams(dimension_semantics=sem, vmem_limit_bytes=VMEM_LIMIT)


def _mm_kernel(x_ref, w_ref, o_ref):
    o_ref[...] = jnp.dot(x_ref[...], w_ref[...], preferred_element_type=F32).astype(o_ref.dtype)


def _matmul(x, w, *, bm, bn, out_dtype, name):
    m, k = x.shape
    n = w.shape[1]
    return pl.pallas_call(
        _mm_kernel,
        grid=(m // bm, n // bn),
        in_specs=[pl.BlockSpec((bm, k), lambda i, j: (i, 0)),
                  pl.BlockSpec((k, bn), lambda i, j: (0, j))],
        out_specs=pl.BlockSpec((bm, bn), lambda i, j: (i, j)),
        out_shape=jax.ShapeDtypeStruct((m, n), out_dtype),
        compiler_params=_cparams(("parallel", "parallel")),
        name=name,
    )(x, w)


def _proj_ln_kernel(*refs, n_parts, widths, bn, nj):
    part_refs = refs[:n_parts]
    w_ref, x_ref, g_ref, b_ref, o32_ref, o16_ref, mix_ref = refs[n_parts:]
    j = pl.program_id(1)

    @pl.when(j == 0)
    def _():
        off = 0
        for p_ref, wd in zip(part_refs, widths):
            mix_ref[:, off:off + wd] = p_ref[...]
            off += wd

    z = ALPHA * x_ref[...] + jnp.dot(mix_ref[...], w_ref[...], preferred_element_type=F32)
    for jj in range(nj):
        @pl.when(j == jj)
        def _(jj=jj):
            o32_ref[:, jj * bn:(jj + 1) * bn] = z

    @pl.when(j == nj - 1)
    def _():
        zz = o32_ref[...]
        mu = jnp.mean(zz, axis=-1, keepdims=True)
        zc = zz - mu
        var = jnp.mean(zc * zc, axis=-1, keepdims=True)
        y = zc * lax.rsqrt(var + LN_EPS) * g_ref[...] + b_ref[...]
        o32_ref[...] = y
        o16_ref[...] = y.astype(BF16)


def _proj_residual_ln(parts, w, x_res, g, b, *, bm, bn):
    t, d = x_res.shape
    widths = tuple(p.shape[1] for p in parts)
    kdim = sum(widths)
    nj = d // bn
    kern = functools.partial(_proj_ln_kernel, n_parts=len(parts), widths=widths, bn=bn, nj=nj)
    in_specs = [pl.BlockSpec((bm, wd), lambda i, j: (i, 0)) for wd in widths]
    in_specs += [pl.BlockSpec((kdim, bn), lambda i, j: (0, j)),
                 pl.BlockSpec((bm, bn), lambda i, j: (i, j)),
                 pl.BlockSpec((1, d), lambda i, j: (0, 0)),
                 pl.BlockSpec((1, d), lambda i, j: (0, 0))]
    return pl.pallas_call(
        kern,
        grid=(t // bm, nj),
        in_specs=in_specs,
        out_specs=[pl.BlockSpec((bm, d), lambda i, j: (i, 0)),
                   pl.BlockSpec((bm, d), lambda i, j: (i, 0))],
        out_shape=[jax.ShapeDtypeStruct((t, d), F32), jax.ShapeDtypeStruct((t, d), BF16)],
        scratch_shapes=[pltpu.VMEM((bm, kdim), BF16)],
        compiler_params=_cparams(("parallel", "arbitrary")),
        name="proj_ln",
    )(*parts, w, x_res, g.reshape(1, d), b.reshape(1, d))


def _nat_geometry(t):
    rows = t // GRID_W
    win_r = min(NAT_WIN_ROWS, rows)
    kr = NAT_KEY_ROWS
    assert rows >= kr and rows % NAT_Q_ROWS == 0
    n_blk = rows // NAT_Q_ROWS
    col = np.arange(GRID_W)
    c0 = np.clip(col - NAT_WIN_COLS // 2, 0, GRID_W - NAT_WIN_COLS)
    keys, ks_list, ty_list, tables = {}, [], [], []
    for i in range(n_blk):
        qr = i * NAT_Q_ROWS + np.arange(NAT_Q_ROWS)
        r0 = np.clip(qr - win_r // 2, 0, rows - win_r)
        ks = int(np.clip(r0[0], 0, rows - kr))
        key = (tuple(qr - ks), tuple(r0 - ks))
        if key not in keys:
            keys[key] = len(tables)
            q_row = np.repeat(qr, GRID_W)
            q_r0 = np.repeat(r0, GRID_W)
            q_col = np.tile(col, NAT_Q_ROWS)
            q_c0 = np.tile(c0, NAT_Q_ROWS)
            k_row = np.repeat(ks + np.arange(kr), GRID_W)
            k_col = np.tile(col, kr)
            row_ok = (k_row[None, :] >= q_r0[:, None]) & (k_row[None, :] < q_r0[:, None] + win_r)
            col_ok = (k_col[None, :] >= q_c0[:, None]) & (k_col[None, :] < q_c0[:, None] + NAT_WIN_COLS)
            dr = np.clip(k_row[None, :] - q_row[:, None] + NAT_WIN_ROWS - 1, 0, 2 * NAT_WIN_ROWS - 2)
            dc = np.clip(k_col[None, :] - q_col[:, None] + NAT_WIN_COLS - 1, 0, 2 * NAT_WIN_COLS - 2)
            tables.append((dr, dc, row_ok & col_ok))
        ks_list.append(ks)
        ty_list.append(keys[key])
    dr = np.stack([tb[0] for tb in tables]).astype(np.int32)
    dc = np.stack([tb[1] for tb in tables]).astype(np.int32)
    ok = np.stack([tb[2] for tb in tables])
    return np.asarray(ks_list, np.int32), np.asarray(ty_list, np.int32), dr, dc, ok


def _nat_kernel(ks_ref, ty_ref, q_ref, k_ref, v_ref, bias_ref, g_ref, o_ref, *, blocks_per_step, kw):
    nq = NAT_Q_ROWS * GRID_W
    base = pl.program_id(1) * blocks_per_step
    scale = HEAD_DIM ** -0.5
    for jj in range(blocks_per_step):
        i = base + jj
        k0 = pl.multiple_of(ks_ref[i] * GRID_W, GRID_W)
        q = q_ref[jj * nq:(jj + 1) * nq, :]
        k = k_ref[pl.ds(k0, kw), :]
        v = v_ref[pl.ds(k0, kw), :]
        s = lax.dot_general(q, k, (((1,), (1,)), ((), ())), preferred_element_type=F32)
        s = s * scale + bias_ref[ty_ref[i], 0]
        m = jnp.max(s, axis=-1, keepdims=True)
        p = jnp.exp(s - m)
        l = jnp.sum(p, axis=-1, keepdims=True)
        o = jnp.dot(p.astype(BF16), v, preferred_element_type=F32) / l
        ms = jnp.mean(o * o, axis=-1, keepdims=True)
        o_ref[jj * nq:(jj + 1) * nq, :] = (o * lax.rsqrt(ms + LN_EPS) * g_ref[0]).astype(o_ref.dtype)


def _nat_attention(u, rpb, norm_g, *, blocks_per_step):
    t = u.shape[0]
    ks, ty, dr, dc, ok = _nat_geometry(t)
    n_blk = ks.shape[0]
    kw = NAT_KEY_ROWS * GRID_W
    nq = NAT_Q_ROWS * GRID_W
    n_type = ok.shape[0]
    pad_c = GRID_W - NAT_WIN_COLS
    toep = _toeplitz(jnp.pad(rpb.astype(F32), ((0, 0), (0, 0), (pad_c, pad_c))), GRID_W, GRID_W)
    dr_small = dr[:, ::GRID_W, ::GRID_W].reshape(-1)
    sel = toep[:, dr_small].reshape(NAT_HEADS, n_type, NAT_Q_ROWS, NAT_KEY_ROWS, GRID_W, GRID_W)
    sel = jnp.transpose(sel, (1, 0, 2, 4, 3, 5)).reshape(n_type, NAT_HEADS, nq, kw)
    bias = jnp.where(ok[:, None], sel, NEG_BIG)
    g = norm_g.astype(F32).reshape(NAT_HEADS, 1, HEAD_DIM)
    rows_per_step = blocks_per_step * nq
    kern = functools.partial(_nat_kernel, blocks_per_step=blocks_per_step, kw=kw)
    grid_spec = pltpu.PrefetchScalarGridSpec(
        num_scalar_prefetch=2,
        grid=(NAT_HEADS, n_blk // blocks_per_step),
        in_specs=[pl.BlockSpec((rows_per_step, HEAD_DIM), lambda h, g_, *_: (g_, h)),
                  pl.BlockSpec((t, HEAD_DIM), lambda h, g_, *_: (0, NAT_HEADS + h)),
                  pl.BlockSpec((t, HEAD_DIM), lambda h, g_, *_: (0, 2 * NAT_HEADS + h)),
                  pl.BlockSpec((n_type, 1, nq, kw), lambda h, g_, *_: (0, h, 0, 0)),
                  pl.BlockSpec((1, 1, HEAD_DIM), lambda h, g_, *_: (h, 0, 0))],
        out_specs=pl.BlockSpec((rows_per_step, HEAD_DIM), lambda h, g_, *_: (g_, h)),
    )
    return pl.pallas_call(
        kern,
        grid_spec=grid_spec,
        out_shape=jax.ShapeDtypeStruct((t, NAT_WIDTH), BF16),
        compiler_params=_cparams(("parallel", "arbitrary")),
        name="nat_attn",
    )(jnp.asarray(ks), jnp.asarray(ty), u, u, u, bias, g)


def _t5_bucket(rel):
    half = REL_BUCKETS // 2
    max_exact = half // 2
    ret = jnp.where(rel > 0, half, 0)
    n = jnp.abs(rel)
    n_f = jnp.maximum(n, 1).astype(jnp.float32)
    large = max_exact + (jnp.log(n_f / max_exact) / math.log(REL_MAX_DIST / max_exact)
                         * (half - max_exact)).astype(jnp.int32)
    large = jnp.minimum(large, half - 1)
    return ret + jnp.where(n < max_exact, n, large)


def _diff_kernel(far_ref, q_ref, k_ref, v_ref, bias_ref, lp_ref, g_ref, o_ref,
                 qs_ref, m_ref, acc_ref, *, bk, ratio, nk, lam_init, rb):
    h = pl.program_id(0)
    qi = pl.program_id(1)
    bq = q_ref.shape[0]
    n_rb = 2 * bq // rb

    q = q_ref[...]
    lane = lax.broadcasted_iota(jnp.int32, q.shape, 1)
    qsc = q * jnp.asarray(DIFF_QK_DIM ** -0.5, q.dtype)
    zero = jnp.zeros_like(qsc)
    qs_ref[0:bq, :] = jnp.where(lane < DIFF_QK_DIM, qsc, zero)
    qs_ref[bq:2 * bq, :] = jnp.where(lane >= DIFF_QK_DIM, qsc, zero)
    m_ref[...] = jnp.full(m_ref.shape, -jnp.inf, F32)
    acc_ref[...] = jnp.zeros(acc_ref.shape, F32)

    def step(ki, near_idx, const):
        k0 = pl.multiple_of(ki * bk, bk)
        k = k_ref[pl.ds(k0, bk), :]
        v = v_ref[pl.ds(k0, bk), :]
        v_ext = jnp.concatenate([v, jnp.ones_like(v)], axis=1)
        for r in range(n_rb):
            rows = slice(r * rb, (r + 1) * rb)
            s = lax.dot_general(qs_ref[rows, :], k, (((1,), (1,)), ((), ())), preferred_element_type=F32)
            if near_idx is not None:
                r0 = (r * rb) % bq
                s = s + bias_ref[0, near_idx, r0:r0 + rb, :]
            mt = jnp.max(s, axis=-1, keepdims=True)
            if const is not None:
                mt = mt + const
            m_prev = m_ref[rows, :]
            m_new = jnp.maximum(m_prev, mt)
            a = jnp.exp(m_prev - m_new)
            shift = m_new if const is None else m_new - const
            p = jnp.exp(s - jnp.tile(shift, (1, bk // LANES)))
            acc_ref[rows, :] = (jnp.tile(a, (1, 2)) * acc_ref[rows, :]
                                + jnp.dot(p.astype(BF16), v_ext, preferred_element_type=F32))
            m_ref[rows, :] = m_new

    c_left = far_ref[h, 0]
    c_right = far_ref[h, 1]
    first_near = qi * ratio - 1
    n_near = ratio + 2

    def left_body(ki, carry):
        step(ki, None, c_left)
        return carry

    lax.fori_loop(0, jnp.clip(first_near, 0, nk), left_body, 0)
    for tt in range(n_near):
        ki = first_near + tt

        @pl.when((ki >= 0) & (ki < nk))
        def _(ki=ki, tt=tt):
            step(ki, tt, None)

    def right_body(ki, carry):
        step(ki, None, c_right)
        return carry

    lax.fori_loop(jnp.clip(first_near + n_near, 0, nk), nk, right_body, 0)

    acc = acc_ref[...]
    o = acc[:, 0:HEAD_DIM] / acc[:, HEAD_DIM:2 * HEAD_DIM]
    lp = lp_ref[...]
    lam = (jnp.exp(jnp.sum(lp[0:1] * lp[1:2], axis=-1, keepdims=True))
           - jnp.exp(jnp.sum(lp[2:3] * lp[3:4], axis=-1, keepdims=True)) + lam_init)
    d = o[0:bq] - lam * o[bq:2 * bq]
    ms = jnp.mean(d * d, axis=-1, keepdims=True)
    o_ref[...] = (d * lax.rsqrt(ms + LN_EPS) * g_ref[...] * (1.0 - lam_init)).astype(o_ref.dtype)


def _diff_bias_tiles(t5_table, bq, bk):
    ratio = bq // bk
    span = bq + bk + 1
    rel = jnp.arange(-span, span + 1, dtype=jnp.int32)
    vec = jnp.take(t5_table.astype(F32), _t5_bucket(rel), axis=0)
    vec_t = vec.T
    diags = jnp.stack([vec_t[:, (tt - 1) * bk + span - (bq - 1):(tt - 1) * bk + span + bk]
                       for tt in range(ratio + 2)], axis=1)
    tiles = _toeplitz(diags, bq, bk)
    far = jnp.stack([vec[0], vec[-1]], axis=1)
    return tiles, far


def _toeplitz(w, n_rows, n_cols):
    period = n_rows + n_cols
    wp = jnp.pad(w, [(0, 0)] * (w.ndim - 1) + [(0, 1)])
    flat = jnp.tile(wp, n_rows)[..., :n_rows * (period - 1)]
    return flat.reshape(w.shape[:-1] + (n_rows, period - 1))[..., n_rows - 1:n_rows - 1 + n_cols]


def _diff_attention(u, bias_tiles, far, lam_params, subln_g, lam_init, *, bq, bk, rb, col0):
    t = u.shape[0]
    assert bk >= REL_MAX_DIST and bq % bk == 0
    ratio = bq // bk
    nk = t // bk
    n_near = ratio + 2
    assert bq % rb == 0
    kern = functools.partial(_diff_kernel, bk=bk, ratio=ratio, nk=nk, lam_init=lam_init, rb=rb)
    grid_spec = pltpu.PrefetchScalarGridSpec(
        num_scalar_prefetch=1,
        grid=(DIFF_HEADS, t // bq),
        in_specs=[pl.BlockSpec((bq, HEAD_DIM), lambda h, i, *_: (i, col0 + h)),
                  pl.BlockSpec((t, HEAD_DIM), lambda h, i, *_: (0, col0 + DIFF_HEADS + h)),
                  pl.BlockSpec((t, HEAD_DIM), lambda h, i, *_: (0, col0 + 2 * DIFF_HEADS + h)),
                  pl.BlockSpec((1, n_near, bq, bk), lambda h, i, *_: (h, 0, 0, 0)),
                  pl.BlockSpec((4, DIFF_QK_DIM), lambda h, i, *_: (0, 0)),
                  pl.BlockSpec((1, HEAD_DIM), lambda h, i, *_: (0, 0))],
        out_specs=pl.BlockSpec((bq, HEAD_DIM), lambda h, i, *_: (i, h)),
        scratch_shapes=[pltpu.VMEM((2 * bq, HEAD_DIM), BF16),
                        pltpu.VMEM((2 * bq, LANES), F32),
                        pltpu.VMEM((2 * bq, 2 * HEAD_DIM), F32)],
    )
    return pl.pallas_call(
        kern,
        grid_spec=grid_spec,
        out_shape=jax.ShapeDtypeStruct((t, DIFF_WIDTH), BF16),
        compiler_params=_cparams(("parallel", "arbitrary")),
        name="diff_attn",
    )(far, u, u, u, bias_tiles, lam_params.astype(F32), subln_g.astype(F32).reshape(1, HEAD_DIM))


CONV_HALO = 16


def _conv_silu_kernel(prev_ref, cur_ref, next_ref, w_ref, o_ref, *, k_scale):
    i = pl.program_id(0)
    j = pl.program_id(1)
    bt = cur_ref.shape[0]
    x = cur_ref[...].astype(F32)
    p = jnp.where(i > 0, prev_ref[...].astype(F32), 0.0)
    nx = jnp.where(i < pl.num_programs(0) - 1, next_ref[...].astype(F32), 0.0)
    ext = jnp.concatenate([p, x, nx], axis=0)
    n_ext = bt + 2 * CONV_HALO
    half = MLSTM_CONV // 2
    acc = w_ref[half:half + 1, :] * x
    for tap in range(MLSTM_CONV):
        s = tap - half
        if s == 0:
            continue
        rolled = pltpu.roll(ext, (n_ext - s) % n_ext, 0)
        acc = acc + w_ref[tap:tap + 1, :] * rolled[CONV_HALO:CONV_HALO + bt]
    y = acc * jax.nn.sigmoid(acc)
    o_ref[...] = (y * jnp.where(j == 1, k_scale, 1.0)).astype(o_ref.dtype)


def _conv_silu(u, conv_w, *, bt, col0):
    t = u.shape[0]
    w = MLSTM_WIDTH
    nb = t // bt
    hb = bt // CONV_HALO
    last = t // CONV_HALO - 1
    return pl.pallas_call(
        functools.partial(_conv_silu_kernel, k_scale=HEAD_DIM ** -0.5),
        grid=(nb, 2),
        in_specs=[pl.BlockSpec((CONV_HALO, w), lambda i, j: (jnp.maximum(i * hb - 1, 0), col0 + j)),
                  pl.BlockSpec((bt, w), lambda i, j: (i, col0 + j)),
                  pl.BlockSpec((CONV_HALO, w), lambda i, j: (jnp.minimum((i + 1) * hb, last), col0 + j)),
                  pl.BlockSpec((MLSTM_CONV, w), lambda i, j: (0, j))],
        out_specs=pl.BlockSpec((bt, w), lambda i, j: (i, j)),
        out_shape=jax.ShapeDtypeStruct((t, 2 * w), BF16),
        compiler_params=_cparams(("parallel", "parallel")),
        name="mlstm_conv",
    )(u, u, u, conv_w.astype(F32))


def _split3(x):
    x1 = x.astype(BF16)
    r1 = x - x1.astype(F32)
    x2 = r1.astype(BF16)
    x3 = (r1 - x2.astype(F32)).astype(BF16)
    return x1, x2, x3


def _log_sigmoid(x):
    return jnp.minimum(x, 0.0) - jnp.log(1.0 + jnp.exp(-jnp.abs(x)))


def _mlstm_dir(q, k, v, bcol, igcol, brow, igrow, c_ref, n_ref, m_ref, idx, fwd, lc):
    t_io = lax.broadcasted_iota(jnp.int32, (lc, lc), 0)
    s_io = lax.broadcasted_iota(jnp.int32, (lc, lc), 1)
    mask = (s_io <= t_io) if fwd else (s_io >= t_io)
    dmat = jnp.where(mask, bcol - brow + igrow, -jnp.inf)
    m_d = jnp.max(dmat, axis=-1, keepdims=True)
    m_prev = m_ref[idx]
    inter = bcol + m_prev
    m_t = jnp.maximum(inter, m_d)
    w_inter = jnp.exp(inter - m_t)
    s = lax.dot_general(q, k, (((1,), (1,)), ((), ())), preferred_element_type=F32)
    p = jnp.exp(dmat - m_t) * s
    num = (w_inter * jnp.dot(q, c_ref[idx].astype(BF16), preferred_element_type=F32)
           + jnp.dot(p.astype(BF16), v, preferred_element_type=F32))
    qn = jnp.sum(q.astype(F32) * n_ref[idx], axis=-1, keepdims=True)
    den = w_inter * qn + jnp.sum(p, axis=-1, keepdims=True)
    h = num / jnp.maximum(jnp.abs(den), jnp.exp(-m_t))
    g = bcol[lc - 1:lc] if fwd else bcol[0:1]
    a = g - bcol + igcol
    m_loc = jnp.max(a, axis=0, keepdims=True)
    kw = k.astype(F32) * jnp.exp(a - m_loc)
    c_loc = lax.dot_general(kw.astype(BF16), v, (((0,), (0,)), ((), ())), preferred_element_type=F32)
    n_loc = jnp.sum(kw, axis=0, keepdims=True)
    m_new = jnp.maximum(g + m_prev, m_loc)
    s_old = jnp.exp(g + m_prev - m_new)
    s_loc = jnp.exp(m_loc - m_new)
    c_ref[idx] = s_old * c_ref[idx] + s_loc * c_loc
    n_ref[idx] = s_old * n_ref[idx] + s_loc * n_loc
    m_ref[idx] = m_new
    return h


def _mlstm_kernel(qf_ref, kf_ref, vf_ref, gf_ref, qb_ref, kb_ref, vb_ref, gb_ref, gbias_ref,
                  hf_ref, hb_ref, c_ref, n_ref, m_ref, *, lc):
    nh = MLSTM_HEADS

    @pl.when(pl.program_id(0) == 0)
    def _():
        c_ref[...] = jnp.zeros(c_ref.shape, F32)
        n_ref[...] = jnp.zeros(n_ref.shape, F32)
        m_ref[...] = jnp.zeros(m_ref.shape, F32)

    t_io = lax.broadcasted_iota(jnp.int32, (lc, lc), 0)
    s_io = lax.broadcasted_iota(jnp.int32, (lc, lc), 1)
    for fwd in (True, False):
        g_ref, q_ref, k_ref, v_ref, h_ref = ((gf_ref, qf_ref, kf_ref, vf_ref, hf_ref) if fwd
                                             else (gb_ref, qb_ref, kb_ref, vb_ref, hb_ref))
        gates = g_ref[...] + gbias_ref[...]
        tri = jnp.where((s_io <= t_io) if fwd else (s_io >= t_io), 1.0, 0.0).astype(BF16)
        cum = None
        for piece in _split3(_log_sigmoid(gates)):
            part = jnp.dot(tri, piece, preferred_element_type=F32)
            cum = part if cum is None else cum + part
        gates_t = jnp.transpose(gates)
        cum_t = jnp.transpose(cum)
        lane0 = 0 if fwd else 2 * nh
        for hh in range(nh):
            li, lb = lane0 + hh, lane0 + nh + hh
            sl = slice(hh * HEAD_DIM, (hh + 1) * HEAD_DIM)
            h = _mlstm_dir(q_ref[:, sl], k_ref[:, sl], v_ref[:, sl],
                           cum[:, lb:lb + 1], gates[:, li:li + 1], cum_t[lb:lb + 1, :], gates_t[li:li + 1, :],
                           c_ref, n_ref, m_ref, (0 if fwd else nh) + hh, fwd, lc)
            h_ref[:, sl] = h


def _mlstm_scan(qk, u, gates, gate_b, *, lc, v_col):
    t = qk.shape[0]
    w = MLSTM_WIDTH
    nc = t // lc
    gbias = jnp.pad(gate_b.astype(F32), (0, LANES - gate_b.shape[0])).reshape(1, LANES)
    fmap = lambda col: (lambda c: (c, col))
    bmap = lambda col: (lambda c: (nc - 1 - c, col))
    nhd = 2 * MLSTM_HEADS
    return pl.pallas_call(
        functools.partial(_mlstm_kernel, lc=lc),
        grid=(nc,),
        in_specs=[pl.BlockSpec((lc, w), fmap(0)), pl.BlockSpec((lc, w), fmap(1)),
                  pl.BlockSpec((lc, w), fmap(v_col)), pl.BlockSpec((lc, LANES), fmap(0)),
                  pl.BlockSpec((lc, w), bmap(0)), pl.BlockSpec((lc, w), bmap(1)),
                  pl.BlockSpec((lc, w), bmap(v_col)), pl.BlockSpec((lc, LANES), bmap(0)),
                  pl.BlockSpec((1, LANES), lambda c: (0, 0))],
        out_specs=[pl.BlockSpec((lc, w), fmap(0)), pl.BlockSpec((lc, w), bmap(0))],
        out_shape=[jax.ShapeDtypeStruct((t, w), F32), jax.ShapeDtypeStruct((t, w), F32)],
        scratch_shapes=[pltpu.VMEM((nhd, HEAD_DIM, HEAD_DIM), F32),
                        pltpu.VMEM((nhd, 1, HEAD_DIM), F32),
                        pltpu.VMEM((nhd, 1, 1), F32)],
        compiler_params=_cparams(("arbitrary",)),
        name="mlstm_scan",
    )(qk, qk, u, gates, qk, qk, u, gates, gbias)


def _mlstm_out_kernel(hf_ref, hb_ref, o_ref, g_ref, out_ref):
    for hh in range(MLSTM_HEADS):
        sl = slice(hh * HEAD_DIM, (hh + 1) * HEAD_DIM)
        x = (hf_ref[:, sl] + hb_ref[:, sl]) * jax.nn.sigmoid(o_ref[:, sl].astype(F32))
        xc = x - jnp.mean(x, axis=-1, keepdims=True)
        y = xc * lax.rsqrt(jnp.mean(xc * xc, axis=-1, keepdims=True) + LN_EPS) * g_ref[:, sl]
        out_ref[:, sl] = y.astype(out_ref.dtype)


def _mlstm_out(hf, hb, u, norm_g, *, bt, o_col):
    t, w = hf.shape
    return pl.pallas_call(
        _mlstm_out_kernel,
        grid=(t // bt,),
        in_specs=[pl.BlockSpec((bt, w), lambda i: (i, 0)), pl.BlockSpec((bt, w), lambda i: (i, 0)),
                  pl.BlockSpec((bt, w), lambda i: (i, o_col)), pl.BlockSpec((1, w), lambda i: (0, 0))],
        out_specs=pl.BlockSpec((bt, w), lambda i: (i, 0)),
        out_shape=jax.ShapeDtypeStruct((t, w), BF16),
        compiler_params=_cparams(("parallel",)),
        name="mlstm_out",
    )(hf, hb, u, norm_g.astype(F32).reshape(1, w))


def _mlstm_block(u, gates, conv_w, gate_b, norm_g, *, lc, col0):
    qk = _conv_silu(u, conv_w, bt=512, col0=col0)
    hf, hb = _mlstm_scan(qk, u, gates, gate_b, lc=lc, v_col=col0 + 2)
    return _mlstm_out(hf, hb, u, norm_g, bt=512, o_col=col0 + 3)


def _router_kernel(x_ref, w_ref, b_ref, e_ref, g_ref):
    logits = jnp.dot(x_ref[...], w_ref[...], preferred_element_type=F32) + b_ref[...]
    lane = lax.broadcasted_iota(jnp.int32, logits.shape, 1)
    work = logits
    e_out = jnp.zeros(logits.shape, jnp.int32)
    vals = []
    for kk in range(TOP_K):
        m = jnp.max(work, axis=-1, keepdims=True)
        idx = jnp.min(jnp.where(work == m, lane, LANES), axis=-1, keepdims=True)
        e_out = jnp.where(lane == kk, idx, e_out)
        vals.append(m)
        work = jnp.where(lane == idx, -jnp.inf, work)
    ex = [jnp.exp(v - vals[0]) for v in vals]
    den = ex[0]
    for e in ex[1:]:
        den = den + e
    g_out = jnp.zeros(logits.shape, F32)
    for kk in range(TOP_K):
        g_out = jnp.where(lane == kk, ex[kk] / den, g_out)
    e_ref[...] = e_out
    g_ref[...] = g_out


def _router(x16, w_router, b_router, *, bm):
    t, d = x16.shape
    n_e = w_router.shape[1]
    w = jnp.pad(w_router, ((0, 0), (0, LANES - n_e))).astype(BF16)
    b = jnp.pad(b_router.astype(F32), (0, LANES - n_e), constant_values=NEG_BIG).reshape(1, LANES)
    return pl.pallas_call(
        _router_kernel,
        grid=(t // bm,),
        in_specs=[pl.BlockSpec((bm, d), lambda i: (i, 0)),
                  pl.BlockSpec((d, LANES), lambda i: (0, 0)),
                  pl.BlockSpec((1, LANES), lambda i: (0, 0))],
        out_specs=[pl.BlockSpec((bm, LANES), lambda i: (i, 0)),
                   pl.BlockSpec((bm, LANES), lambda i: (i, 0))],
        out_shape=[jax.ShapeDtypeStruct((t, LANES), jnp.int32), jax.ShapeDtypeStruct((t, LANES), F32)],
        compiler_params=_cparams(("parallel",)),
        name="router",
    )(x16, w, b)


def _dispatch_plan(top_e, n_experts, blk):
    t = top_e.shape[0]
    n_slots = t * TOP_K
    n_blocks = -(-n_slots // blk) + n_experts
    slot_e = top_e.reshape(-1)
    onehot = (slot_e[:, None] == jnp.arange(n_experts, dtype=jnp.int32)[None, :]).astype(jnp.int32)
    csum = jnp.cumsum(onehot, axis=0)
    rank = jnp.sum(csum * onehot, axis=1) - 1
    counts = csum[-1]
    padded = (counts + blk - 1) // blk * blk
    pad_end = jnp.cumsum(padded)
    pad_start = pad_end - padded
    dest = (jnp.sum(pad_start[None, :] * onehot, axis=1) + rank).astype(jnp.int32)
    slot_tok = jnp.arange(n_slots, dtype=jnp.int32) // TOP_K
    row_tok = jnp.zeros((n_blocks * blk,), jnp.int32).at[dest].set(slot_tok)
    n_act = (pad_end[-1] // blk).astype(jnp.int32)
    blk_id = jnp.minimum(jnp.arange(n_blocks, dtype=jnp.int32), n_act - 1)
    blk_expert = jnp.minimum(jnp.searchsorted(pad_end, blk_id * blk, side='right'), n_experts - 1).astype(jnp.int32)
    return row_tok, dest, blk_expert, n_act.reshape(1)


def _expert_kernel(be_ref, tok_ref, nact_ref, x_hbm, wg_ref, bg_ref, wl_ref, bl_ref, wd_ref, bd_ref, o_ref,
                   xg_ref, sem_ref, *, blk):
    b = pl.program_id(0)
    n_act = nact_ref[0]

    def issue(blk_idx, slot):
        def body(r, c):
            tok = tok_ref[blk_idx * blk + r]
            pltpu.make_async_copy(x_hbm.at[pl.ds(tok, 1), :], xg_ref.at[slot, pl.ds(r, 1), :],
                                  sem_ref.at[slot]).start()
            return c
        lax.fori_loop(0, blk, body, 0, unroll=8)

    @pl.when(b == 0)
    def _():
        issue(0, 0)

    @pl.when(b < n_act)
    def _():
        slot = lax.rem(b, 2)

        @pl.when(b + 1 < n_act)
        def _():
            issue(b + 1, 1 - slot)

        pltpu.make_async_copy(xg_ref.at[slot], xg_ref.at[slot], sem_ref.at[slot]).wait()
        xb = xg_ref[slot].astype(BF16)
        hg = jnp.minimum(jnp.dot(xb, wg_ref[0], preferred_element_type=F32) + bg_ref[0], SWIGLU_LIMIT)
        hl = jnp.clip(jnp.dot(xb, wl_ref[0], preferred_element_type=F32) + bl_ref[0], -SWIGLU_LIMIT, SWIGLU_LIMIT)
        hh = hg * jax.nn.sigmoid(SWIGLU_ALPHA * hg) * (hl + 1.0)
        o_ref[...] = jnp.dot(hh.astype(BF16), wd_ref[0], preferred_element_type=F32) + bd_ref[0]

    @pl.when(b >= n_act)
    def _():
        o_ref[...] = jnp.zeros(o_ref.shape, o_ref.dtype)


def _expert_ffn(x32, row_tok, blk_expert, n_act, w_glu, b_glu, w_lin, b_lin, w_down, b_down, *, blk):
    t, d = x32.shape
    n_e, _, ff = w_glu.shape
    n_blocks = blk_expert.shape[0]
    single = pl.Buffered(1)
    wmap = lambda i, be, tok, na: (be[i], 0, 0)
    grid_spec = pltpu.PrefetchScalarGridSpec(
        num_scalar_prefetch=3,
        grid=(n_blocks,),
        in_specs=[pl.BlockSpec(memory_space=pl.ANY),
                  pl.BlockSpec((1, d, ff), wmap, pipeline_mode=single),
                  pl.BlockSpec((1, 1, ff), wmap),
                  pl.BlockSpec((1, d, ff), wmap, pipeline_mode=single),
                  pl.BlockSpec((1, 1, ff), wmap),
                  pl.BlockSpec((1, ff, d), wmap, pipeline_mode=single),
                  pl.BlockSpec((1, 1, d), wmap)],
        out_specs=pl.BlockSpec((blk, d), lambda i, be, tok, na: (i, 0)),
        scratch_shapes=[pltpu.VMEM((2, blk, d), F32), pltpu.SemaphoreType.DMA((2,))],
    )
    return pl.pallas_call(
        functools.partial(_expert_kernel, blk=blk),
        grid_spec=grid_spec,
        out_shape=jax.ShapeDtypeStruct((n_blocks * blk, d), F32),
        compiler_params=_cparams(("arbitrary",)),
        name="expert_ffn",
    )(blk_expert, row_tok, n_act, x32, w_glu.astype(BF16), b_glu.astype(F32).reshape(n_e, 1, ff),
      w_lin.astype(BF16), b_lin.astype(F32).reshape(n_e, 1, ff),
      w_down.astype(BF16), b_down.astype(F32).reshape(n_e, 1, d))


def _combine_kernel(dest_ref, ys_hbm, x_ref, gate_ref, g_ref, b_ref, o32_ref, o16_ref, buf_ref, sem_ref, *, bt):
    i = pl.program_id(0)
    n = pl.num_programs(0)

    def issue(blk_idx, slot):
        def body(r, c):
            for kk in range(TOP_K):
                row = dest_ref[(blk_idx * bt + r) * TOP_K + kk]
                pltpu.make_async_copy(ys_hbm.at[pl.ds(row, 1), :], buf_ref.at[slot, kk, pl.ds(r, 1), :],
                                      sem_ref.at[slot]).start()
            return c
        lax.fori_loop(0, bt, body, 0, unroll=2)

    @pl.when(i == 0)
    def _():
        issue(0, 0)

    slot = lax.rem(i, 2)

    @pl.when(i + 1 < n)
    def _():
        issue(i + 1, 1 - slot)

    pltpu.make_async_copy(buf_ref.at[slot], buf_ref.at[slot], sem_ref.at[slot]).wait()
    gate = gate_ref[...]
    z = ALPHA * x_ref[...]
    for kk in range(TOP_K):
        z = z + gate[:, kk:kk + 1] * buf_ref[slot, kk]
    mu = jnp.mean(z, axis=-1, keepdims=True)
    zc = z - mu
    var = jnp.mean(zc * zc, axis=-1, keepdims=True)
    y = zc * lax.rsqrt(var + LN_EPS) * g_ref[...] + b_ref[...]
    o32_ref[...] = y
    o16_ref[...] = y.astype(BF16)


def _combine_ln(ys, dest, x32, gate, g, b, *, bt):
    t, d = x32.shape
    grid_spec = pltpu.PrefetchScalarGridSpec(
        num_scalar_prefetch=1,
        grid=(t // bt,),
        in_specs=[pl.BlockSpec(memory_space=pl.ANY),
                  pl.BlockSpec((bt, d), lambda i, dr: (i, 0)),
                  pl.BlockSpec((bt, LANES), lambda i, dr: (i, 0)),
                  pl.BlockSpec((1, d), lambda i, dr: (0, 0)),
                  pl.BlockSpec((1, d), lambda i, dr: (0, 0))],
        out_specs=[pl.BlockSpec((bt, d), lambda i, dr: (i, 0)),
                   pl.BlockSpec((bt, d), lambda i, dr: (i, 0))],
        scratch_shapes=[pltpu.VMEM((2, TOP_K, bt, d), F32), pltpu.SemaphoreType.DMA((2,))],
    )
    return pl.pallas_call(
        functools.partial(_combine_kernel, bt=bt),
        grid_spec=grid_spec,
        out_shape=[jax.ShapeDtypeStruct((t, d), F32), jax.ShapeDtypeStruct((t, d), BF16)],
        compiler_params=_cparams(("arbitrary",)),
        name="moe_combine_ln",
    )(dest, ys, x32, gate, g.astype(F32).reshape(1, d), b.astype(F32).reshape(1, d))


def _moe_block(x32, x16, w_router, b_router, w_glu, b_glu, w_lin, b_lin, w_down, b_down, ln_g, ln_b,
               *, blk, bt, router_bm):
    top_e, gate = _router(x16, w_router, b_router, bm=router_bm)
    row_tok, dest, blk_expert, n_act = _dispatch_plan(top_e[:, :TOP_K], w_glu.shape[0], blk)
    ys = _expert_ffn(x32, row_tok, blk_expert, n_act, w_glu, b_glu, w_lin, b_lin, w_down, b_down, blk=blk)
    return _combine_ln(ys, dest, x32, gate, ln_g, ln_b, bt=bt)


def _head_layer_norm(x, g):
    xf = x.astype(jnp.float32)
    xc = xf - jnp.mean(xf, -1, keepdims=True)
    return xc * lax.rsqrt(jnp.mean(jnp.square(xc), -1, keepdims=True) + LN_EPS) * g.astype(jnp.float32)


def _mlstm_chunkwise(q, k, v, ig, lf):
    B, H, T, Dh = q.shape
    L = MLSTM_CHUNK
    nc = T // L
    q = q.reshape(B, H, nc, L, Dh)
    k = k.reshape(B, H, nc, L, Dh)
    v = v.reshape(B, H, nc, L, Dh)
    ig = ig.reshape(B, H, nc, L)
    b = jnp.cumsum(lf.reshape(B, H, nc, L), axis=-1)
    g = b[..., -1]
    tri = jnp.tril(jnp.ones((L, L), dtype=bool))
    D = jnp.where(tri, b[..., :, None] - b[..., None, :] + ig[..., None, :], -jnp.inf)
    a = g[..., None] - b + ig
    m_loc = jnp.max(a, axis=-1)
    w = jnp.exp(a - m_loc[..., None])
    C_loc = jnp.einsum('bhcsk,bhcsv->bhckv', k * w[..., None], v)
    n_loc = jnp.einsum('bhcs,bhcsk->bhck', w, k)

    def step(carry, inp):
        C, n, m = carry
        C_l, n_l, m_l, g_c = inp
        m_new = jnp.maximum(g_c + m, m_l)
        s_old = jnp.exp(g_c + m - m_new)
        s_loc = jnp.exp(m_l - m_new)
        C_new = s_old[..., None, None] * C + s_loc[..., None, None] * C_l
        n_new = s_old[..., None] * n + s_loc[..., None] * n_l
        return (C_new, n_new, m_new), (C, n, m)

    init = (jnp.zeros((B, H, Dh, Dh), q.dtype), jnp.zeros((B, H, Dh), q.dtype), jnp.zeros((B, H), q.dtype))
    xs = (jnp.moveaxis(C_loc, 2, 0), jnp.moveaxis(n_loc, 2, 0), jnp.moveaxis(m_loc, 2, 0), jnp.moveaxis(g, 2, 0))
    _, (C_prev, n_prev, m_prev) = lax.scan(step, init, xs)
    C_prev = jnp.moveaxis(C_prev, 0, 2)
    n_prev = jnp.moveaxis(n_prev, 0, 2)
    m_prev = jnp.moveaxis(m_prev, 0, 2)
    inter_log = b + m_prev[..., None]
    m_t = jnp.maximum(inter_log, jnp.max(D, axis=-1))
    w_inter = jnp.exp(inter_log - m_t)
    P = jnp.exp(D - m_t[..., None]) * jnp.einsum('bhctd,bhcsd->bhcts', q, k)
    num = w_inter[..., None] * jnp.einsum('bhctk,bhckv->bhctv', q, C_prev) + jnp.einsum('bhcts,bhcsv->bhctv', P, v)
    den = w_inter * jnp.einsum('bhctk,bhck->bhct', q, n_prev) + jnp.sum(P, axis=-1)
    h = num / jnp.maximum(jnp.abs(den), jnp.exp(-m_t))[..., None]
    return h.reshape(B, H, T, Dh)


def _centred_depthwise_conv(x, w):
    C = x.shape[-1]
    W = w.shape[0]
    return lax.conv_general_dilated(x, w[:, None, :].astype(x.dtype), window_strides=(1,),
                                    padding=[(W // 2, W // 2)], dimension_numbers=('NWC', 'WIO', 'NWC'),
                                    feature_group_count=C)


def _mlstm_mixer(q, k, v, o, gates, conv_w, gate_b, norm_g):
    B, T, _ = q.shape
    qk = jax.nn.silu(_centred_depthwise_conv(jnp.concatenate([q, k], axis=-1), conv_w))
    q, k = jnp.split(qk, 2, axis=-1)
    heads = lambda t: jnp.transpose(t.astype(jnp.float32).reshape(B, T, MLSTM_HEADS, HEAD_DIM), (0, 2, 1, 3))
    qh = heads(q)
    kh = heads(k) * HEAD_DIM ** -0.5
    vh = heads(v)
    g = jnp.transpose((gates + gate_b).astype(jnp.float32).reshape(B, T, 4, MLSTM_HEADS), (2, 0, 3, 1))
    h_f = _mlstm_chunkwise(qh, kh, vh, g[0], jax.nn.log_sigmoid(g[1]))
    flip = lambda t: jnp.flip(t, axis=2)
    h_b = flip(_mlstm_chunkwise(flip(qh), flip(kh), flip(vh), flip(g[2]), flip(jax.nn.log_sigmoid(g[3]))))
    h = jnp.transpose(h_f + h_b, (0, 2, 1, 3))
    h = jax.nn.sigmoid(o.astype(jnp.float32)).reshape(B, T, MLSTM_HEADS, HEAD_DIM) * h
    return _head_layer_norm(h, norm_g.reshape(MLSTM_HEADS, HEAD_DIM))


def _moe_ffn(x, w_router, b_router, w_glu, b_glu, w_lin, b_lin, w_down, b_down):
    B, T, D = x.shape
    N = B * T
    xt = x.reshape(N, D)
    logits = (xt @ w_router + b_router).astype(jnp.float32)
    top_v, top_e = lax.top_k(logits, TOP_K)
    gate = jax.nn.softmax(top_v, axis=-1).astype(x.dtype)
    slot_e = top_e.reshape(-1)
    slot_tok = jnp.repeat(jnp.arange(N, dtype=jnp.int32), TOP_K)
    slot_gate = gate.reshape(-1)
    order = jnp.argsort(slot_e)
    e_sorted = slot_e[order]
    tok_sorted = slot_tok[order]
    gate_sorted = slot_gate[order]
    counts = jnp.bincount(slot_e, length=N_EXPERTS)
    padded = (counts + MOE_BLOCK - 1) // MOE_BLOCK * MOE_BLOCK
    start = jnp.cumsum(counts) - counts
    pad_end = jnp.cumsum(padded)
    pad_start = pad_end - padded
    dest = pad_start[e_sorted] + jnp.arange(N * TOP_K) - start[e_sorted]
    n_blocks = -(-(N * TOP_K) // MOE_BLOCK) + N_EXPERTS
    n_rows = n_blocks * MOE_BLOCK
    row_tok = jnp.zeros((n_rows,), jnp.int32).at[dest].set(tok_sorted)
    row_gate = jnp.zeros((n_rows,), x.dtype).at[dest].set(gate_sorted)
    blk_expert = jnp.minimum(jnp.searchsorted(pad_end, jnp.arange(n_blocks) * MOE_BLOCK, side='right'),
                             N_EXPERTS - 1)

    def body(y, inp):
        e, toks, gt = inp
        xb = xt[toks]
        hg = jnp.minimum(xb @ w_glu[e] + b_glu[e], SWIGLU_LIMIT)
        hl = jnp.clip(xb @ w_lin[e] + b_lin[e], -SWIGLU_LIMIT, SWIGLU_LIMIT)
        h = hg * jax.nn.sigmoid(SWIGLU_ALPHA * hg) * (hl + 1.0)
        out = (h @ w_down[e] + b_down[e]) * gt[:, None]
        return y.at[toks].add(out), None

    y, _ = lax.scan(body, jnp.zeros_like(xt),
                    (blk_expert, row_tok.reshape(n_blocks, MOE_BLOCK), row_gate.reshape(n_blocks, MOE_BLOCK)))
    return y.reshape(B, T, D)


def _layer_norm(x, g, b):
    xf = x.astype(jnp.float32)
    mu = jnp.mean(xf, -1, keepdims=True)
    var = jnp.mean(jnp.square(xf - mu), -1, keepdims=True)
    y = (xf - mu) * lax.rsqrt(var + LN_EPS) * g.astype(jnp.float32) + b.astype(jnp.float32)
    return y.astype(x.dtype)


def kernel(x, w_in, nat_rpb, nat_norm_g, diff_lambda, diff_subln_g, t5_table, ml_conv_w, ml_gate_b, ml_norm_g,
           w_out, ln_mix_g, ln_mix_b, router_w, router_b, exp_w_glu, exp_b_glu, exp_w_lin, exp_b_lin,
           exp_w_down, exp_b_down, ln_ffn_g, ln_ffn_b):
    bsz, t, d = x.shape
    assert bsz == 1
    x32 = x.reshape(t, d)
    x16 = x32.astype(BF16)
    diff_bq, diff_bk = 1024, 512
    bias_tiles, far = _diff_bias_tiles(t5_table, diff_bq, diff_bk)
    ml_col0 = (3 * NAT_WIDTH + 3 * DIFF_WIDTH) // MLSTM_WIDTH
    for l in range(DEPTH):
        w_main = w_in[l][:, :MAIN_WIDTH].astype(BF16)
        w_gate = jnp.pad(w_in[l][:, MAIN_WIDTH:], ((0, 0), (0, LANES - 4 * MLSTM_HEADS))).astype(BF16)
        u = _matmul(x16, w_main, bm=1024, bn=1024, out_dtype=BF16, name="inproj")
        gates = _matmul(x16, w_gate, bm=1024, bn=LANES, out_dtype=F32, name="inproj_gates")
        nat = _nat_attention(u, nat_rpb[l], nat_norm_g[l], blocks_per_step=8)
        lam_init = 0.8 - 0.6 * math.exp(-0.3 * l)
        dif = _diff_attention(u, bias_tiles, far, diff_lambda[l], diff_subln_g[l], lam_init,
                              bq=diff_bq, bk=diff_bk, rb=512, col0=3 * NAT_WIDTH // HEAD_DIM)
        ml = _mlstm_block(u, gates, ml_conv_w[l], ml_gate_b[l], ml_norm_g[l], lc=256, col0=ml_col0)
        x1_32, x1_16 = _proj_residual_ln([nat, dif, ml], w_out[l].astype(BF16), x32, ln_mix_g[l], ln_mix_b[l],
                                         bm=256, bn=1024)
        x32, x16 = _moe_block(x1_32, x1_16, router_w[l], router_b[l], exp_w_glu[l], exp_b_glu[l], exp_w_lin[l],
                              exp_b_lin[l], exp_w_down[l], exp_b_down[l], ln_ffn_g[l], ln_ffn_b[l],
                              blk=256, bt=128, router_bm=512)
    return x32.reshape(bsz, t, d)
```

```python
import functools
import math

import numpy as np
import jax
import jax.numpy as jnp
from jax import lax
from jax.experimental import pallas as pl
from jax.experimental.pallas import tpu as pltpu

DEPTH = 2
HEAD_DIM = 128
NAT_HEADS = 16
DIFF_HEADS = 8
MLSTM_HEADS = 8
NAT_WIDTH = NAT_HEADS * HEAD_DIM
DIFF_WIDTH = DIFF_HEADS * HEAD_DIM
MLSTM_WIDTH = MLSTM_HEADS * HEAD_DIM
DIFF_QK_DIM = HEAD_DIM // 2
MAIN_WIDTH = 3 * NAT_WIDTH + 3 * DIFF_WIDTH + 4 * MLSTM_WIDTH
GRID_W = 64
NAT_WIN_ROWS = 8
NAT_WIN_COLS = 16
NAT_Q_ROWS = 2
NAT_KEY_ROWS = 10
REL_BUCKETS = 32
REL_MAX_DIST = 128
MLSTM_CHUNK = 64
MLSTM_CONV = 5
N_EXPERTS = 32
TOP_K = 4
SWIGLU_ALPHA = 1.702
SWIGLU_LIMIT = 7.0
MOE_BLOCK = 512
LN_EPS = 1e-5
ALPHA = (2.0 * DEPTH) ** 0.25
NEG_BIG = -1e30

LANES = 128
VMEM_LIMIT = 56 * 1024 * 1024

BF16 = jnp.bfloat16
F32 = jnp.float32


def _cparams(sem):
    return pltpu.CompilerParams(dimension_semantics=sem, vmem_limit_bytes=VMEM_LIMIT)


def _mm_kernel(x_ref, w_ref, o_ref):
    o_ref[...] = jnp.dot(x_ref[...], w_ref[...], preferred_element_type=F32).astype(o_ref.dtype)


def _matmul(x, w, *, bm, bn, out_dtype, name):
    m, k = x.shape
    n = w.shape[1]
    return pl.pallas_call(
        _mm_kernel,
        grid=(m // bm, n // bn),
        in_specs=[pl.BlockSpec((bm, k), lambda i, j: (i, 0)),
                  pl.BlockSpec((k, bn), lambda i, j: (0, j))],
        out_specs=pl.BlockSpec((bm, bn), lambda i, j: (i, j)),
        out_shape=jax.ShapeDtypeStruct((m, n), out_dtype),
        compiler_params=_cparams(("parallel", "parallel")),
        name=name,
    )(x, w)


def _proj_ln_kernel(*refs, n_parts, widths, bn, nj):
    part_refs = refs[:n_parts]
    w_ref, x_ref, g_ref, b_ref, o32_ref, o16_ref, mix_ref = refs[n_parts:]
    j = pl.program_id(1)

    @pl.when(j == 0)
    def _():
        off = 0
        for p_ref, wd in zip(part_refs, widths):
            mix_ref[:, off:off + wd] = p_ref[...]
            off += wd

    z = ALPHA * x_ref[...] + jnp.dot(mix_ref[...], w_ref[...], preferred_element_type=F32)
    for jj in range(nj):
        @pl.when(j == jj)
        def _(jj=jj):
            o32_ref[:, jj * bn:(jj + 1) * bn] = z

    @pl.when(j == nj - 1)
    def _():
        zz = o32_ref[...]
        mu = jnp.mean(zz, axis=-1, keepdims=True)
        zc = zz - mu
        var = jnp.mean(zc * zc, axis=-1, keepdims=True)
        y = zc * lax.rsqrt(var + LN_EPS) * g_ref[...] + b_ref[...]
        o32_ref[...] = y
        o16_ref[...] = y.astype(BF16)


def _proj_residual_ln(parts, w, x_res, g, b, *, bm, bn):
    t, d = x_res.shape
    widths = tuple(p.shape[1] for p in parts)
    kdim = sum(widths)
    nj = d // bn
    kern = functools.partial(_proj_ln_kernel, n_parts=len(parts), widths=widths, bn=bn, nj=nj)
    in_specs = [pl.BlockSpec((bm, wd), lambda i, j: (i, 0)) for wd in widths]
    in_specs += [pl.BlockSpec((kdim, bn), lambda i, j: (0, j)),
                 pl.BlockSpec((bm, bn), lambda i, j: (i, j)),
                 pl.BlockSpec((1, d), lambda i, j: (0, 0)),
                 pl.BlockSpec((1, d), lambda i, j: (0, 0))]
    return pl.pallas_call(
        kern,
        grid=(t // bm, nj),
        in_specs=in_specs,
        out_specs=[pl.BlockSpec((bm, d), lambda i, j: (i, 0)),
                   pl.BlockSpec((bm, d), lambda i, j: (i, 0))],
        out_shape=[jax.ShapeDtypeStruct((t, d), F32), jax.ShapeDtypeStruct((t, d), BF16)],
        scratch_shapes=[pltpu.VMEM((bm, kdim), BF16)],
        compiler_params=_cparams(("parallel", "arbitrary")),
        name="proj_ln",
    )(*parts, w, x_res, g.reshape(1, d), b.reshape(1, d))


def _nat_geometry(t):
    rows = t // GRID_W
    win_r = min(NAT_WIN_ROWS, rows)
    kr = NAT_KEY_ROWS
    assert rows >= kr and rows % NAT_Q_ROWS == 0
    n_blk = rows // NAT_Q_ROWS
    col = np.arange(GRID_W)
    c0 = np.clip(col - NAT_WIN_COLS // 2, 0, GRID_W - NAT_WIN_COLS)
    keys, ks_list, ty_list, tables = {}, [], [], []
    for i in range(n_blk):
        qr = i * NAT_Q_ROWS + np.arange(NAT_Q_ROWS)
        r0 = np.clip(qr - win_r // 2, 0, rows - win_r)
        ks = int(np.clip(r0[0], 0, rows - kr))
        key = (tuple(qr - ks), tuple(r0 - ks))
        if key not in keys:
            keys[key] = len(tables)
            q_row = np.repeat(qr, GRID_W)
            q_r0 = np.repeat(r0, GRID_W)
            q_col = np.tile(col, NAT_Q_ROWS)
            q_c0 = np.tile(c0, NAT_Q_ROWS)
            k_row = np.repeat(ks + np.arange(kr), GRID_W)
            k_col = np.tile(col, kr)
            row_ok = (k_row[None, :] >= q_r0[:, None]) & (k_row[None, :] < q_r0[:, None] + win_r)
            col_ok = (k_col[None, :] >= q_c0[:, None]) & (k_col[None, :] < q_c0[:, None] + NAT_WIN_COLS)
            dr = np.clip(k_row[None, :] - q_row[:, None] + NAT_WIN_ROWS - 1, 0, 2 * NAT_WIN_ROWS - 2)
            dc = np.clip(k_col[None, :] - q_col[:, None] + NAT_WIN_COLS - 1, 0, 2 * NAT_WIN_COLS - 2)
            tables.append((dr, dc, row_ok & col_ok))
        ks_list.append(ks)
        ty_list.append(keys[key])
    dr = np.stack([tb[0] for tb in tables]).astype(np.int32)
    dc = np.stack([tb[1] for tb in tables]).astype(np.int32)
    ok = np.stack([tb[2] for tb in tables])
    return np.asarray(ks_list, np.int32), np.asarray(ty_list, np.int32), dr, dc, ok


def _nat_kernel(ks_ref, ty_ref, q_ref, k_ref, v_ref, bias_ref, g_ref, o_ref, *, blocks_per_step, kw):
    nq = NAT_Q_ROWS * GRID_W
    base = pl.program_id(1) * blocks_per_step
    scale = HEAD_DIM ** -0.5
    for jj in range(blocks_per_step):
        i = base + jj
        k0 = pl.multiple_of(ks_ref[i] * GRID_W, GRID_W)
        q = q_ref[jj * nq:(jj + 1) * nq, :]
        k = k_ref[pl.ds(k0, kw), :]
        v = v_ref[pl.ds(k0, kw), :]
        s = lax.dot_general(q, k, (((1,), (1,)), ((), ())), preferred_element_type=F32)
        s = s * scale + bias_ref[ty_ref[i], 0]
        m = jnp.max(s, axis=-1, keepdims=True)
        p = jnp.exp(s - m)
        l = jnp.sum(p, axis=-1, keepdims=True)
        o = jnp.dot(p.astype(BF16), v, preferred_element_type=F32) / l
        ms = jnp.mean(o * o, axis=-1, keepdims=True)
        o_ref[jj * nq:(jj + 1) * nq, :] = (o * lax.rsqrt(ms + LN_EPS) * g_ref[0]).astype(o_ref.dtype)


def _nat_attention(u, rpb, norm_g, *, blocks_per_step):
    t = u.shape[0]
    ks, ty, dr, dc, ok = _nat_geometry(t)
    n_blk = ks.shape[0]
    kw = NAT_KEY_ROWS * GRID_W
    nq = NAT_Q_ROWS * GRID_W
    n_type = ok.shape[0]
    pad_c = GRID_W - NAT_WIN_COLS
    toep = _toeplitz(jnp.pad(rpb.astype(F32), ((0, 0), (0, 0), (pad_c, pad_c))), GRID_W, GRID_W)
    dr_small = dr[:, ::GRID_W, ::GRID_W].reshape(-1)
    sel = toep[:, dr_small].reshape(NAT_HEADS, n_type, NAT_Q_ROWS, NAT_KEY_ROWS, GRID_W, GRID_W)
    sel = jnp.transpose(sel, (1, 0, 2, 4, 3, 5)).reshape(n_type, NAT_HEADS, nq, kw)
    bias = jnp.where(ok[:, None], sel, NEG_BIG)
    g = norm_g.astype(F32).reshape(NAT_HEADS, 1, HEAD_DIM)
    rows_per_step = blocks_per_step * nq
    kern = functools.partial(_nat_kernel, blocks_per_step=blocks_per_step, kw=kw)
    grid_spec = pltpu.PrefetchScalarGridSpec(
        num_scalar_prefetch=2,
        grid=(NAT_HEADS, n_blk // blocks_per_step),
        in_specs=[pl.BlockSpec((rows_per_step, HEAD_DIM), lambda h, g_, *_: (g_, h)),
                  pl.BlockSpec((t, HEAD_DIM), lambda h, g_, *_: (0, NAT_HEADS + h)),
                  pl.BlockSpec((t, HEAD_DIM), lambda h, g_, *_: (0, 2 * NAT_HEADS + h)),
                  pl.BlockSpec((n_type, 1, nq, kw), lambda h, g_, *_: (0, h, 0, 0)),
                  pl.BlockSpec((1, 1, HEAD_DIM), lambda h, g_, *_: (h, 0, 0))],
        out_specs=pl.BlockSpec((rows_per_step, HEAD_DIM), lambda h, g_, *_: (g_, h)),
    )
    return pl.pallas_call(
        kern,
        grid_spec=grid_spec,
        out_shape=jax.ShapeDtypeStruct((t, NAT_WIDTH), BF16),
        compiler_params=_cparams(("parallel", "arbitrary")),
        name="nat_attn",
    )(jnp.asarray(ks), jnp.asarray(ty), u, u, u, bias, g)


def _t5_bucket(rel):
    half = REL_BUCKETS // 2
    max_exact = half // 2
    ret = jnp.where(rel > 0, half, 0)
    n = jnp.abs(rel)
    n_f = jnp.maximum(n, 1).astype(jnp.float32)
    large = max_exact + (jnp.log(n_f / max_exact) / math.log(REL_MAX_DIST / max_exact)
                         * (half - max_exact)).astype(jnp.int32)
    large = jnp.minimum(large, half - 1)
    return ret + jnp.where(n < max_exact, n, large)


def _diff_kernel(far_ref, q_ref, k_ref, v_ref, bias_ref, lp_ref, g_ref, o_ref,
                 qs_ref, m_ref, acc_ref, *, bk, ratio, nk, lam_init, rb):
    h = pl.program_id(0)
    qi = pl.program_id(1)
    bq = q_ref.shape[0]
    n_rb = 2 * bq // rb

    q = q_ref[...]
    lane = lax.broadcasted_iota(jnp.int32, q.shape, 1)
    qsc = q * jnp.asarray(DIFF_QK_DIM ** -0.5, q.dtype)
    zero = jnp.zeros_like(qsc)
    qs_ref[0:bq, :] = jnp.where(lane < DIFF_QK_DIM, qsc, zero)
    qs_ref[bq:2 * bq, :] = jnp.where(lane >= DIFF_QK_DIM, qsc, zero)
    m_ref[...] = jnp.full(m_ref.shape, -jnp.inf, F32)
    acc_ref[...] = jnp.zeros(acc_ref.shape, F32)

    def step(ki, near_idx, const):
        k0 = pl.multiple_of(ki * bk, bk)
        k = k_ref[pl.ds(k0, bk), :]
        v = v_ref[pl.ds(k0, bk), :]
        v_ext = jnp.concatenate([v, jnp.ones_like(v)], axis=1)
        for r in range(n_rb):
            rows = slice(r * rb, (r + 1) * rb)
            s = lax.dot_general(qs_ref[rows, :], k, (((1,), (1,)), ((), ())), preferred_element_type=F32)
            if near_idx is not None:
                r0 = (r * rb) % bq
                s = s + bias_ref[0, near_idx, r0:r0 + rb, :]
            mt = jnp.max(s, axis=-1, keepdims=True)
            if const is not None:
                mt = mt + const
            m_prev = m_ref[rows, :]
            m_new = jnp.maximum(m_prev, mt)
            a = jnp.exp(m_prev - m_new)
            shift = m_new if const is None else m_new - const
            p = jnp.exp(s - jnp.tile(shift, (1, bk // LANES)))
            acc_ref[rows, :] = (jnp.tile(a, (1, 2)) * acc_ref[rows, :]
                                + jnp.dot(p.astype(BF16), v_ext, preferred_element_type=F32))
            m_ref[rows, :] = m_new

    c_left = far_ref[h, 0]
    c_right = far_ref[h, 1]
    first_near = qi * ratio - 1
    n_near = ratio + 2

    def left_body(ki, carry):
        step(ki, None, c_left)
        return carry

    lax.fori_loop(0, jnp.clip(first_near, 0, nk), left_body, 0)
    for tt in range(n_near):
        ki = first_near + tt

        @pl.when((ki >= 0) & (ki < nk))
        def _(ki=ki, tt=tt):
            step(ki, tt, None)

    def right_body(ki, carry):
        step(ki, None, c_right)
        return carry

    lax.fori_loop(jnp.clip(first_near + n_near, 0, nk), nk, right_body, 0)

    acc = acc_ref[...]
    o = acc[:, 0:HEAD_DIM] / acc[:, HEAD_DIM:2 * HEAD_DIM]
    lp = lp_ref[...]
    lam = (jnp.exp(jnp.sum(lp[0:1] * lp[1:2], axis=-1, keepdims=True))
           - jnp.exp(jnp.sum(lp[2:3] * lp[3:4], axis=-1, keepdims=True)) + lam_init)
    d = o[0:bq] - lam * o[bq:2 * bq]
    ms = jnp.mean(d * d, axis=-1, keepdims=True)
    o_ref[...] = (d * lax.rsqrt(ms + LN_EPS) * g_ref[...] * (1.0 - lam_init)).astype(o_ref.dtype)


def _diff_bias_tiles(t5_table, bq, bk):
    ratio = bq // bk
    span = bq + bk + 1
    rel = jnp.arange(-span, span + 1, dtype=jnp.int32)
    vec = jnp.take(t5_table.astype(F32), _t5_bucket(rel), axis=0)
    vec_t = vec.T
    diags = jnp.stack([vec_t[:, (tt - 1) * bk + span - (bq - 1):(tt - 1) * bk + span + bk]
                       for tt in range(ratio + 2)], axis=1)
    tiles = _toeplitz(diags, bq, bk)
    far = jnp.stack([vec[0], vec[-1]], axis=1)
    return tiles, far


def _toeplitz(w, n_rows, n_cols):
    period = n_rows + n_cols
    wp = jnp.pad(w, [(0, 0)] * (w.ndim - 1) + [(0, 1)])
    flat = jnp.tile(wp, n_rows)[..., :n_rows * (period - 1)]
    return flat.reshape(w.shape[:-1] + (n_rows, period - 1))[..., n_rows - 1:n_rows - 1 + n_cols]


def _diff_attention(u, bias_tiles, far, lam_params, subln_g, lam_init, *, bq, bk, rb, col0):
    t = u.shape[0]
    assert bk >= REL_MAX_DIST and bq % bk == 0
    ratio = bq // bk
    nk = t // bk
    n_near = ratio + 2
    assert bq % rb == 0
    kern = functools.partial(_diff_kernel, bk=bk, ratio=ratio, nk=nk, lam_init=lam_init, rb=rb)
    grid_spec = pltpu.PrefetchScalarGridSpec(
        num_scalar_prefetch=1,
        grid=(DIFF_HEADS, t // bq),
        in_specs=[pl.BlockSpec((bq, HEAD_DIM), lambda h, i, *_: (i, col0 + h)),
                  pl.BlockSpec((t, HEAD_DIM), lambda h, i, *_: (0, col0 + DIFF_HEADS + h)),
                  pl.BlockSpec((t, HEAD_DIM), lambda h, i, *_: (0, col0 + 2 * DIFF_HEADS + h)),
                  pl.BlockSpec((1, n_near, bq, bk), lambda h, i, *_: (h, 0, 0, 0)),
                  pl.BlockSpec((4, DIFF_QK_DIM), lambda h, i, *_: (0, 0)),
                  pl.BlockSpec((1, HEAD_DIM), lambda h, i, *_: (0, 0))],
        out_specs=pl.BlockSpec((bq, HEAD_DIM), lambda h, i, *_: (i, h)),
        scratch_shapes=[pltpu.VMEM((2 * bq, HEAD_DIM), BF16),
                        pltpu.VMEM((2 * bq, LANES), F32),
                        pltpu.VMEM((2 * bq, 2 * HEAD_DIM), F32)],
    )
    return pl.pallas_call(
        kern,
        grid_spec=grid_spec,
        out_shape=jax.ShapeDtypeStruct((t, DIFF_WIDTH), BF16),
        compiler_params=_cparams(("parallel", "arbitrary")),
        name="diff_attn",
    )(far, u, u, u, bias_tiles, lam_params.astype(F32), subln_g.astype(F32).reshape(1, HEAD_DIM))


CONV_HALO = 16


def _conv_silu_kernel(prev_ref, cur_ref, next_ref, w_ref, o_ref, *, k_scale):
    i = pl.program_id(0)
    j = pl.program_id(1)
    bt = cur_ref.shape[0]
    x = cur_ref[...].astype(F32)
    p = jnp.where(i > 0, prev_ref[...].astype(F32), 0.0)
    nx = jnp.where(i < pl.num_programs(0) - 1, next_ref[...].astype(F32), 0.0)
    ext = jnp.concatenate([p, x, nx], axis=0)
    n_ext = bt + 2 * CONV_HALO
    half = MLSTM_CONV // 2
    acc = w_ref[half:half + 1, :] * x
    for tap in range(MLSTM_CONV):
        s = tap - half
        if s == 0:
            continue
        rolled = pltpu.roll(ext, (n_ext - s) % n_ext, 0)
        acc = acc + w_ref[tap:tap + 1, :] * rolled[CONV_HALO:CONV_HALO + bt]
    y = acc * jax.nn.sigmoid(acc)
    o_ref[...] = (y * jnp.where(j == 1, k_scale, 1.0)).astype(o_ref.dtype)


def _conv_silu(u, conv_w, *, bt, col0):
    t = u.shape[0]
    w = MLSTM_WIDTH
    nb = t // bt
    hb = bt // CONV_HALO
    last = t // CONV_HALO - 1
    return pl.pallas_call(
        functools.partial(_conv_silu_kernel, k_scale=HEAD_DIM ** -0.5),
        grid=(nb, 2),
        in_specs=[pl.BlockSpec((CONV_HALO, w), lambda i, j: (jnp.maximum(i * hb - 1, 0), col0 + j)),
                  pl.BlockSpec((bt, w), lambda i, j: (i, col0 + j)),
                  pl.BlockSpec((CONV_HALO, w), lambda i, j: (jnp.minimum((i + 1) * hb, last), col0 + j)),
                  pl.BlockSpec((MLSTM_CONV, w), lambda i, j: (0, j))],
        out_specs=pl.BlockSpec((bt, w), lambda i, j: (i, j)),
        out_shape=jax.ShapeDtypeStruct((t, 2 * w), BF16),
        compiler_params=_cparams(("parallel", "parallel")),
        name="mlstm_conv",
    )(u, u, u, conv_w.astype(F32))


def _split3(x):
    x1 = x.astype(BF16)
    r1 = x - x1.astype(F32)
    x2 = r1.astype(BF16)
    x3 = (r1 - x2.astype(F32)).astype(BF16)
    return x1, x2, x3


def _log_sigmoid(x):
    return jnp.minimum(x, 0.0) - jnp.log(1.0 + jnp.exp(-jnp.abs(x)))


def _mlstm_dir(q, k, v, bcol, igcol, brow, igrow, c_ref, n_ref, m_ref, idx, fwd, lc):
    t_io = lax.broadcasted_iota(jnp.int32, (lc, lc), 0)
    s_io = lax.broadcasted_iota(jnp.int32, (lc, lc), 1)
    mask = (s_io <= t_io) if fwd else (s_io >= t_io)
    dmat = jnp.where(mask, bcol - brow + igrow, -jnp.inf)
    m_d = jnp.max(dmat, axis=-1, keepdims=True)
    m_prev = m_ref[idx]
    inter = bcol + m_prev
    m_t = jnp.maximum(inter, m_d)
    w_inter = jnp.exp(inter - m_t)
    s = lax.dot_general(q, k, (((1,), (1,)), ((), ())), preferred_element_type=F32)
    p = jnp.exp(dmat - m_t) * s
    num = (w_inter * jnp.dot(q, c_ref[idx].astype(BF16), preferred_element_type=F32)
           + jnp.dot(p.astype(BF16), v, preferred_element_type=F32))
    qn = jnp.sum(q.astype(F32) * n_ref[idx], axis=-1, keepdims=True)
    den = w_inter * qn + jnp.sum(p, axis=-1, keepdims=True)
    h = num / jnp.maximum(jnp.abs(den), jnp.exp(-m_t))
    g = bcol[lc - 1:lc] if fwd else bcol[0:1]
    a = g - bcol + igcol
    m_loc = jnp.max(a, axis=0, keepdims=True)
    kw = k.astype(F32) * jnp.exp(a - m_loc)
    c_loc = lax.dot_general(kw.astype(BF16), v, (((0,), (0,)), ((), ())), preferred_element_type=F32)
    n_loc = jnp.sum(kw, axis=0, keepdims=True)
    m_new = jnp.maximum(g + m_prev, m_loc)
    s_old = jnp.exp(g + m_prev - m_new)
    s_loc = jnp.exp(m_loc - m_new)
    c_ref[idx] = s_old * c_ref[idx] + s_loc * c_loc
    n_ref[idx] = s_old * n_ref[idx] + s_loc * n_loc
    m_ref[idx] = m_new
    return h


def _mlstm_kernel(qf_ref, kf_ref, vf_ref, gf_ref, qb_ref, kb_ref, vb_ref, gb_ref, gbias_ref,
                  hf_ref, hb_ref, c_ref, n_ref, m_ref, *, lc):
    nh = MLSTM_HEADS

    @pl.when(pl.program_id(0) == 0)
    def _():
        c_ref[...] = jnp.zeros(c_ref.shape, F32)
        n_ref[...] = jnp.zeros(n_ref.shape, F32)
        m_ref[...] = jnp.zeros(m_ref.shape, F32)

    t_io = lax.broadcasted_iota(jnp.int32, (lc, lc), 0)
    s_io = lax.broadcasted_iota(jnp.int32, (lc, lc), 1)
    for fwd in (True, False):
        g_ref, q_ref, k_ref, v_ref, h_ref = ((gf_ref, qf_ref, kf_ref, vf_ref, hf_ref) if fwd
                                             else (gb_ref, qb_ref, kb_ref, vb_ref, hb_ref))
        gates = g_ref[...] + gbias_ref[...]
        tri = jnp.where((s_io <= t_io) if fwd else (s_io >= t_io), 1.0, 0.0).astype(BF16)
        cum = None
        for piece in _split3(_log_sigmoid(gates)):
            part = jnp.dot(tri, piece, preferred_element_type=F32)
            cum = part if cum is None else cum + part
        gates_t = jnp.transpose(gates)
        cum_t = jnp.transpose(cum)
        lane0 = 0 if fwd else 2 * nh
        for hh in range(nh):
            li, lb = lane0 + hh, lane0 + nh + hh
            sl = slice(hh * HEAD_DIM, (hh + 1) * HEAD_DIM)
            h = _mlstm_dir(q_ref[:, sl], k_ref[:, sl], v_ref[:, sl],
                           cum[:, lb:lb + 1], gates[:, li:li + 1], cum_t[lb:lb + 1, :], gates_t[li:li + 1, :],
                           c_ref, n_ref, m_ref, (0 if fwd else nh) + hh, fwd, lc)
            h_ref[:, sl] = h


def _mlstm_scan(qk, u, gates, gate_b, *, lc, v_col):
    t = qk.shape[0]
    w = MLSTM_WIDTH
    nc = t // lc
    gbias = jnp.pad(gate_b.astype(F32), (0, LANES - gate_b.shape[0])).reshape(1, LANES)
    fmap = lambda col: (lambda c: (c, col))
    bmap = lambda col: (lambda c: (nc - 1 - c, col))
    nhd = 2 * MLSTM_HEADS
    return pl.pallas_call(
        functools.partial(_mlstm_kernel, lc=lc),
        grid=(nc,),
        in_specs=[pl.BlockSpec((lc, w), fmap(0)), pl.BlockSpec((lc, w), fmap(1)),
                  pl.BlockSpec((lc, w), fmap(v_col)), pl.BlockSpec((lc, LANES), fmap(0)),
                  pl.BlockSpec((lc, w), bmap(0)), pl.BlockSpec((lc, w), bmap(1)),
                  pl.BlockSpec((lc, w), bmap(v_col)), pl.BlockSpec((lc, LANES), bmap(0)),
                  pl.BlockSpec((1, LANES), lambda c: (0, 0))],
        out_specs=[pl.BlockSpec((lc, w), fmap(0)), pl.BlockSpec((lc, w), bmap(0))],
        out_shape=[jax.ShapeDtypeStruct((t, w), F32), jax.ShapeDtypeStruct((t, w), F32)],
        scratch_shapes=[pltpu.VMEM((nhd, HEAD_DIM, HEAD_DIM), F32),
                        pltpu.VMEM((nhd, 1, HEAD_DIM), F32),
                        pltpu.VMEM((nhd, 1, 1), F32)],
        compiler_params=_cparams(("arbitrary",)),
        name="mlstm_scan",
    )(qk, qk, u, gates, qk, qk, u, gates, gbias)


def _mlstm_out_kernel(hf_ref, hb_ref, o_ref, g_ref, out_ref):
    for hh in range(MLSTM_HEADS):
        sl = slice(hh * HEAD_DIM, (hh + 1) * HEAD_DIM)
        x = (hf_ref[:, sl] + hb_ref[:, sl]) * jax.nn.sigmoid(o_ref[:, sl].astype(F32))
        xc = x - jnp.mean(x, axis=-1, keepdims=True)
        y = xc * lax.rsqrt(jnp.mean(xc * xc, axis=-1, keepdims=True) + LN_EPS) * g_ref[:, sl]
        out_ref[:, sl] = y.astype(out_ref.dtype)


def _mlstm_out(hf, hb, u, norm_g, *, bt, o_col):
    t, w = hf.shape
    return pl.pallas_call(
        _mlstm_out_kernel,
        grid=(t // bt,),
        in_specs=[pl.BlockSpec((bt, w), lambda i: (i, 0)), pl.BlockSpec((bt, w), lambda i: (i, 0)),
                  pl.BlockSpec((bt, w), lambda i: (i, o_col)), pl.BlockSpec((1, w), lambda i: (0, 0))],
        out_specs=pl.BlockSpec((bt, w), lambda i: (i, 0)),
        out_shape=jax.ShapeDtypeStruct((t, w), BF16),
        compiler_params=_cparams(("parallel",)),
        name="mlstm_out",
    )(hf, hb, u, norm_g.astype(F32).reshape(1, w))


def _mlstm_block(u, gates, conv_w, gate_b, norm_g, *, lc, col0):
    qk = _conv_silu(u, conv_w, bt=512, col0=col0)
    hf, hb = _mlstm_scan(qk, u, gates, gate_b, lc=lc, v_col=col0 + 2)
    return _mlstm_out(hf, hb, u, norm_g, bt=512, o_col=col0 + 3)


def _router_kernel(x_ref, w_ref, b_ref, e_ref, g_ref):
    logits = jnp.dot(x_ref[...], w_ref[...], preferred_element_type=F32) + b_ref[...]
    lane = lax.broadcasted_iota(jnp.int32, logits.shape, 1)
    work = logits
    e_out = jnp.zeros(logits.shape, jnp.int32)
    vals = []
    for kk in range(TOP_K):
        m = jnp.max(work, axis=-1, keepdims=True)
        idx = jnp.min(jnp.where(work == m, lane, LANES), axis=-1, keepdims=True)
        e_out = jnp.where(lane == kk, idx, e_out)
        vals.append(m)
        work = jnp.where(lane == idx, -jnp.inf, work)
    ex = [jnp.exp(v - vals[0]) for v in vals]
    den = ex[0]
    for e in ex[1:]:
        den = den + e
    g_out = jnp.zeros(logits.shape, F32)
    for kk in range(TOP_K):
        g_out = jnp.where(lane == kk, ex[kk] / den, g_out)
    e_ref[...] = e_out
    g_ref[...] = g_out


def _router(x16, w_router, b_router, *, bm):
    t, d = x16.shape
    n_e = w_router.shape[1]
    w = jnp.pad(w_router, ((0, 0), (0, LANES - n_e))).astype(BF16)
    b = jnp.pad(b_router.astype(F32), (0, LANES - n_e), constant_values=NEG_BIG).reshape(1, LANES)
    return pl.pallas_call(
        _router_kernel,
        grid=(t // bm,),
        in_specs=[pl.BlockSpec((bm, d), lambda i: (i, 0)),
                  pl.BlockSpec((d, LANES), lambda i: (0, 0)),
                  pl.BlockSpec((1, LANES), lambda i: (0, 0))],
        out_specs=[pl.BlockSpec((bm, LANES), lambda i: (i, 0)),
                   pl.BlockSpec((bm, LANES), lambda i: (i, 0))],
        out_shape=[jax.ShapeDtypeStruct((t, LANES), jnp.int32), jax.ShapeDtypeStruct((t, LANES), F32)],
        compiler_params=_cparams(("parallel",)),
        name="router",
    )(x16, w, b)


def _rank_kernel(e_ref, rank_ref, cnt_ref, base_ref):
    @pl.when(pl.program_id(0) == 0)
    def _():
        base_ref[...] = jnp.zeros(base_ref.shape, F32)

    e = e_ref[...]
    bt = e.shape[0]
    lane = lax.broadcasted_iota(jnp.int32, e.shape, 1)
    onehots = [jnp.where(lane == e[:, kk:kk + 1], 1.0, 0.0) for kk in range(TOP_K)]
    tot = onehots[0]
    for oh in onehots[1:]:
        tot = tot + oh
    t_io = lax.broadcasted_iota(jnp.int32, (bt, bt), 0)
    s_io = lax.broadcasted_iota(jnp.int32, (bt, bt), 1)
    earlier = jnp.where(s_io < t_io, 1.0, 0.0).astype(BF16)
    pos = jnp.dot(earlier, tot.astype(BF16), preferred_element_type=F32) + base_ref[...]
    out = jnp.zeros(e.shape, jnp.int32)
    for kk in range(TOP_K):
        rk = jnp.sum(onehots[kk] * pos, axis=-1, keepdims=True)
        out = jnp.where(lane == kk, rk.astype(jnp.int32), out)
    rank_ref[...] = out
    base_ref[...] = base_ref[...] + jnp.sum(tot, axis=0, keepdims=True)
    cnt_ref[...] = base_ref[...]


def _slot_ranks(top_e, *, bt):
    t = top_e.shape[0]
    return pl.pallas_call(
        _rank_kernel,
        grid=(t // bt,),
        in_specs=[pl.BlockSpec((bt, LANES), lambda i: (i, 0))],
        out_specs=[pl.BlockSpec((bt, LANES), lambda i: (i, 0)), pl.BlockSpec((1, LANES), lambda i: (0, 0))],
        out_shape=[jax.ShapeDtypeStruct((t, LANES), jnp.int32), jax.ShapeDtypeStruct((1, LANES), F32)],
        scratch_shapes=[pltpu.VMEM((1, LANES), F32)],
        compiler_params=_cparams(("arbitrary",)),
        name="moe_rank",
    )(top_e)


def _dispatch_plan(top_e_lanes, n_experts, blk, rank_bt):
    t = top_e_lanes.shape[0]
    n_slots = t * TOP_K
    n_blocks = -(-n_slots // blk) + n_experts
    rank_lanes, counts_f = _slot_ranks(top_e_lanes, bt=rank_bt)
    slot_e = top_e_lanes[:, :TOP_K].reshape(-1)
    rank = rank_lanes[:, :TOP_K].reshape(-1)
    counts = counts_f[0, :n_experts].astype(jnp.int32)
    onehot = (slot_e[:, None] == jnp.arange(n_experts, dtype=jnp.int32)[None, :]).astype(jnp.int32)
    padded = (counts + blk - 1) // blk * blk
    pad_end = jnp.cumsum(padded)
    pad_start = pad_end - padded
    dest = (jnp.sum(pad_start[None, :] * onehot, axis=1) + rank).astype(jnp.int32)
    slot_tok = jnp.arange(n_slots, dtype=jnp.int32) // TOP_K
    row_tok = jnp.zeros((n_blocks * blk,), jnp.int32).at[dest].set(slot_tok)
    n_act = (pad_end[-1] // blk).astype(jnp.int32)
    blk_id = jnp.minimum(jnp.arange(n_blocks, dtype=jnp.int32), n_act - 1)
    blk_expert = jnp.minimum(jnp.searchsorted(pad_end, blk_id * blk, side='right'), n_experts - 1).astype(jnp.int32)
    return row_tok, dest, blk_expert, n_act.reshape(1)


def _expert_kernel(be_ref, tok_ref, nact_ref, x_hbm, wg_ref, bg_ref, wl_ref, bl_ref, wd_ref, bd_ref, o_ref,
                   xg_ref, sem_ref, *, blk):
    b = pl.program_id(0)
    n_act = nact_ref[0]
    slot = lax.rem(b, 2)

    def row_copy(blk_idx, r, dst_slot):
        tok = tok_ref[blk_idx * blk + r]
        return pltpu.make_async_copy(x_hbm.at[pl.ds(tok, 1), :], xg_ref.at[dst_slot, pl.ds(r, 1), :],
                                     sem_ref.at[dst_slot])

    def compute(prefetch_next):
        pltpu.make_async_copy(xg_ref.at[slot], xg_ref.at[slot], sem_ref.at[slot]).wait()
        if prefetch_next:
            for r in range(blk):
                row_copy(b + 1, r, 1 - slot).start()
        xb = xg_ref[slot].astype(BF16)
        hg = jnp.minimum(jnp.dot(xb, wg_ref[0], preferred_element_type=F32) + bg_ref[0], SWIGLU_LIMIT)
        hl = jnp.clip(jnp.dot(xb, wl_ref[0], preferred_element_type=F32) + bl_ref[0], -SWIGLU_LIMIT, SWIGLU_LIMIT)
        hh = hg * jax.nn.sigmoid(SWIGLU_ALPHA * hg) * (hl + 1.0)
        o_ref[...] = jnp.dot(hh.astype(BF16), wd_ref[0], preferred_element_type=F32) + bd_ref[0]

    @pl.when(b == 0)
    def _():
        def body(r, c):
            row_copy(0, r, 0).start()
            return c
        lax.fori_loop(0, blk, body, 0, unroll=8)

    @pl.when(b + 1 < n_act)
    def _():
        compute(True)

    @pl.when(b + 1 == n_act)
    def _():
        compute(False)

    @pl.when(b >= n_act)
    def _():
        o_ref[...] = jnp.zeros(o_ref.shape, o_ref.dtype)


def _expert_ffn(x32, row_tok, blk_expert, n_act, w_glu, b_glu, w_lin, b_lin, w_down, b_down, *, blk):
    t, d = x32.shape
    n_e, _, ff = w_glu.shape
    n_blocks = blk_expert.shape[0]
    single = pl.Buffered(1)
    wmap = lambda i, be, tok, na: (be[i], 0, 0)
    grid_spec = pltpu.PrefetchScalarGridSpec(
        num_scalar_prefetch=3,
        grid=(n_blocks,),
        in_specs=[pl.BlockSpec(memory_space=pl.ANY),
                  pl.BlockSpec((1, d, ff), wmap, pipeline_mode=single),
                  pl.BlockSpec((1, 1, ff), wmap),
                  pl.BlockSpec((1, d, ff), wmap, pipeline_mode=single),
                  pl.BlockSpec((1, 1, ff), wmap),
                  pl.BlockSpec((1, ff, d), wmap, pipeline_mode=single),
                  pl.BlockSpec((1, 1, d), wmap)],
        out_specs=pl.BlockSpec((blk, d), lambda i, be, tok, na: (i, 0)),
        scratch_shapes=[pltpu.VMEM((2, blk, d), F32), pltpu.SemaphoreType.DMA((2,))],
    )
    return pl.pallas_call(
        functools.partial(_expert_kernel, blk=blk),
        grid_spec=grid_spec,
        out_shape=jax.ShapeDtypeStruct((n_blocks * blk, d), F32),
        compiler_params=_cparams(("arbitrary",)),
        name="expert_ffn",
    )(blk_expert, row_tok, n_act, x32, w_glu.astype(BF16), b_glu.astype(F32).reshape(n_e, 1, ff),
      w_lin.astype(BF16), b_lin.astype(F32).reshape(n_e, 1, ff),
      w_down.astype(BF16), b_down.astype(F32).reshape(n_e, 1, d))


def _combine_kernel(dest_ref, ys_hbm, x_ref, gate_ref, g_ref, b_ref, o32_ref, o16_ref, buf_ref, sem_ref, *, bt):
    i = pl.program_id(0)
    n = pl.num_programs(0)

    def row_copy(blk_idx, r, kk, dst_slot):
        row = dest_ref[(blk_idx * bt + r) * TOP_K + kk]
        return pltpu.make_async_copy(ys_hbm.at[pl.ds(row, 1), :], buf_ref.at[dst_slot, kk, pl.ds(r, 1), :],
                                     sem_ref.at[dst_slot])

    @pl.when(i == 0)
    def _():
        def body(r, c):
            for kk in range(TOP_K):
                row_copy(0, r, kk, 0).start()
            return c
        lax.fori_loop(0, bt, body, 0, unroll=2)

    slot = lax.rem(i, 2)
    pltpu.make_async_copy(buf_ref.at[slot], buf_ref.at[slot], sem_ref.at[slot]).wait()
    nxt = jnp.where(i + 1 < n, i + 1, 0)
    for r in range(bt):
        for kk in range(TOP_K):
            row_copy(nxt, r, kk, 1 - slot).start()
    gate = gate_ref[...]
    z = ALPHA * x_ref[...]
    for kk in range(TOP_K):
        z = z + gate[:, kk:kk + 1] * buf_ref[slot, kk]
    mu = jnp.mean(z, axis=-1, keepdims=True)
    zc = z - mu
    var = jnp.mean(zc * zc, axis=-1, keepdims=True)
    y = zc * lax.rsqrt(var + LN_EPS) * g_ref[...] + b_ref[...]
    o32_ref[...] = y
    o16_ref[...] = y.astype(BF16)

    @pl.when(i == n - 1)
    def _():
        pltpu.make_async_copy(buf_ref.at[1 - slot], buf_ref.at[1 - slot], sem_ref.at[1 - slot]).wait()


def _combine_ln(ys, dest, x32, gate, g, b, *, bt):
    t, d = x32.shape
    grid_spec = pltpu.PrefetchScalarGridSpec(
        num_scalar_prefetch=1,
        grid=(t // bt,),
        in_specs=[pl.BlockSpec(memory_space=pl.ANY),
                  pl.BlockSpec((bt, d), lambda i, dr: (i, 0)),
                  pl.BlockSpec((bt, LANES), lambda i, dr: (i, 0)),
                  pl.BlockSpec((1, d), lambda i, dr: (0, 0)),
                  pl.BlockSpec((1, d), lambda i, dr: (0, 0))],
        out_specs=[pl.BlockSpec((bt, d), lambda i, dr: (i, 0)),
                   pl.BlockSpec((bt, d), lambda i, dr: (i, 0))],
        scratch_shapes=[pltpu.VMEM((2, TOP_K, bt, d), F32), pltpu.SemaphoreType.DMA((2,))],
    )
    return pl.pallas_call(
        functools.partial(_combine_kernel, bt=bt),
        grid_spec=grid_spec,
        out_shape=[jax.ShapeDtypeStruct((t, d), F32), jax.ShapeDtypeStruct((t, d), BF16)],
        compiler_params=_cparams(("arbitrary",)),
        name="moe_combine_ln",
    )(dest, ys, x32, gate, g.astype(F32).reshape(1, d), b.astype(F32).reshape(1, d))


def _moe_block(x32, x16, w_router, b_router, w_glu, b_glu, w_lin, b_lin, w_down, b_down, ln_g, ln_b,
               *, blk, bt, router_bm):
    top_e, gate = _router(x16, w_router, b_router, bm=router_bm)
    row_tok, dest, blk_expert, n_act = _dispatch_plan(top_e, w_glu.shape[0], blk, router_bm)
    ys = _expert_ffn(x32, row_tok, blk_expert, n_act, w_glu, b_glu, w_lin, b_lin, w_down, b_down, blk=blk)
    return _combine_ln(ys, dest, x32, gate, ln_g, ln_b, bt=bt)


def kernel(x, w_in, nat_rpb, nat_norm_g, diff_lambda, diff_subln_g, t5_table, ml_conv_w, ml_gate_b, ml_norm_g,
           w_out, ln_mix_g, ln_mix_b, router_w, router_b, exp_w_glu, exp_b_glu, exp_w_lin, exp_b_lin,
           exp_w_down, exp_b_down, ln_ffn_g, ln_ffn_b):
    bsz, t, d = x.shape
    assert bsz == 1
    x32 = x.reshape(t, d)
    x16 = x32.astype(BF16)
    diff_bq, diff_bk = 1024, 1024
    bias_tiles, far = _diff_bias_tiles(t5_table, diff_bq, diff_bk)
    ml_col0 = (3 * NAT_WIDTH + 3 * DIFF_WIDTH) // MLSTM_WIDTH
    for l in range(DEPTH):
        w_main = w_in[l][:, :MAIN_WIDTH].astype(BF16)
        w_gate = jnp.pad(w_in[l][:, MAIN_WIDTH:], ((0, 0), (0, LANES - 4 * MLSTM_HEADS))).astype(BF16)
        u = _matmul(x16, w_main, bm=1024, bn=1024, out_dtype=BF16, name="inproj")
        gates = _matmul(x16, w_gate, bm=1024, bn=LANES, out_dtype=F32, name="inproj_gates")
        nat = _nat_attention(u, nat_rpb[l], nat_norm_g[l], blocks_per_step=8)
        lam_init = 0.8 - 0.6 * math.exp(-0.3 * l)
        dif = _diff_attention(u, bias_tiles, far, diff_lambda[l], diff_subln_g[l], lam_init,
                              bq=diff_bq, bk=diff_bk, rb=128, col0=3 * NAT_WIDTH // HEAD_DIM)
        ml = _mlstm_block(u, gates, ml_conv_w[l], ml_gate_b[l], ml_norm_g[l], lc=256, col0=ml_col0)
        x1_32, x1_16 = _proj_residual_ln([nat, dif, ml], w_out[l].astype(BF16), x32, ln_mix_g[l], ln_mix_b[l],
                                         bm=256, bn=1024)
        x32, x16 = _moe_block(x1_32, x1_16, router_w[l], router_b[l], exp_w_glu[l], exp_b_glu[l], exp_w_lin[l],
                              exp_b_lin[l], exp_w_down[l], exp_b_down[l], ln_ffn_g[l], ln_ffn_b[l],
                              blk=256, bt=128, router_bm=512)
    return x32.reshape(bsz, t, d)
```

```python
import functools
import math

import numpy as np
import jax
import jax.numpy as jnp
from jax import lax
from jax.experimental import pallas as pl
from jax.experimental.pallas import tpu as pltpu

DEPTH = 2
HEAD_DIM = 128
NAT_HEADS = 16
DIFF_HEADS = 8
MLSTM_HEADS = 8
NAT_WIDTH = NAT_HEADS * HEAD_DIM
DIFF_WIDTH = DIFF_HEADS * HEAD_DIM
MLSTM_WIDTH = MLSTM_HEADS * HEAD_DIM
DIFF_QK_DIM = HEAD_DIM // 2
MAIN_WIDTH = 3 * NAT_WIDTH + 3 * DIFF_WIDTH + 4 * MLSTM_WIDTH
GRID_W = 64
NAT_WIN_ROWS = 8
NAT_WIN_COLS = 16
NAT_Q_ROWS = 2
NAT_KEY_ROWS = 10
REL_BUCKETS = 32
REL_MAX_DIST = 128
MLSTM_CHUNK = 64
MLSTM_CONV = 5
N_EXPERTS = 32
TOP_K = 4
SWIGLU_ALPHA = 1.702
SWIGLU_LIMIT = 7.0
MOE_BLOCK = 512
LN_EPS = 1e-5
ALPHA = (2.0 * DEPTH) ** 0.25
NEG_BIG = -1e30

LANES = 128
VMEM_LIMIT = 56 * 1024 * 1024

BF16 = jnp.bfloat16
F32 = jnp.float32


def _cparams(sem):
    return pltpu.CompilerParams(dimension_semantics=sem, vmem_limit_bytes=VMEM_LIMIT)


def _mm_kernel(x_ref, w_ref, o_ref):
    o_ref[...] = jnp.dot(x_ref[...], w_ref[...], preferred_element_type=F32).astype(o_ref.dtype)


def _matmul(x, w, *, bm, bn, out_dtype, name):
    m, k = x.shape
    n = w.shape[1]
    return pl.pallas_call(
        _mm_kernel,
        grid=(m // bm, n // bn),
        in_specs=[pl.BlockSpec((bm, k), lambda i, j: (i, 0)),
                  pl.BlockSpec((k, bn), lambda i, j: (0, j))],
        out_specs=pl.BlockSpec((bm, bn), lambda i, j: (i, j)),
        out_shape=jax.ShapeDtypeStruct((m, n), out_dtype),
        compiler_params=_cparams(("parallel", "parallel")),
        name=name,
    )(x, w)


def _proj_ln_kernel(*refs, n_parts, widths, bn, nj):
    part_refs = refs[:n_parts]
    w_ref, x_ref, g_ref, b_ref, o32_ref, o16_ref, mix_ref = refs[n_parts:]
    j = pl.program_id(1)

    @pl.when(j == 0)
    def _():
        off = 0
        for p_ref, wd in zip(part_refs, widths):
            mix_ref[:, off:off + wd] = p_ref[...]
            off += wd

    z = ALPHA * x_ref[...] + jnp.dot(mix_ref[...], w_ref[...], preferred_element_type=F32)
    for jj in range(nj):
        @pl.when(j == jj)
        def _(jj=jj):
            o32_ref[:, jj * bn:(jj + 1) * bn] = z

    @pl.when(j == nj - 1)
    def _():
        zz = o32_ref[...]
        mu = jnp.mean(zz, axis=-1, keepdims=True)
        zc = zz - mu
        var = jnp.mean(zc * zc, axis=-1, keepdims=True)
        y = zc * lax.rsqrt(var + LN_EPS) * g_ref[...] + b_ref[...]
        o32_ref[...] = y
        o16_ref[...] = y.astype(BF16)


def _proj_residual_ln(parts, w, x_res, g, b, *, bm, bn):
    t, d = x_res.shape
    widths = tuple(p.shape[1] for p in parts)
    kdim = sum(widths)
    nj = d // bn
    kern = functools.partial(_proj_ln_kernel, n_parts=len(parts), widths=widths, bn=bn, nj=nj)
    in_specs = [pl.BlockSpec((bm, wd), lambda i, j: (i, 0)) for wd in widths]
    in_specs += [pl.BlockSpec((kdim, bn), lambda i, j: (0, j)),
                 pl.BlockSpec((bm, bn), lambda i, j: (i, j)),
                 pl.BlockSpec((1, d), lambda i, j: (0, 0)),
                 pl.BlockSpec((1, d), lambda i, j: (0, 0))]
    return pl.pallas_call(
        kern,
        grid=(t // bm, nj),
        in_specs=in_specs,
        out_specs=[pl.BlockSpec((bm, d), lambda i, j: (i, 0)),
                   pl.BlockSpec((bm, d), lambda i, j: (i, 0))],
        out_shape=[jax.ShapeDtypeStruct((t, d), F32), jax.ShapeDtypeStruct((t, d), BF16)],
        scratch_shapes=[pltpu.VMEM((bm, kdim), BF16)],
        compiler_params=_cparams(("parallel", "arbitrary")),
        name="proj_ln",
    )(*parts, w, x_res, g.reshape(1, d), b.reshape(1, d))


def _nat_geometry(t):
    rows = t // GRID_W
    win_r = min(NAT_WIN_ROWS, rows)
    kr = NAT_KEY_ROWS
    assert rows >= kr and rows % NAT_Q_ROWS == 0
    n_blk = rows // NAT_Q_ROWS
    col = np.arange(GRID_W)
    c0 = np.clip(col - NAT_WIN_COLS // 2, 0, GRID_W - NAT_WIN_COLS)
    keys, ks_list, ty_list, tables = {}, [], [], []
    for i in range(n_blk):
        qr = i * NAT_Q_ROWS + np.arange(NAT_Q_ROWS)
        r0 = np.clip(qr - win_r // 2, 0, rows - win_r)
        ks = int(np.clip(r0[0], 0, rows - kr))
        key = (tuple(qr - ks), tuple(r0 - ks))
        if key not in keys:
            keys[key] = len(tables)
            q_row = np.repeat(qr, GRID_W)
            q_r0 = np.repeat(r0, GRID_W)
            q_col = np.tile(col, NAT_Q_ROWS)
            q_c0 = np.tile(c0, NAT_Q_ROWS)
            k_row = np.repeat(ks + np.arange(kr), GRID_W)
            k_col = np.tile(col, kr)
            row_ok = (k_row[None, :] >= q_r0[:, None]) & (k_row[None, :] < q_r0[:, None] + win_r)
            col_ok = (k_col[None, :] >= q_c0[:, None]) & (k_col[None, :] < q_c0[:, None] + NAT_WIN_COLS)
            dr = np.clip(k_row[None, :] - q_row[:, None] + NAT_WIN_ROWS - 1, 0, 2 * NAT_WIN_ROWS - 2)
            dc = np.clip(k_col[None, :] - q_col[:, None] + NAT_WIN_COLS - 1, 0, 2 * NAT_WIN_COLS - 2)
            tables.append((dr, dc, row_ok & col_ok))
        ks_list.append(ks)
        ty_list.append(keys[key])
    dr = np.stack([tb[0] for tb in tables]).astype(np.int32)
    dc = np.stack([tb[1] for tb in tables]).astype(np.int32)
    ok = np.stack([tb[2] for tb in tables])
    return np.asarray(ks_list, np.int32), np.asarray(ty_list, np.int32), dr, dc, ok


def _nat_bias_kernel(rp_ref, ok_ref, o_ref, *, dr_small):
    lane = lax.broadcasted_iota(jnp.int32, (GRID_W, LANES), 1)
    for ty, dr_ty in enumerate(dr_small):
        for qrl, dr_row in enumerate(dr_ty):
            for pair in range(len(dr_row) // 2):
                da, db = int(dr_row[2 * pair]), int(dr_row[2 * pair + 1])
                ya = _toeplitz_rows(rp_ref[0, da:da + 1, :], GRID_W, GRID_W - 1)
                yb = _toeplitz_rows(rp_ref[0, db:db + 1, :], GRID_W, LANES - 1)
                rows = slice(qrl * GRID_W, (qrl + 1) * GRID_W)
                cols = slice(pair * LANES, (pair + 1) * LANES)
                o_ref[ty, 0, rows, cols] = jnp.where(ok_ref[ty, rows, cols] > 0.5,
                                                     jnp.where(lane < GRID_W, ya, yb), NEG_BIG)


def _nat_bias_table(rpb, dr_small, ok):
    n_type, nq, kw = ok.shape
    assert 2 * GRID_W == LANES and NAT_KEY_ROWS % 2 == 0
    pad_c = GRID_W - NAT_WIN_COLS
    n_dr = rpb.shape[1]
    rp = jnp.pad(rpb.astype(F32), ((0, 0), (0, -n_dr % 8), (pad_c, LANES - rpb.shape[2] - pad_c)))
    return pl.pallas_call(
        functools.partial(_nat_bias_kernel, dr_small=dr_small.tolist()),
        grid=(NAT_HEADS,),
        in_specs=[pl.BlockSpec((1, rp.shape[1], LANES), lambda h: (h, 0, 0)),
                  pl.BlockSpec((n_type, nq, kw), lambda h: (0, 0, 0))],
        out_specs=pl.BlockSpec((n_type, 1, nq, kw), lambda h: (0, h, 0, 0)),
        out_shape=jax.ShapeDtypeStruct((n_type, NAT_HEADS, nq, kw), F32),
        compiler_params=_cparams(("parallel",)),
        name="nat_bias_table",
    )(rp, jnp.asarray(ok.astype(np.float32)))


def _nat_kernel(ks_ref, ty_ref, q_ref, k_ref, v_ref, bias_ref, g_ref, o_ref, *, blocks_per_step, kw):
    nq = NAT_Q_ROWS * GRID_W
    base = pl.program_id(1) * blocks_per_step
    scale = HEAD_DIM ** -0.5
    for jj in range(blocks_per_step):
        i = base + jj
        k0 = pl.multiple_of(ks_ref[i] * GRID_W, GRID_W)
        q = q_ref[jj * nq:(jj + 1) * nq, :]
        k = k_ref[pl.ds(k0, kw), :]
        v = v_ref[pl.ds(k0, kw), :]
        s = lax.dot_general(q, k, (((1,), (1,)), ((), ())), preferred_element_type=F32)
        s = s * scale + bias_ref[ty_ref[i], 0]
        m = jnp.max(s, axis=-1, keepdims=True)
        p = jnp.exp(s - m)
        l = jnp.sum(p, axis=-1, keepdims=True)
        o = jnp.dot(p.astype(BF16), v, preferred_element_type=F32) / l
        ms = jnp.mean(o * o, axis=-1, keepdims=True)
        o_ref[jj * nq:(jj + 1) * nq, :] = (o * lax.rsqrt(ms + LN_EPS) * g_ref[0]).astype(o_ref.dtype)


def _nat_attention(u, rpb, norm_g, *, blocks_per_step):
    t = u.shape[0]
    ks, ty, dr, dc, ok = _nat_geometry(t)
    n_blk = ks.shape[0]
    kw = NAT_KEY_ROWS * GRID_W
    nq = NAT_Q_ROWS * GRID_W
    n_type = ok.shape[0]
    bias = _nat_bias_table(rpb, dr[:, ::GRID_W, ::GRID_W], ok)
    g = norm_g.astype(F32).reshape(NAT_HEADS, 1, HEAD_DIM)
    rows_per_step = blocks_per_step * nq
    kern = functools.partial(_nat_kernel, blocks_per_step=blocks_per_step, kw=kw)
    grid_spec = pltpu.PrefetchScalarGridSpec(
        num_scalar_prefetch=2,
        grid=(NAT_HEADS, n_blk // blocks_per_step),
        in_specs=[pl.BlockSpec((rows_per_step, HEAD_DIM), lambda h, g_, *_: (g_, h)),
                  pl.BlockSpec((t, HEAD_DIM), lambda h, g_, *_: (0, NAT_HEADS + h)),
                  pl.BlockSpec((t, HEAD_DIM), lambda h, g_, *_: (0, 2 * NAT_HEADS + h)),
                  pl.BlockSpec((n_type, 1, nq, kw), lambda h, g_, *_: (0, h, 0, 0)),
                  pl.BlockSpec((1, 1, HEAD_DIM), lambda h, g_, *_: (h, 0, 0))],
        out_specs=pl.BlockSpec((rows_per_step, HEAD_DIM), lambda h, g_, *_: (g_, h)),
    )
    return pl.pallas_call(
        kern,
        grid_spec=grid_spec,
        out_shape=jax.ShapeDtypeStruct((t, NAT_WIDTH), BF16),
        compiler_params=_cparams(("parallel", "arbitrary")),
        name="nat_attn",
    )(jnp.asarray(ks), jnp.asarray(ty), u, u, u, bias, g)


def _t5_bucket(rel):
    half = REL_BUCKETS // 2
    max_exact = half // 2
    ret = jnp.where(rel > 0, half, 0)
    n = jnp.abs(rel)
    n_f = jnp.maximum(n, 1).astype(jnp.float32)
    large = max_exact + (jnp.log(n_f / max_exact) / math.log(REL_MAX_DIST / max_exact)
                         * (half - max_exact)).astype(jnp.int32)
    large = jnp.minimum(large, half - 1)
    return ret + jnp.where(n < max_exact, n, large)


def _diff_kernel(far_ref, q_ref, k_ref, v_ref, bias_ref, lp_ref, g_ref, o_ref,
                 qs_ref, m_ref, acc_ref, *, bk, ratio, nk, lam_init, rb):
    h = pl.program_id(0)
    qi = pl.program_id(1)
    bq = q_ref.shape[0]
    n_rb = 2 * bq // rb

    q = q_ref[...]
    lane = lax.broadcasted_iota(jnp.int32, q.shape, 1)
    qsc = q * jnp.asarray(DIFF_QK_DIM ** -0.5, q.dtype)
    zero = jnp.zeros_like(qsc)
    qs_ref[0:bq, :] = jnp.where(lane < DIFF_QK_DIM, qsc, zero)
    qs_ref[bq:2 * bq, :] = jnp.where(lane >= DIFF_QK_DIM, qsc, zero)
    m_ref[...] = jnp.full(m_ref.shape, -jnp.inf, F32)
    acc_ref[...] = jnp.zeros(acc_ref.shape, F32)

    def step(ki, near_idx, const):
        k0 = pl.multiple_of(ki * bk, bk)
        k = k_ref[pl.ds(k0, bk), :]
        v = v_ref[pl.ds(k0, bk), :]
        v_ext = jnp.concatenate([v, jnp.ones_like(v)], axis=1)
        for r in range(n_rb):
            rows = slice(r * rb, (r + 1) * rb)
            s = lax.dot_general(qs_ref[rows, :], k, (((1,), (1,)), ((), ())), preferred_element_type=F32)
            if near_idx is not None:
                r0 = (r * rb) % bq
                s = s + bias_ref[0, near_idx, r0:r0 + rb, :]
            mt = jnp.max(s, axis=-1, keepdims=True)
            if const is not None:
                mt = mt + const
            m_prev = m_ref[rows, :]
            m_new = jnp.maximum(m_prev, mt)
            a = jnp.exp(m_prev - m_new)
            shift = m_new if const is None else m_new - const
            p = jnp.exp(s - jnp.tile(shift, (1, bk // LANES)))
            acc_ref[rows, :] = (jnp.tile(a, (1, 2)) * acc_ref[rows, :]
                                + jnp.dot(p.astype(BF16), v_ext, preferred_element_type=F32))
            m_ref[rows, :] = m_new

    c_left = far_ref[h, 0]
    c_right = far_ref[h, 1]
    first_near = qi * ratio - 1
    n_near = ratio + 2

    def left_body(ki, carry):
        step(ki, None, c_left)
        return carry

    lax.fori_loop(0, jnp.clip(first_near, 0, nk), left_body, 0)
    for tt in range(n_near):
        ki = first_near + tt

        @pl.when((ki >= 0) & (ki < nk))
        def _(ki=ki, tt=tt):
            step(ki, tt, None)

    def right_body(ki, carry):
        step(ki, None, c_right)
        return carry

    lax.fori_loop(jnp.clip(first_near + n_near, 0, nk), nk, right_body, 0)

    acc = acc_ref[...]
    o = acc[:, 0:HEAD_DIM] / acc[:, HEAD_DIM:2 * HEAD_DIM]
    lp = lp_ref[...]
    lam = (jnp.exp(jnp.sum(lp[0:1] * lp[1:2], axis=-1, keepdims=True))
           - jnp.exp(jnp.sum(lp[2:3] * lp[3:4], axis=-1, keepdims=True)) + lam_init)
    d = o[0:bq] - lam * o[bq:2 * bq]
    ms = jnp.mean(d * d, axis=-1, keepdims=True)
    o_ref[...] = (d * lax.rsqrt(ms + LN_EPS) * g_ref[...] * (1.0 - lam_init)).astype(o_ref.dtype)


def _diff_bias_tiles(t5_table, bq, bk):
    ratio = bq // bk
    span = bq + bk + 1
    rel = jnp.arange(-span, span + 1, dtype=jnp.int32)
    vec = jnp.take(t5_table.astype(F32), _t5_bucket(rel), axis=0)
    vec_t = vec.T
    diags = jnp.stack([vec_t[:, (tt - 1) * bk + span - (bq - 1):(tt - 1) * bk + span + bk]
                       for tt in range(ratio + 2)], axis=1)
    tiles = _toeplitz(diags, bq, bk)
    far = jnp.stack([vec[0], vec[-1]], axis=1)
    return tiles, far


def _toeplitz_rows(w_row, n_rows, origin):
    width = w_row.shape[1]
    x = jnp.broadcast_to(w_row, (n_rows, width))
    return pltpu.roll(x, (width - origin) % width, 1, stride=1, stride_axis=0)


def _toeplitz_kernel(d_ref, o_ref):
    bq, bk = o_ref.shape[2], o_ref.shape[3]
    o_ref[0, 0] = _toeplitz_rows(d_ref[0, 0], bq, bq - 1)[:, :bk]


def _toeplitz(w, n_rows, n_cols):
    na, nb, ln = w.shape
    width = -(-ln // LANES) * LANES
    wp = jnp.pad(w, ((0, 0), (0, 0), (0, width - ln))).reshape(na, nb, 1, width)
    return pl.pallas_call(
        _toeplitz_kernel,
        grid=(na, nb),
        in_specs=[pl.BlockSpec((1, 1, 1, width), lambda a, b: (a, b, 0, 0))],
        out_specs=pl.BlockSpec((1, 1, n_rows, n_cols), lambda a, b: (a, b, 0, 0)),
        out_shape=jax.ShapeDtypeStruct((na, nb, n_rows, n_cols), F32),
        compiler_params=_cparams(("parallel", "parallel")),
        name="toeplitz_tiles",
    )(wp)


def _diff_attention(u, bias_tiles, far, lam_params, subln_g, lam_init, *, bq, bk, rb, col0):
    t = u.shape[0]
    assert bk >= REL_MAX_DIST and bq % bk == 0
    ratio = bq // bk
    nk = t // bk
    n_near = ratio + 2
    assert bq % rb == 0
    kern = functools.partial(_diff_kernel, bk=bk, ratio=ratio, nk=nk, lam_init=lam_init, rb=rb)
    grid_spec = pltpu.PrefetchScalarGridSpec(
        num_scalar_prefetch=1,
        grid=(DIFF_HEADS, t // bq),
        in_specs=[pl.BlockSpec((bq, HEAD_DIM), lambda h, i, *_: (i, col0 + h)),
                  pl.BlockSpec((t, HEAD_DIM), lambda h, i, *_: (0, col0 + DIFF_HEADS + h)),
                  pl.BlockSpec((t, HEAD_DIM), lambda h, i, *_: (0, col0 + 2 * DIFF_HEADS + h)),
                  pl.BlockSpec((1, n_near, bq, bk), lambda h, i, *_: (h, 0, 0, 0)),
                  pl.BlockSpec((4, DIFF_QK_DIM), lambda h, i, *_: (0, 0)),
                  pl.BlockSpec((1, HEAD_DIM), lambda h, i, *_: (0, 0))],
        out_specs=pl.BlockSpec((bq, HEAD_DIM), lambda h, i, *_: (i, h)),
        scratch_shapes=[pltpu.VMEM((2 * bq, HEAD_DIM), BF16),
                        pltpu.VMEM((2 * bq, LANES), F32),
                        pltpu.VMEM((2 * bq, 2 * HEAD_DIM), F32)],
    )
    return pl.pallas_call(
        kern,
        grid_spec=grid_spec,
        out_shape=jax.ShapeDtypeStruct((t, DIFF_WIDTH), BF16),
        compiler_params=_cparams(("parallel", "arbitrary")),
        name="diff_attn",
    )(far, u, u, u, bias_tiles, lam_params.astype(F32), subln_g.astype(F32).reshape(1, HEAD_DIM))


CONV_HALO = 16


def _conv_silu_kernel(prev_ref, cur_ref, next_ref, w_ref, o_ref, *, k_scale):
    i = pl.program_id(0)
    j = pl.program_id(1)
    bt = cur_ref.shape[0]
    x = cur_ref[...].astype(F32)
    p = jnp.where(i > 0, prev_ref[...].astype(F32), 0.0)
    nx = jnp.where(i < pl.num_programs(0) - 1, next_ref[...].astype(F32), 0.0)
    ext = jnp.concatenate([p, x, nx], axis=0)
    n_ext = bt + 2 * CONV_HALO
    half = MLSTM_CONV // 2
    acc = w_ref[half:half + 1, :] * x
    for tap in range(MLSTM_CONV):
        s = tap - half
        if s == 0:
            continue
        rolled = pltpu.roll(ext, (n_ext - s) % n_ext, 0)
        acc = acc + w_ref[tap:tap + 1, :] * rolled[CONV_HALO:CONV_HALO + bt]
    y = acc * jax.nn.sigmoid(acc)
    o_ref[...] = (y * jnp.where(j == 1, k_scale, 1.0)).astype(o_ref.dtype)


def _conv_silu(u, conv_w, *, bt, col0):
    t = u.shape[0]
    w = MLSTM_WIDTH
    nb = t // bt
    hb = bt // CONV_HALO
    last = t // CONV_HALO - 1
    return pl.pallas_call(
        functools.partial(_conv_silu_kernel, k_scale=HEAD_DIM ** -0.5),
        grid=(nb, 2),
        in_specs=[pl.BlockSpec((CONV_HALO, w), lambda i, j: (jnp.maximum(i * hb - 1, 0), col0 + j)),
                  pl.BlockSpec((bt, w), lambda i, j: (i, col0 + j)),
                  pl.BlockSpec((CONV_HALO, w), lambda i, j: (jnp.minimum((i + 1) * hb, last), col0 + j)),
                  pl.BlockSpec((MLSTM_CONV, w), lambda i, j: (0, j))],
        out_specs=pl.BlockSpec((bt, w), lambda i, j: (i, j)),
        out_shape=jax.ShapeDtypeStruct((t, 2 * w), BF16),
        compiler_params=_cparams(("parallel", "parallel")),
        name="mlstm_conv",
    )(u, u, u, conv_w.astype(F32))


def _split3(x):
    x1 = x.astype(BF16)
    r1 = x - x1.astype(F32)
    x2 = r1.astype(BF16)
    x3 = (r1 - x2.astype(F32)).astype(BF16)
    return x1, x2, x3


def _log_sigmoid(x):
    return jnp.minimum(x, 0.0) - jnp.log(1.0 + jnp.exp(-jnp.abs(x)))


def _mlstm_dir(q, k, v, bcol, igcol, brow, igrow, c_ref, n_ref, m_ref, idx, fwd, lc):
    t_io = lax.broadcasted_iota(jnp.int32, (lc, lc), 0)
    s_io = lax.broadcasted_iota(jnp.int32, (lc, lc), 1)
    mask = (s_io <= t_io) if fwd else (s_io >= t_io)
    dmat = jnp.where(mask, bcol - brow + igrow, -jnp.inf)
    m_d = jnp.max(dmat, axis=-1, keepdims=True)
    m_prev = m_ref[idx]
    inter = bcol + m_prev
    m_t = jnp.maximum(inter, m_d)
    w_inter = jnp.exp(inter - m_t)
    s = lax.dot_general(q, k, (((1,), (1,)), ((), ())), preferred_element_type=F32)
    p = jnp.exp(dmat - m_t) * s
    num = (w_inter * jnp.dot(q, c_ref[idx].astype(BF16), preferred_element_type=F32)
           + jnp.dot(p.astype(BF16), v, preferred_element_type=F32))
    qn = jnp.sum(q.astype(F32) * n_ref[idx], axis=-1, keepdims=True)
    den = w_inter * qn + jnp.sum(p, axis=-1, keepdims=True)
    h = num / jnp.maximum(jnp.abs(den), jnp.exp(-m_t))
    g = bcol[lc - 1:lc] if fwd else bcol[0:1]
    a = g - bcol + igcol
    m_loc = jnp.max(a, axis=0, keepdims=True)
    kw = k.astype(F32) * jnp.exp(a - m_loc)
    c_loc = lax.dot_general(kw.astype(BF16), v, (((0,), (0,)), ((), ())), preferred_element_type=F32)
    n_loc = jnp.sum(kw, axis=0, keepdims=True)
    m_new = jnp.maximum(g + m_prev, m_loc)
    s_old = jnp.exp(g + m_prev - m_new)
    s_loc = jnp.exp(m_loc - m_new)
    c_ref[idx] = s_old * c_ref[idx] + s_loc * c_loc
    n_ref[idx] = s_old * n_ref[idx] + s_loc * n_loc
    m_ref[idx] = m_new
    return h


def _mlstm_kernel(qf_ref, kf_ref, vf_ref, gf_ref, qb_ref, kb_ref, vb_ref, gb_ref, gbias_ref,
                  hf_ref, hb_ref, c_ref, n_ref, m_ref, *, lc):
    nh = MLSTM_HEADS

    @pl.when(pl.program_id(0) == 0)
    def _():
        c_ref[...] = jnp.zeros(c_ref.shape, F32)
        n_ref[...] = jnp.zeros(n_ref.shape, F32)
        m_ref[...] = jnp.zeros(m_ref.shape, F32)

    t_io = lax.broadcasted_iota(jnp.int32, (lc, lc), 0)
    s_io = lax.broadcasted_iota(jnp.int32, (lc, lc), 1)
    for fwd in (True, False):
        g_ref, q_ref, k_ref, v_ref, h_ref = ((gf_ref, qf_ref, kf_ref, vf_ref, hf_ref) if fwd
                                             else (gb_ref, qb_ref, kb_ref, vb_ref, hb_ref))
        gates = g_ref[...] + gbias_ref[...]
        tri = jnp.where((s_io <= t_io) if fwd else (s_io >= t_io), 1.0, 0.0).astype(BF16)
        cum = None
        for piece in _split3(_log_sigmoid(gates)):
            part = jnp.dot(tri, piece, preferred_element_type=F32)
            cum = part if cum is None else cum + part
        gates_t = jnp.transpose(gates)
        cum_t = jnp.transpose(cum)
        lane0 = 0 if fwd else 2 * nh
        for hh in range(nh):
            li, lb = lane0 + hh, lane0 + nh + hh
            sl = slice(hh * HEAD_DIM, (hh + 1) * HEAD_DIM)
            h = _mlstm_dir(q_ref[:, sl], k_ref[:, sl], v_ref[:, sl],
                           cum[:, lb:lb + 1], gates[:, li:li + 1], cum_t[lb:lb + 1, :], gates_t[li:li + 1, :],
                           c_ref, n_ref, m_ref, (0 if fwd else nh) + hh, fwd, lc)
            h_ref[:, sl] = h


def _mlstm_scan(qk, u, gates, gate_b, *, lc, v_col):
    t = qk.shape[0]
    w = MLSTM_WIDTH
    nc = t // lc
    gbias = jnp.pad(gate_b.astype(F32), (0, LANES - gate_b.shape[0])).reshape(1, LANES)
    fmap = lambda col: (lambda c: (c, col))
    bmap = lambda col: (lambda c: (nc - 1 - c, col))
    nhd = 2 * MLSTM_HEADS
    return pl.pallas_call(
        functools.partial(_mlstm_kernel, lc=lc),
        grid=(nc,),
        in_specs=[pl.BlockSpec((lc, w), fmap(0)), pl.BlockSpec((lc, w), fmap(1)),
                  pl.BlockSpec((lc, w), fmap(v_col)), pl.BlockSpec((lc, LANES), fmap(0)),
                  pl.BlockSpec((lc, w), bmap(0)), pl.BlockSpec((lc, w), bmap(1)),
                  pl.BlockSpec((lc, w), bmap(v_col)), pl.BlockSpec((lc, LANES), bmap(0)),
                  pl.BlockSpec((1, LANES), lambda c: (0, 0))],
        out_specs=[pl.BlockSpec((lc, w), fmap(0)), pl.BlockSpec((lc, w), bmap(0))],
        out_shape=[jax.ShapeDtypeStruct((t, w), F32), jax.ShapeDtypeStruct((t, w), F32)],
        scratch_shapes=[pltpu.VMEM((nhd, HEAD_DIM, HEAD_DIM), F32),
                        pltpu.VMEM((nhd, 1, HEAD_DIM), F32),
                        pltpu.VMEM((nhd, 1, 1), F32)],
        compiler_params=_cparams(("arbitrary",)),
        name="mlstm_scan",
    )(qk, qk, u, gates, qk, qk, u, gates, gbias)


def _mlstm_out_kernel(hf_ref, hb_ref, o_ref, g_ref, out_ref):
    for hh in range(MLSTM_HEADS):
        sl = slice(hh * HEAD_DIM, (hh + 1) * HEAD_DIM)
        x = (hf_ref[:, sl] + hb_ref[:, sl]) * jax.nn.sigmoid(o_ref[:, sl].astype(F32))
        xc = x - jnp.mean(x, axis=-1, keepdims=True)
        y = xc * lax.rsqrt(jnp.mean(xc * xc, axis=-1, keepdims=True) + LN_EPS) * g_ref[:, sl]
        out_ref[:, sl] = y.astype(out_ref.dtype)


def _mlstm_out(hf, hb, u, norm_g, *, bt, o_col):
    t, w = hf.shape
    return pl.pallas_call(
        _mlstm_out_kernel,
        grid=(t // bt,),
        in_specs=[pl.BlockSpec((bt, w), lambda i: (i, 0)), pl.BlockSpec((bt, w), lambda i: (i, 0)),
                  pl.BlockSpec((bt, w), lambda i: (i, o_col)), pl.BlockSpec((1, w), lambda i: (0, 0))],
        out_specs=pl.BlockSpec((bt, w), lambda i: (i, 0)),
        out_shape=jax.ShapeDtypeStruct((t, w), BF16),
        compiler_params=_cparams(("parallel",)),
        name="mlstm_out",
    )(hf, hb, u, norm_g.astype(F32).reshape(1, w))


def _mlstm_block(u, gates, conv_w, gate_b, norm_g, *, lc, col0):
    qk = _conv_silu(u, conv_w, bt=512, col0=col0)
    hf, hb = _mlstm_scan(qk, u, gates, gate_b, lc=lc, v_col=col0 + 2)
    return _mlstm_out(hf, hb, u, norm_g, bt=512, o_col=col0 + 3)


def _router_kernel(x_ref, w_ref, b_ref, e_ref, g_ref):
    logits = jnp.dot(x_ref[...], w_ref[...], preferred_element_type=F32) + b_ref[...]
    lane = lax.broadcasted_iota(jnp.int32, logits.shape, 1)
    work = logits
    e_out = jnp.zeros(logits.shape, jnp.int32)
    vals = []
    for kk in range(TOP_K):
        m = jnp.max(work, axis=-1, keepdims=True)
        idx = jnp.min(jnp.where(work == m, lane, LANES), axis=-1, keepdims=True)
        e_out = jnp.where(lane == kk, idx, e_out)
        vals.append(m)
        work = jnp.where(lane == idx, -jnp.inf, work)
    ex = [jnp.exp(v - vals[0]) for v in vals]
    den = ex[0]
    for e in ex[1:]:
        den = den + e
    g_out = jnp.zeros(logits.shape, F32)
    for kk in range(TOP_K):
        g_out = jnp.where(lane == kk, ex[kk] / den, g_out)
    e_ref[...] = e_out
    g_ref[...] = g_out


def _router(x16, w_router, b_router, *, bm):
    t, d = x16.shape
    n_e = w_router.shape[1]
    w = jnp.pad(w_router, ((0, 0), (0, LANES - n_e))).astype(BF16)
    b = jnp.pad(b_router.astype(F32), (0, LANES - n_e), constant_values=NEG_BIG).reshape(1, LANES)
    return pl.pallas_call(
        _router_kernel,
        grid=(t // bm,),
        in_specs=[pl.BlockSpec((bm, d), lambda i: (i, 0)),
                  pl.BlockSpec((d, LANES), lambda i: (0, 0)),
                  pl.BlockSpec((1, LANES), lambda i: (0, 0))],
        out_specs=[pl.BlockSpec((bm, LANES), lambda i: (i, 0)),
                   pl.BlockSpec((bm, LANES), lambda i: (i, 0))],
        out_shape=[jax.ShapeDtypeStruct((t, LANES), jnp.int32), jax.ShapeDtypeStruct((t, LANES), F32)],
        compiler_params=_cparams(("parallel",)),
        name="router",
    )(x16, w, b)


def _rank_kernel(e_ref, rank_ref, cnt_ref, base_ref):
    @pl.when(pl.program_id(0) == 0)
    def _():
        base_ref[...] = jnp.zeros(base_ref.shape, F32)

    e = e_ref[...]
    bt = e.shape[0]
    lane = lax.broadcasted_iota(jnp.int32, e.shape, 1)
    onehots = [jnp.where(lane == e[:, kk:kk + 1], 1.0, 0.0) for kk in range(TOP_K)]
    tot = onehots[0]
    for oh in onehots[1:]:
        tot = tot + oh
    t_io = lax.broadcasted_iota(jnp.int32, (bt, bt), 0)
    s_io = lax.broadcasted_iota(jnp.int32, (bt, bt), 1)
    earlier = jnp.where(s_io < t_io, 1.0, 0.0).astype(BF16)
    pos = jnp.dot(earlier, tot.astype(BF16), preferred_element_type=F32) + base_ref[...]
    out = jnp.zeros(e.shape, jnp.int32)
    for kk in range(TOP_K):
        rk = jnp.sum(onehots[kk] * pos, axis=-1, keepdims=True)
        out = jnp.where(lane == kk, rk.astype(jnp.int32), out)
    rank_ref[...] = out
    base_ref[...] = base_ref[...] + jnp.sum(tot, axis=0, keepdims=True)
    cnt_ref[...] = base_ref[...]


def _slot_ranks(top_e, *, bt):
    t = top_e.shape[0]
    return pl.pallas_call(
        _rank_kernel,
        grid=(t // bt,),
        in_specs=[pl.BlockSpec((bt, LANES), lambda i: (i, 0))],
        out_specs=[pl.BlockSpec((bt, LANES), lambda i: (i, 0)), pl.BlockSpec((1, LANES), lambda i: (0, 0))],
        out_shape=[jax.ShapeDtypeStruct((t, LANES), jnp.int32), jax.ShapeDtypeStruct((1, LANES), F32)],
        scratch_shapes=[pltpu.VMEM((1, LANES), F32)],
        compiler_params=_cparams(("arbitrary",)),
        name="moe_rank",
    )(top_e)


def _dispatch_plan(top_e_lanes, n_experts, blk, rank_bt):
    t = top_e_lanes.shape[0]
    n_slots = t * TOP_K
    n_blocks = -(-n_slots // blk) + n_experts
    rank_lanes, counts_f = _slot_ranks(top_e_lanes, bt=rank_bt)
    slot_e = top_e_lanes[:, :TOP_K].reshape(-1)
    rank = rank_lanes[:, :TOP_K].reshape(-1)
    counts = counts_f[0, :n_experts].astype(jnp.int32)
    onehot = (slot_e[:, None] == jnp.arange(n_experts, dtype=jnp.int32)[None, :]).astype(jnp.int32)
    padded = (counts + blk - 1) // blk * blk
    pad_end = jnp.cumsum(padded)
    pad_start = pad_end - padded
    dest = (jnp.sum(pad_start[None, :] * onehot, axis=1) + rank).astype(jnp.int32)
    slot_tok = jnp.arange(n_slots, dtype=jnp.int32) // TOP_K
    row_tok = jnp.zeros((n_blocks * blk,), jnp.int32).at[dest].set(slot_tok)
    n_act = (pad_end[-1] // blk).astype(jnp.int32)
    blk_id = jnp.minimum(jnp.arange(n_blocks, dtype=jnp.int32), n_act - 1)
    blk_expert = jnp.minimum(jnp.searchsorted(pad_end, blk_id * blk, side='right'), n_experts - 1).astype(jnp.int32)
    return row_tok, dest, blk_expert, n_act.reshape(1)


def _expert_kernel(be_ref, tok_ref, nact_ref, x_hbm, wg_ref, bg_ref, wl_ref, bl_ref, wd_ref, bd_ref, o_ref,
                   xg_ref, sem_ref, *, blk):
    b = pl.program_id(0)
    n_act = nact_ref[0]
    slot = lax.rem(b, 2)

    def row_copy(blk_idx, r, dst_slot):
        tok = tok_ref[blk_idx * blk + r]
        return pltpu.make_async_copy(x_hbm.at[pl.ds(tok, 1), :], xg_ref.at[dst_slot, pl.ds(r, 1), :],
                                     sem_ref.at[dst_slot])

    def compute(prefetch_next):
        pltpu.make_async_copy(xg_ref.at[slot], xg_ref.at[slot], sem_ref.at[slot]).wait()
        if prefetch_next:
            for r in range(blk):
                row_copy(b + 1, r, 1 - slot).start()
        xb = xg_ref[slot].astype(BF16)
        hg = jnp.minimum(jnp.dot(xb, wg_ref[0], preferred_element_type=F32) + bg_ref[0], SWIGLU_LIMIT)
        hl = jnp.clip(jnp.dot(xb, wl_ref[0], preferred_element_type=F32) + bl_ref[0], -SWIGLU_LIMIT, SWIGLU_LIMIT)
        hh = hg * jax.nn.sigmoid(SWIGLU_ALPHA * hg) * (hl + 1.0)
        o_ref[...] = jnp.dot(hh.astype(BF16), wd_ref[0], preferred_element_type=F32) + bd_ref[0]

    @pl.when(b == 0)
    def _():
        def body(r, c):
            row_copy(0, r, 0).start()
            return c
        lax.fori_loop(0, blk, body, 0, unroll=8)

    @pl.when(b + 1 < n_act)
    def _():
        compute(True)

    @pl.when(b + 1 == n_act)
    def _():
        compute(False)

    @pl.when(b >= n_act)
    def _():
        o_ref[...] = jnp.zeros(o_ref.shape, o_ref.dtype)


def _expert_ffn(x32, row_tok, blk_expert, n_act, w_glu, b_glu, w_lin, b_lin, w_down, b_down, *, blk):
    t, d = x32.shape
    n_e, _, ff = w_glu.shape
    n_blocks = blk_expert.shape[0]
    single = pl.Buffered(1)
    wmap = lambda i, be, tok, na: (be[i], 0, 0)
    grid_spec = pltpu.PrefetchScalarGridSpec(
        num_scalar_prefetch=3,
        grid=(n_blocks,),
        in_specs=[pl.BlockSpec(memory_space=pl.ANY),
                  pl.BlockSpec((1, d, ff), wmap, pipeline_mode=single),
                  pl.BlockSpec((1, 1, ff), wmap),
                  pl.BlockSpec((1, d, ff), wmap, pipeline_mode=single),
                  pl.BlockSpec((1, 1, ff), wmap),
                  pl.BlockSpec((1, ff, d), wmap, pipeline_mode=single),
                  pl.BlockSpec((1, 1, d), wmap)],
        out_specs=pl.BlockSpec((blk, d), lambda i, be, tok, na: (i, 0)),
        scratch_shapes=[pltpu.VMEM((2, blk, d), F32), pltpu.SemaphoreType.DMA((2,))],
    )
    return pl.pallas_call(
        functools.partial(_expert_kernel, blk=blk),
        grid_spec=grid_spec,
        out_shape=jax.ShapeDtypeStruct((n_blocks * blk, d), F32),
        compiler_params=_cparams(("arbitrary",)),
        name="expert_ffn",
    )(blk_expert, row_tok, n_act, x32, w_glu.astype(BF16), b_glu.astype(F32).reshape(n_e, 1, ff),
      w_lin.astype(BF16), b_lin.astype(F32).reshape(n_e, 1, ff),
      w_down.astype(BF16), b_down.astype(F32).reshape(n_e, 1, d))


def _combine_kernel(dest_ref, ys_hbm, x_ref, gate_ref, g_ref, b_ref, o32_ref, o16_ref, buf_ref, sem_ref, *, bt):
    i = pl.program_id(0)
    n = pl.num_programs(0)

    def row_copy(blk_idx, r, kk, dst_slot):
        row = dest_ref[(blk_idx * bt + r) * TOP_K + kk]
        return pltpu.make_async_copy(ys_hbm.at[pl.ds(row, 1), :], buf_ref.at[dst_slot, kk, pl.ds(r, 1), :],
                                     sem_ref.at[dst_slot])

    @pl.when(i == 0)
    def _():
        def body(r, c):
            for kk in range(TOP_K):
                row_copy(0, r, kk, 0).start()
            return c
        lax.fori_loop(0, bt, body, 0, unroll=2)

    slot = lax.rem(i, 2)
    pltpu.make_async_copy(buf_ref.at[slot], buf_ref.at[slot], sem_ref.at[slot]).wait()
    nxt = jnp.where(i + 1 < n, i + 1, 0)
    for r in range(bt):
        for kk in range(TOP_K):
            row_copy(nxt, r, kk, 1 - slot).start()
    gate = gate_ref[...]
    z = ALPHA * x_ref[...]
    for kk in range(TOP_K):
        z = z + gate[:, kk:kk + 1] * buf_ref[slot, kk]
    mu = jnp.mean(z, axis=-1, keepdims=True)
    zc = z - mu
    var = jnp.mean(zc * zc, axis=-1, keepdims=True)
    y = zc * lax.rsqrt(var + LN_EPS) * g_ref[...] + b_ref[...]
    o32_ref[...] = y
    o16_ref[...] = y.astype(BF16)

    @pl.when(i == n - 1)
    def _():
        pltpu.make_async_copy(buf_ref.at[1 - slot], buf_ref.at[1 - slot], sem_ref.at[1 - slot]).wait()


def _combine_ln(ys, dest, x32, gate, g, b, *, bt):
    t, d = x32.shape
    grid_spec = pltpu.PrefetchScalarGridSpec(
        num_scalar_prefetch=1,
        grid=(t // bt,),
        in_specs=[pl.BlockSpec(memory_space=pl.ANY),
                  pl.BlockSpec((bt, d), lambda i, dr: (i, 0)),
                  pl.BlockSpec((bt, LANES), lambda i, dr: (i, 0)),
                  pl.BlockSpec((1, d), lambda i, dr: (0, 0)),
                  pl.BlockSpec((1, d), lambda i, dr: (0, 0))],
        out_specs=[pl.BlockSpec((bt, d), lambda i, dr: (i, 0)),
                   pl.BlockSpec((bt, d), lambda i, dr: (i, 0))],
        scratch_shapes=[pltpu.VMEM((2, TOP_K, bt, d), F32), pltpu.SemaphoreType.DMA((2,))],
    )
    return pl.pallas_call(
        functools.partial(_combine_kernel, bt=bt),
        grid_spec=grid_spec,
        out_shape=[jax.ShapeDtypeStruct((t, d), F32), jax.ShapeDtypeStruct((t, d), BF16)],
        compiler_params=_cparams(("arbitrary",)),
        name="moe_combine_ln",
    )(dest, ys, x32, gate, g.astype(F32).reshape(1, d), b.astype(F32).reshape(1, d))


def _moe_block(x32, x16, w_router, b_router, w_glu, b_glu, w_lin, b_lin, w_down, b_down, ln_g, ln_b,
               *, blk, bt, router_bm):
    top_e, gate = _router(x16, w_router, b_router, bm=router_bm)
    row_tok, dest, blk_expert, n_act = _dispatch_plan(top_e, w_glu.shape[0], blk, router_bm)
    ys = _expert_ffn(x32, row_tok, blk_expert, n_act, w_glu, b_glu, w_lin, b_lin, w_down, b_down, blk=blk)
    return _combine_ln(ys, dest, x32, gate, ln_g, ln_b, bt=bt)


def kernel(x, w_in, nat_rpb, nat_norm_g, diff_lambda, diff_subln_g, t5_table, ml_conv_w, ml_gate_b, ml_norm_g,
           w_out, ln_mix_g, ln_mix_b, router_w, router_b, exp_w_glu, exp_b_glu, exp_w_lin, exp_b_lin,
           exp_w_down, exp_b_down, ln_ffn_g, ln_ffn_b):
    bsz, t, d = x.shape
    assert bsz == 1
    x32 = x.reshape(t, d)
    x16 = x32.astype(BF16)
    diff_bq, diff_bk = 1024, 1024
    bias_tiles, far = _diff_bias_tiles(t5_table, diff_bq, diff_bk)
    ml_col0 = (3 * NAT_WIDTH + 3 * DIFF_WIDTH) // MLSTM_WIDTH
    for l in range(DEPTH):
        w_main = w_in[l][:, :MAIN_WIDTH].astype(BF16)
        w_gate = jnp.pad(w_in[l][:, MAIN_WIDTH:], ((0, 0), (0, LANES - 4 * MLSTM_HEADS))).astype(BF16)
        u = _matmul(x16, w_main, bm=1024, bn=1024, out_dtype=BF16, name="inproj")
        gates = _matmul(x16, w_gate, bm=1024, bn=LANES, out_dtype=F32, name="inproj_gates")
        nat = _nat_attention(u, nat_rpb[l], nat_norm_g[l], blocks_per_step=8)
        lam_init = 0.8 - 0.6 * math.exp(-0.3 * l)
        dif = _diff_attention(u, bias_tiles, far, diff_lambda[l], diff_subln_g[l], lam_init,
                              bq=diff_bq, bk=diff_bk, rb=128, col0=3 * NAT_WIDTH // HEAD_DIM)
        ml = _mlstm_block(u, gates, ml_conv_w[l], ml_gate_b[l], ml_norm_g[l], lc=256, col0=ml_col0)
        x1_32, x1_16 = _proj_residual_ln([nat, dif, ml], w_out[l].astype(BF16), x32, ln_mix_g[l], ln_mix_b[l],
                                         bm=512, bn=512)
        x32, x16 = _moe_block(x1_32, x1_16, router_w[l], router_b[l], exp_w_glu[l], exp_b_glu[l], exp_w_lin[l],
                              exp_b_lin[l], exp_w_down[l], exp_b_down[l], ln_ffn_g[l], ln_ffn_b[l],
                              blk=256, bt=128, router_bm=512)
    return x32.reshape(bsz, t, d)
```

```python
import functools
import math

import numpy as np
import jax
import jax.numpy as jnp
from jax import lax
from jax.experimental import pallas as pl
from jax.experimental.pallas import tpu as pltpu

DEPTH = 2
HEAD_DIM = 128
NAT_HEADS = 16
DIFF_HEADS = 8
MLSTM_HEADS = 8
NAT_WIDTH = NAT_HEADS * HEAD_DIM
DIFF_WIDTH = DIFF_HEADS * HEAD_DIM
MLSTM_WIDTH = MLSTM_HEADS * HEAD_DIM
DIFF_QK_DIM = HEAD_DIM // 2
MAIN_WIDTH = 3 * NAT_WIDTH + 3 * DIFF_WIDTH + 4 * MLSTM_WIDTH
GRID_W = 64
NAT_WIN_ROWS = 8
NAT_WIN_COLS = 16
NAT_Q_ROWS = 2
NAT_KEY_ROWS = 10
REL_BUCKETS = 32
REL_MAX_DIST = 128
MLSTM_CHUNK = 64
MLSTM_CONV = 5
N_EXPERTS = 32
TOP_K = 4
SWIGLU_ALPHA = 1.702
SWIGLU_LIMIT = 7.0
MOE_BLOCK = 512
LN_EPS = 1e-5
ALPHA = (2.0 * DEPTH) ** 0.25
NEG_BIG = -1e30

LANES = 128
VMEM_LIMIT = 56 * 1024 * 1024

BF16 = jnp.bfloat16
F32 = jnp.float32


def _cparams(sem):
    return pltpu.CompilerParams(dimension_semantics=sem, vmem_limit_bytes=VMEM_LIMIT)


def _mm_kernel(x_ref, w_ref, o_ref):
    o_ref[...] = jnp.dot(x_ref[...], w_ref[...], preferred_element_type=F32).astype(o_ref.dtype)


def _matmul(x, w, *, bm, bn, out_dtype, name):
    m, k = x.shape
    n = w.shape[1]
    return pl.pallas_call(
        _mm_kernel,
        grid=(m // bm, n // bn),
        in_specs=[pl.BlockSpec((bm, k), lambda i, j: (i, 0)),
                  pl.BlockSpec((k, bn), lambda i, j: (0, j))],
        out_specs=pl.BlockSpec((bm, bn), lambda i, j: (i, j)),
        out_shape=jax.ShapeDtypeStruct((m, n), out_dtype),
        compiler_params=_cparams(("parallel", "parallel")),
        name=name,
    )(x, w)


def _inproj_kernel(x_ref, w_ref, o_ref, wb_ref):
    @pl.when(pl.program_id(1) == 0)
    def _():
        wb_ref[...] = w_ref[...].astype(BF16)

    o_ref[...] = jnp.dot(x_ref[...], wb_ref[...], preferred_element_type=F32).astype(o_ref.dtype)


def _inproj(x16, w_all, layer, n_cols, *, bm, bn):
    m, k = x16.shape
    return pl.pallas_call(
        _inproj_kernel,
        grid=(n_cols // bn, m // bm),
        in_specs=[pl.BlockSpec((bm, k), lambda j, i: (i, 0)),
                  pl.BlockSpec((None, k, bn), lambda j, i: (layer, 0, j), pipeline_mode=pl.Buffered(1))],
        out_specs=pl.BlockSpec((bm, bn), lambda j, i: (i, j)),
        out_shape=jax.ShapeDtypeStruct((m, n_cols), BF16),
        scratch_shapes=[pltpu.VMEM((k, bn), BF16)],
        compiler_params=_cparams(("arbitrary", "arbitrary")),
        name="inproj",
    )(x16, w_all)


def _proj_ln_kernel(*refs, n_parts, widths, bn, nj):
    part_refs = refs[:n_parts]
    w_ref, x_ref, g_ref, b_ref, o32_ref, o16_ref, mix_ref = refs[n_parts:]
    j = pl.program_id(1)

    @pl.when(j == 0)
    def _():
        off = 0
        for p_ref, wd in zip(part_refs, widths):
            mix_ref[:, off:off + wd] = p_ref[...]
            off += wd

    z = ALPHA * x_ref[...] + jnp.dot(mix_ref[...], w_ref[...], preferred_element_type=F32)
    for jj in range(nj):
        @pl.when(j == jj)
        def _(jj=jj):
            o32_ref[:, jj * bn:(jj + 1) * bn] = z

    @pl.when(j == nj - 1)
    def _():
        zz = o32_ref[...]
        mu = jnp.mean(zz, axis=-1, keepdims=True)
        zc = zz - mu
        var = jnp.mean(zc * zc, axis=-1, keepdims=True)
        y = zc * lax.rsqrt(var + LN_EPS) * g_ref[...] + b_ref[...]
        o32_ref[...] = y
        o16_ref[...] = y.astype(BF16)


def _proj_residual_ln(parts, w, x_res, g, b, *, bm, bn):
    t, d = x_res.shape
    widths = tuple(p.shape[1] for p in parts)
    kdim = sum(widths)
    nj = d // bn
    kern = functools.partial(_proj_ln_kernel, n_parts=len(parts), widths=widths, bn=bn, nj=nj)
    in_specs = [pl.BlockSpec((bm, wd), lambda i, j: (i, 0)) for wd in widths]
    in_specs += [pl.BlockSpec((kdim, bn), lambda i, j: (0, j)),
                 pl.BlockSpec((bm, bn), lambda i, j: (i, j)),
                 pl.BlockSpec((1, d), lambda i, j: (0, 0)),
                 pl.BlockSpec((1, d), lambda i, j: (0, 0))]
    return pl.pallas_call(
        kern,
        grid=(t // bm, nj),
        in_specs=in_specs,
        out_specs=[pl.BlockSpec((bm, d), lambda i, j: (i, 0)),
                   pl.BlockSpec((bm, d), lambda i, j: (i, 0))],
        out_shape=[jax.ShapeDtypeStruct((t, d), F32), jax.ShapeDtypeStruct((t, d), BF16)],
        scratch_shapes=[pltpu.VMEM((bm, kdim), BF16)],
        compiler_params=_cparams(("parallel", "arbitrary")),
        name="proj_ln",
    )(*parts, w, x_res, g.reshape(1, d), b.reshape(1, d))


def _nat_geometry(t):
    rows = t // GRID_W
    win_r = min(NAT_WIN_ROWS, rows)
    kr = NAT_KEY_ROWS
    assert rows >= kr and rows % NAT_Q_ROWS == 0
    n_blk = rows // NAT_Q_ROWS
    col = np.arange(GRID_W)
    c0 = np.clip(col - NAT_WIN_COLS // 2, 0, GRID_W - NAT_WIN_COLS)
    keys, ks_list, ty_list, tables = {}, [], [], []
    for i in range(n_blk):
        qr = i * NAT_Q_ROWS + np.arange(NAT_Q_ROWS)
        r0 = np.clip(qr - win_r // 2, 0, rows - win_r)
        ks = int(np.clip(r0[0], 0, rows - kr))
        key = (tuple(qr - ks), tuple(r0 - ks))
        if key not in keys:
            keys[key] = len(tables)
            q_row = np.repeat(qr, GRID_W)
            q_r0 = np.repeat(r0, GRID_W)
            q_col = np.tile(col, NAT_Q_ROWS)
            q_c0 = np.tile(c0, NAT_Q_ROWS)
            k_row = np.repeat(ks + np.arange(kr), GRID_W)
            k_col = np.tile(col, kr)
            row_ok = (k_row[None, :] >= q_r0[:, None]) & (k_row[None, :] < q_r0[:, None] + win_r)
            col_ok = (k_col[None, :] >= q_c0[:, None]) & (k_col[None, :] < q_c0[:, None] + NAT_WIN_COLS)
            dr = np.clip(k_row[None, :] - q_row[:, None] + NAT_WIN_ROWS - 1, 0, 2 * NAT_WIN_ROWS - 2)
            dc = np.clip(k_col[None, :] - q_col[:, None] + NAT_WIN_COLS - 1, 0, 2 * NAT_WIN_COLS - 2)
            tables.append((dr, dc, row_ok & col_ok))
        ks_list.append(ks)
        ty_list.append(keys[key])
    dr = np.stack([tb[0] for tb in tables]).astype(np.int32)
    dc = np.stack([tb[1] for tb in tables]).astype(np.int32)
    ok = np.stack([tb[2] for tb in tables])
    return np.asarray(ks_list, np.int32), np.asarray(ty_list, np.int32), dr, dc, ok


def _nat_bias_kernel(rp_ref, ok_ref, o_ref, *, dr_small):
    lane = lax.broadcasted_iota(jnp.int32, (GRID_W, LANES), 1)
    for ty, dr_ty in enumerate(dr_small):
        for qrl, dr_row in enumerate(dr_ty):
            for pair in range(len(dr_row) // 2):
                da, db = int(dr_row[2 * pair]), int(dr_row[2 * pair + 1])
                ya = _toeplitz_rows(rp_ref[0, da:da + 1, :], GRID_W, GRID_W - 1)
                yb = _toeplitz_rows(rp_ref[0, db:db + 1, :], GRID_W, LANES - 1)
                rows = slice(qrl * GRID_W, (qrl + 1) * GRID_W)
                cols = slice(pair * LANES, (pair + 1) * LANES)
                o_ref[ty, 0, rows, cols] = jnp.where(ok_ref[ty, rows, cols] > 0.5,
                                                     jnp.where(lane < GRID_W, ya, yb), NEG_BIG)


def _nat_bias_table(rpb, dr_small, ok):
    n_type, nq, kw = ok.shape
    assert 2 * GRID_W == LANES and NAT_KEY_ROWS % 2 == 0
    pad_c = GRID_W - NAT_WIN_COLS
    n_dr = rpb.shape[1]
    rp = jnp.pad(rpb.astype(F32), ((0, 0), (0, -n_dr % 8), (pad_c, LANES - rpb.shape[2] - pad_c)))
    return pl.pallas_call(
        functools.partial(_nat_bias_kernel, dr_small=dr_small.tolist()),
        grid=(NAT_HEADS,),
        in_specs=[pl.BlockSpec((1, rp.shape[1], LANES), lambda h: (h, 0, 0)),
                  pl.BlockSpec((n_type, nq, kw), lambda h: (0, 0, 0))],
        out_specs=pl.BlockSpec((n_type, 1, nq, kw), lambda h: (0, h, 0, 0)),
        out_shape=jax.ShapeDtypeStruct((n_type, NAT_HEADS, nq, kw), F32),
        compiler_params=_cparams(("parallel",)),
        name="nat_bias_table",
    )(rp, jnp.asarray(ok.astype(np.float32)))


def _nat_kernel(ks_ref, ty_ref, q_ref, k_ref, v_ref, bias_ref, g_ref, o_ref, *, blocks_per_step, kw):
    nq = NAT_Q_ROWS * GRID_W
    base = pl.program_id(1) * blocks_per_step
    scale = HEAD_DIM ** -0.5
    blocks = range(blocks_per_step)
    k0s = [pl.multiple_of(ks_ref[base + jj] * GRID_W, GRID_W) for jj in blocks]
    ss = [lax.dot_general(q_ref[jj * nq:(jj + 1) * nq, :], k_ref[pl.ds(k0s[jj], kw), :],
                          (((1,), (1,)), ((), ())), preferred_element_type=F32) * scale
          + bias_ref[ty_ref[base + jj], 0] for jj in blocks]
    ps = [jnp.exp(s - jnp.max(s, axis=-1, keepdims=True)) for s in ss]
    ls = [jnp.sum(p, axis=-1, keepdims=True) for p in ps]
    os = [jnp.dot(ps[jj].astype(BF16), v_ref[pl.ds(k0s[jj], kw), :], preferred_element_type=F32) / ls[jj]
          for jj in blocks]
    for jj in blocks:
        o = os[jj]
        ms = jnp.mean(o * o, axis=-1, keepdims=True)
        o_ref[jj * nq:(jj + 1) * nq, :] = (o * lax.rsqrt(ms + LN_EPS) * g_ref[0]).astype(o_ref.dtype)


def _nat_attention(u, rpb, norm_g, *, blocks_per_step):
    t = u.shape[0]
    ks, ty, dr, dc, ok = _nat_geometry(t)
    n_blk = ks.shape[0]
    kw = NAT_KEY_ROWS * GRID_W
    nq = NAT_Q_ROWS * GRID_W
    n_type = ok.shape[0]
    bias = _nat_bias_table(rpb, dr[:, ::GRID_W, ::GRID_W], ok)
    g = norm_g.astype(F32).reshape(NAT_HEADS, 1, HEAD_DIM)
    rows_per_step = blocks_per_step * nq
    kern = functools.partial(_nat_kernel, blocks_per_step=blocks_per_step, kw=kw)
    grid_spec = pltpu.PrefetchScalarGridSpec(
        num_scalar_prefetch=2,
        grid=(NAT_HEADS, n_blk // blocks_per_step),
        in_specs=[pl.BlockSpec((rows_per_step, HEAD_DIM), lambda h, g_, *_: (g_, h)),
                  pl.BlockSpec((t, HEAD_DIM), lambda h, g_, *_: (0, NAT_HEADS + h)),
                  pl.BlockSpec((t, HEAD_DIM), lambda h, g_, *_: (0, 2 * NAT_HEADS + h)),
                  pl.BlockSpec((n_type, 1, nq, kw), lambda h, g_, *_: (0, h, 0, 0)),
                  pl.BlockSpec((1, 1, HEAD_DIM), lambda h, g_, *_: (h, 0, 0))],
        out_specs=pl.BlockSpec((rows_per_step, HEAD_DIM), lambda h, g_, *_: (g_, h)),
    )
    return pl.pallas_call(
        kern,
        grid_spec=grid_spec,
        out_shape=jax.ShapeDtypeStruct((t, NAT_WIDTH), BF16),
        compiler_params=_cparams(("parallel", "arbitrary")),
        name="nat_attn",
    )(jnp.asarray(ks), jnp.asarray(ty), u, u, u, bias, g)


def _t5_bucket(rel):
    half = REL_BUCKETS // 2
    max_exact = half // 2
    ret = jnp.where(rel > 0, half, 0)
    n = jnp.abs(rel)
    n_f = jnp.maximum(n, 1).astype(jnp.float32)
    large = max_exact + (jnp.log(n_f / max_exact) / math.log(REL_MAX_DIST / max_exact)
                         * (half - max_exact)).astype(jnp.int32)
    large = jnp.minimum(large, half - 1)
    return ret + jnp.where(n < max_exact, n, large)


def _diff_kernel(far_ref, q_ref, k_ref, v_ref, bias_ref, lp_ref, g_ref, o_ref,
                 qs_ref, m_ref, acc_ref, *, bk, ratio, nk, lam_init, rb):
    h = pl.program_id(0)
    qi = pl.program_id(1)
    bq = q_ref.shape[0]
    n_rb = 2 * bq // rb

    q = q_ref[...]
    lane = lax.broadcasted_iota(jnp.int32, q.shape, 1)
    qsc = q * jnp.asarray(DIFF_QK_DIM ** -0.5, q.dtype)
    zero = jnp.zeros_like(qsc)
    qs_ref[0:bq, :] = jnp.where(lane < DIFF_QK_DIM, qsc, zero)
    qs_ref[bq:2 * bq, :] = jnp.where(lane >= DIFF_QK_DIM, qsc, zero)
    m_ref[...] = jnp.full(m_ref.shape, -jnp.inf, F32)
    acc_ref[...] = jnp.zeros(acc_ref.shape, F32)

    def step(ki, near_idx, const):
        k0 = pl.multiple_of(ki * bk, bk)
        k = k_ref[pl.ds(k0, bk), :]
        v = v_ref[pl.ds(k0, bk), :]
        v_ext = jnp.concatenate([v, jnp.ones_like(v)], axis=1)
        for r in range(n_rb):
            rows = slice(r * rb, (r + 1) * rb)
            s = lax.dot_general(qs_ref[rows, :], k, (((1,), (1,)), ((), ())), preferred_element_type=F32)
            if near_idx is not None:
                r0 = (r * rb) % bq
                s = s + bias_ref[0, near_idx, r0:r0 + rb, :]
            mt = jnp.max(s, axis=-1, keepdims=True)
            if const is not None:
                mt = mt + const
            m_prev = m_ref[rows, :]
            m_new = jnp.maximum(m_prev, mt)
            a = jnp.exp(m_prev - m_new)
            shift = m_new if const is None else m_new - const
            p = jnp.exp(s - jnp.tile(shift, (1, bk // LANES)))
            acc_ref[rows, :] = (jnp.tile(a, (1, 2)) * acc_ref[rows, :]
                                + jnp.dot(p.astype(BF16), v_ext, preferred_element_type=F32))
            m_ref[rows, :] = m_new

    c_left = far_ref[h, 0]
    c_right = far_ref[h, 1]
    first_near = qi * ratio - 1
    n_near = ratio + 2

    def left_body(ki, carry):
        step(ki, None, c_left)
        return carry

    lax.fori_loop(0, jnp.clip(first_near, 0, nk), left_body, 0)
    for tt in range(n_near):
        ki = first_near + tt

        @pl.when((ki >= 0) & (ki < nk))
        def _(ki=ki, tt=tt):
            step(ki, tt, None)

    def right_body(ki, carry):
        step(ki, None, c_right)
        return carry

    lax.fori_loop(jnp.clip(first_near + n_near, 0, nk), nk, right_body, 0)

    acc = acc_ref[...]
    o = acc[:, 0:HEAD_DIM] / acc[:, HEAD_DIM:2 * HEAD_DIM]
    lp = lp_ref[...]
    lam = (jnp.exp(jnp.sum(lp[0:1] * lp[1:2], axis=-1, keepdims=True))
           - jnp.exp(jnp.sum(lp[2:3] * lp[3:4], axis=-1, keepdims=True)) + lam_init)
    d = o[0:bq] - lam * o[bq:2 * bq]
    ms = jnp.mean(d * d, axis=-1, keepdims=True)
    o_ref[...] = (d * lax.rsqrt(ms + LN_EPS) * g_ref[...] * (1.0 - lam_init)).astype(o_ref.dtype)


def _diff_bias_tiles(t5_table, bq, bk):
    ratio = bq // bk
    span = bq + bk + 1
    rel = jnp.arange(-span, span + 1, dtype=jnp.int32)
    vec = jnp.take(t5_table.astype(F32), _t5_bucket(rel), axis=0)
    vec_t = vec.T
    diags = jnp.stack([vec_t[:, (tt - 1) * bk + span - (bq - 1):(tt - 1) * bk + span + bk]
                       for tt in range(ratio + 2)], axis=1)
    tiles = _toeplitz(diags, bq, bk)
    far = jnp.stack([vec[0], vec[-1]], axis=1)
    return tiles, far


def _toeplitz_rows(w_row, n_rows, origin):
    width = w_row.shape[1]
    x = jnp.broadcast_to(w_row, (n_rows, width))
    return pltpu.roll(x, (width - origin) % width, 1, stride=1, stride_axis=0)


def _toeplitz_kernel(d_ref, o_ref):
    bq, bk = o_ref.shape[2], o_ref.shape[3]
    o_ref[0, 0] = _toeplitz_rows(d_ref[0, 0], bq, bq - 1)[:, :bk]


def _toeplitz(w, n_rows, n_cols):
    na, nb, ln = w.shape
    width = -(-ln // LANES) * LANES
    wp = jnp.pad(w, ((0, 0), (0, 0), (0, width - ln))).reshape(na, nb, 1, width)
    return pl.pallas_call(
        _toeplitz_kernel,
        grid=(na, nb),
        in_specs=[pl.BlockSpec((1, 1, 1, width), lambda a, b: (a, b, 0, 0))],
        out_specs=pl.BlockSpec((1, 1, n_rows, n_cols), lambda a, b: (a, b, 0, 0)),
        out_shape=jax.ShapeDtypeStruct((na, nb, n_rows, n_cols), F32),
        compiler_params=_cparams(("parallel", "parallel")),
        name="toeplitz_tiles",
    )(wp)


def _diff_attention(u, bias_tiles, far, lam_params, subln_g, lam_init, *, bq, bk, rb, col0):
    t = u.shape[0]
    assert bk >= REL_MAX_DIST and bq % bk == 0
    ratio = bq // bk
    nk = t // bk
    n_near = ratio + 2
    assert bq % rb == 0
    kern = functools.partial(_diff_kernel, bk=bk, ratio=ratio, nk=nk, lam_init=lam_init, rb=rb)
    grid_spec = pltpu.PrefetchScalarGridSpec(
        num_scalar_prefetch=1,
        grid=(DIFF_HEADS, t // bq),
        in_specs=[pl.BlockSpec((bq, HEAD_DIM), lambda h, i, *_: (i, col0 + h)),
                  pl.BlockSpec((t, HEAD_DIM), lambda h, i, *_: (0, col0 + DIFF_HEADS + h)),
                  pl.BlockSpec((t, HEAD_DIM), lambda h, i, *_: (0, col0 + 2 * DIFF_HEADS + h)),
                  pl.BlockSpec((1, n_near, bq, bk), lambda h, i, *_: (h, 0, 0, 0)),
                  pl.BlockSpec((4, DIFF_QK_DIM), lambda h, i, *_: (0, 0)),
                  pl.BlockSpec((1, HEAD_DIM), lambda h, i, *_: (0, 0))],
        out_specs=pl.BlockSpec((bq, HEAD_DIM), lambda h, i, *_: (i, h)),
        scratch_shapes=[pltpu.VMEM((2 * bq, HEAD_DIM), BF16),
                        pltpu.VMEM((2 * bq, LANES), F32),
                        pltpu.VMEM((2 * bq, 2 * HEAD_DIM), F32)],
    )
    return pl.pallas_call(
        kern,
        grid_spec=grid_spec,
        out_shape=jax.ShapeDtypeStruct((t, DIFF_WIDTH), BF16),
        compiler_params=_cparams(("parallel", "arbitrary")),
        name="diff_attn",
    )(far, u, u, u, bias_tiles, lam_params.astype(F32), subln_g.astype(F32).reshape(1, HEAD_DIM))


CONV_HALO = 16


def _conv_silu_kernel(prev_ref, cur_ref, next_ref, w_ref, o_ref, *, k_scale):
    i = pl.program_id(0)
    j = pl.program_id(1)
    bt = cur_ref.shape[0]
    x = cur_ref[...].astype(F32)
    p = jnp.where(i > 0, prev_ref[...].astype(F32), 0.0)
    nx = jnp.where(i < pl.num_programs(0) - 1, next_ref[...].astype(F32), 0.0)
    ext = jnp.concatenate([p, x, nx], axis=0)
    n_ext = bt + 2 * CONV_HALO
    half = MLSTM_CONV // 2
    acc = w_ref[half:half + 1, :] * x
    for tap in range(MLSTM_CONV):
        s = tap - half
        if s == 0:
            continue
        rolled = pltpu.roll(ext, (n_ext - s) % n_ext, 0)
        acc = acc + w_ref[tap:tap + 1, :] * rolled[CONV_HALO:CONV_HALO + bt]
    y = acc * jax.nn.sigmoid(acc)
    o_ref[...] = (y * jnp.where(j == 1, k_scale, 1.0)).astype(o_ref.dtype)


def _conv_silu(u, conv_w, *, bt, col0):
    t = u.shape[0]
    w = MLSTM_WIDTH
    nb = t // bt
    hb = bt // CONV_HALO
    last = t // CONV_HALO - 1
    return pl.pallas_call(
        functools.partial(_conv_silu_kernel, k_scale=HEAD_DIM ** -0.5),
        grid=(nb, 2),
        in_specs=[pl.BlockSpec((CONV_HALO, w), lambda i, j: (jnp.maximum(i * hb - 1, 0), col0 + j)),
                  pl.BlockSpec((bt, w), lambda i, j: (i, col0 + j)),
                  pl.BlockSpec((CONV_HALO, w), lambda i, j: (jnp.minimum((i + 1) * hb, last), col0 + j)),
                  pl.BlockSpec((MLSTM_CONV, w), lambda i, j: (0, j))],
        out_specs=pl.BlockSpec((bt, w), lambda i, j: (i, j)),
        out_shape=jax.ShapeDtypeStruct((t, 2 * w), BF16),
        compiler_params=_cparams(("parallel", "parallel")),
        name="mlstm_conv",
    )(u, u, u, conv_w.astype(F32))


def _split3(x):
    x1 = x.astype(BF16)
    r1 = x - x1.astype(F32)
    x2 = r1.astype(BF16)
    x3 = (r1 - x2.astype(F32)).astype(BF16)
    return x1, x2, x3


def _log_sigmoid(x):
    return jnp.minimum(x, 0.0) - jnp.log(1.0 + jnp.exp(-jnp.abs(x)))


def _mlstm_units(units, c_ref, n_ref, m_ref, lc):
    nt_dims = (((1,), (1,)), ((), ()))
    tn_dims = (((0,), (0,)), ((), ()))
    rd = lambda u, name: u[name][:, u["sl"]]
    dmats = [jnp.where(u["mask"], u["bcol"] - u["brow"] + u["igrow"], -jnp.inf) for u in units]
    m_prevs = [m_ref[u["idx"]] for u in units]
    inters = [u["bcol"] + mp for u, mp in zip(units, m_prevs)]
    m_ts = [jnp.maximum(it, jnp.max(d, axis=-1, keepdims=True)) for it, d in zip(inters, dmats)]
    w_inters = [jnp.exp(it - mt) for it, mt in zip(inters, m_ts)]
    ss = [lax.dot_general(rd(u, "q_ref"), rd(u, "k_ref"), nt_dims, preferred_element_type=F32) for u in units]
    ps = [jnp.exp(d - mt) * s for d, mt, s in zip(dmats, m_ts, ss)]
    nums = [wi * jnp.dot(rd(u, "q_ref"), c_ref[u["idx"]].astype(BF16), preferred_element_type=F32)
            + jnp.dot(p.astype(BF16), rd(u, "v_ref"), preferred_element_type=F32)
            for u, wi, p in zip(units, w_inters, ps)]
    dens = [wi * jnp.sum(rd(u, "q_ref").astype(F32) * n_ref[u["idx"]], axis=-1, keepdims=True)
            + jnp.sum(p, axis=-1, keepdims=True) for u, wi, p in zip(units, w_inters, ps)]
    for u, num, den, mt in zip(units, nums, dens, m_ts):
        u["h_ref"][:, u["sl"]] = num / jnp.maximum(jnp.abs(den), jnp.exp(-mt))
    gs = [u["bcol"][lc - 1:lc] if u["fwd"] else u["bcol"][0:1] for u in units]
    a_s = [g - u["bcol"] + u["igcol"] for u, g in zip(units, gs)]
    m_locs = [jnp.max(a, axis=0, keepdims=True) for a in a_s]
    kws = [rd(u, "k_ref").astype(F32) * jnp.exp(a - ml) for u, a, ml in zip(units, a_s, m_locs)]
    c_locs = [lax.dot_general(kw.astype(BF16), rd(u, "v_ref"), tn_dims, preferred_element_type=F32)
              for u, kw in zip(units, kws)]
    for u, g, mp, ml, kw, c_loc in zip(units, gs, m_prevs, m_locs, kws, c_locs):
        m_new = jnp.maximum(g + mp, ml)
        s_old = jnp.exp(g + mp - m_new)
        s_loc = jnp.exp(ml - m_new)
        c_ref[u["idx"]] = s_old * c_ref[u["idx"]] + s_loc * c_loc
        n_ref[u["idx"]] = s_old * n_ref[u["idx"]] + s_loc * jnp.sum(kw, axis=0, keepdims=True)
        m_ref[u["idx"]] = m_new


def _mlstm_kernel(qf_ref, kf_ref, vf_ref, gf_ref, qb_ref, kb_ref, vb_ref, gb_ref, gbias_ref,
                  hf_ref, hb_ref, c_ref, n_ref, m_ref, *, lc):
    nh = MLSTM_HEADS

    @pl.when(pl.program_id(0) == 0)
    def _():
        c_ref[...] = jnp.zeros(c_ref.shape, F32)
        n_ref[...] = jnp.zeros(n_ref.shape, F32)
        m_ref[...] = jnp.zeros(m_ref.shape, F32)

    t_io = lax.broadcasted_iota(jnp.int32, (lc, lc), 0)
    s_io = lax.broadcasted_iota(jnp.int32, (lc, lc), 1)
    units = []
    for fwd in (True, False):
        g_ref, q_ref, k_ref, v_ref, h_ref = ((gf_ref, qf_ref, kf_ref, vf_ref, hf_ref) if fwd
                                             else (gb_ref, qb_ref, kb_ref, vb_ref, hb_ref))
        gates = g_ref[...] + gbias_ref[...]
        mask = (s_io <= t_io) if fwd else (s_io >= t_io)
        tri = jnp.where(mask, 1.0, 0.0).astype(BF16)
        cum = None
        for piece in _split3(_log_sigmoid(gates)):
            part = jnp.dot(tri, piece, preferred_element_type=F32)
            cum = part if cum is None else cum + part
        gates_t = jnp.transpose(gates)
        cum_t = jnp.transpose(cum)
        lane0 = 0 if fwd else 2 * nh
        for hh in range(nh):
            li, lb = lane0 + hh, lane0 + nh + hh
            units.append(dict(fwd=fwd, idx=(0 if fwd else nh) + hh, sl=slice(hh * HEAD_DIM, (hh + 1) * HEAD_DIM),
                              q_ref=q_ref, k_ref=k_ref, v_ref=v_ref, h_ref=h_ref, mask=mask,
                              bcol=cum[:, lb:lb + 1], igcol=gates[:, li:li + 1],
                              brow=cum_t[lb:lb + 1, :], igrow=gates_t[li:li + 1, :]))
    _mlstm_units(units, c_ref, n_ref, m_ref, lc)


def _mlstm_scan(qk, u, gates, gate_b, *, lc, v_col):
    t = qk.shape[0]
    w = MLSTM_WIDTH
    nc = t // lc
    gbias = jnp.pad(gate_b.astype(F32), (0, LANES - gate_b.shape[0])).reshape(1, LANES)
    fmap = lambda col: (lambda c: (c, col))
    bmap = lambda col: (lambda c: (nc - 1 - c, col))
    nhd = 2 * MLSTM_HEADS
    return pl.pallas_call(
        functools.partial(_mlstm_kernel, lc=lc),
        grid=(nc,),
        in_specs=[pl.BlockSpec((lc, w), fmap(0)), pl.BlockSpec((lc, w), fmap(1)),
                  pl.BlockSpec((lc, w), fmap(v_col)), pl.BlockSpec((lc, LANES), fmap(0)),
                  pl.BlockSpec((lc, w), bmap(0)), pl.BlockSpec((lc, w), bmap(1)),
                  pl.BlockSpec((lc, w), bmap(v_col)), pl.BlockSpec((lc, LANES), bmap(0)),
                  pl.BlockSpec((1, LANES), lambda c: (0, 0))],
        out_specs=[pl.BlockSpec((lc, w), fmap(0)), pl.BlockSpec((lc, w), bmap(0))],
        out_shape=[jax.ShapeDtypeStruct((t, w), F32), jax.ShapeDtypeStruct((t, w), F32)],
        scratch_shapes=[pltpu.VMEM((nhd, HEAD_DIM, HEAD_DIM), F32),
                        pltpu.VMEM((nhd, 1, HEAD_DIM), F32),
                        pltpu.VMEM((nhd, 1, 1), F32)],
        compiler_params=_cparams(("arbitrary",)),
        name="mlstm_scan",
    )(qk, qk, u, gates, qk, qk, u, gates, gbias)


def _mlstm_out_kernel(hf_ref, hb_ref, o_ref, g_ref, out_ref):
    for hh in range(MLSTM_HEADS):
        sl = slice(hh * HEAD_DIM, (hh + 1) * HEAD_DIM)
        x = (hf_ref[:, sl] + hb_ref[:, sl]) * jax.nn.sigmoid(o_ref[:, sl].astype(F32))
        xc = x - jnp.mean(x, axis=-1, keepdims=True)
        y = xc * lax.rsqrt(jnp.mean(xc * xc, axis=-1, keepdims=True) + LN_EPS) * g_ref[:, sl]
        out_ref[:, sl] = y.astype(out_ref.dtype)


def _mlstm_out(hf, hb, u, norm_g, *, bt, o_col):
    t, w = hf.shape
    return pl.pallas_call(
        _mlstm_out_kernel,
        grid=(t // bt,),
        in_specs=[pl.BlockSpec((bt, w), lambda i: (i, 0)), pl.BlockSpec((bt, w), lambda i: (i, 0)),
                  pl.BlockSpec((bt, w), lambda i: (i, o_col)), pl.BlockSpec((1, w), lambda i: (0, 0))],
        out_specs=pl.BlockSpec((bt, w), lambda i: (i, 0)),
        out_shape=jax.ShapeDtypeStruct((t, w), BF16),
        compiler_params=_cparams(("parallel",)),
        name="mlstm_out",
    )(hf, hb, u, norm_g.astype(F32).reshape(1, w))


def _mlstm_block(u, gates, conv_w, gate_b, norm_g, *, lc, col0):
    qk = _conv_silu(u, conv_w, bt=512, col0=col0)
    hf, hb = _mlstm_scan(qk, u, gates, gate_b, lc=lc, v_col=col0 + 2)
    return _mlstm_out(hf, hb, u, norm_g, bt=512, o_col=col0 + 3)


def _router_kernel(x_ref, w_ref, b_ref, e_ref, g_ref):
    logits = jnp.dot(x_ref[...], w_ref[...], preferred_element_type=F32) + b_ref[...]
    lane = lax.broadcasted_iota(jnp.int32, logits.shape, 1)
    work = logits
    e_out = jnp.zeros(logits.shape, jnp.int32)
    vals = []
    for kk in range(TOP_K):
        m = jnp.max(work, axis=-1, keepdims=True)
        idx = jnp.min(jnp.where(work == m, lane, LANES), axis=-1, keepdims=True)
        e_out = jnp.where(lane == kk, idx, e_out)
        vals.append(m)
        work = jnp.where(lane == idx, -jnp.inf, work)
    ex = [jnp.exp(v - vals[0]) for v in vals]
    den = ex[0]
    for e in ex[1:]:
        den = den + e
    g_out = jnp.zeros(logits.shape, F32)
    for kk in range(TOP_K):
        g_out = jnp.where(lane == kk, ex[kk] / den, g_out)
    e_ref[...] = e_out
    g_ref[...] = g_out


def _router(x16, w_router, b_router, *, bm):
    t, d = x16.shape
    n_e = w_router.shape[1]
    w = jnp.pad(w_router, ((0, 0), (0, LANES - n_e))).astype(BF16)
    b = jnp.pad(b_router.astype(F32), (0, LANES - n_e), constant_values=NEG_BIG).reshape(1, LANES)
    return pl.pallas_call(
        _router_kernel,
        grid=(t // bm,),
        in_specs=[pl.BlockSpec((bm, d), lambda i: (i, 0)),
                  pl.BlockSpec((d, LANES), lambda i: (0, 0)),
                  pl.BlockSpec((1, LANES), lambda i: (0, 0))],
        out_specs=[pl.BlockSpec((bm, LANES), lambda i: (i, 0)),
                   pl.BlockSpec((bm, LANES), lambda i: (i, 0))],
        out_shape=[jax.ShapeDtypeStruct((t, LANES), jnp.int32), jax.ShapeDtypeStruct((t, LANES), F32)],
        compiler_params=_cparams(("parallel",)),
        name="router",
    )(x16, w, b)


def _rank_kernel(e_ref, rank_ref, cnt_ref, base_ref):
    @pl.when(pl.program_id(0) == 0)
    def _():
        base_ref[...] = jnp.zeros(base_ref.shape, F32)

    e = e_ref[...]
    bt = e.shape[0]
    lane = lax.broadcasted_iota(jnp.int32, e.shape, 1)
    onehots = [jnp.where(lane == e[:, kk:kk + 1], 1.0, 0.0) for kk in range(TOP_K)]
    tot = onehots[0]
    for oh in onehots[1:]:
        tot = tot + oh
    t_io = lax.broadcasted_iota(jnp.int32, (bt, bt), 0)
    s_io = lax.broadcasted_iota(jnp.int32, (bt, bt), 1)
    earlier = jnp.where(s_io < t_io, 1.0, 0.0).astype(BF16)
    pos = jnp.dot(earlier, tot.astype(BF16), preferred_element_type=F32) + base_ref[...]
    out = jnp.zeros(e.shape, jnp.int32)
    for kk in range(TOP_K):
        rk = jnp.sum(onehots[kk] * pos, axis=-1, keepdims=True)
        out = jnp.where(lane == kk, rk.astype(jnp.int32), out)
    rank_ref[...] = out
    base_ref[...] = base_ref[...] + jnp.sum(tot, axis=0, keepdims=True)
    cnt_ref[...] = base_ref[...]


def _slot_ranks(top_e, *, bt):
    t = top_e.shape[0]
    return pl.pallas_call(
        _rank_kernel,
        grid=(t // bt,),
        in_specs=[pl.BlockSpec((bt, LANES), lambda i: (i, 0))],
        out_specs=[pl.BlockSpec((bt, LANES), lambda i: (i, 0)), pl.BlockSpec((1, LANES), lambda i: (0, 0))],
        out_shape=[jax.ShapeDtypeStruct((t, LANES), jnp.int32), jax.ShapeDtypeStruct((1, LANES), F32)],
        scratch_shapes=[pltpu.VMEM((1, LANES), F32)],
        compiler_params=_cparams(("arbitrary",)),
        name="moe_rank",
    )(top_e)


def _dispatch_plan(top_e_lanes, n_experts, blk, rank_bt):
    t = top_e_lanes.shape[0]
    n_slots = t * TOP_K
    n_blocks = -(-n_slots // blk) + n_experts
    rank_lanes, counts_f = _slot_ranks(top_e_lanes, bt=rank_bt)
    slot_e = top_e_lanes[:, :TOP_K].reshape(-1)
    rank = rank_lanes[:, :TOP_K].reshape(-1)
    counts = counts_f[0, :n_experts].astype(jnp.int32)
    onehot = (slot_e[:, None] == jnp.arange(n_experts, dtype=jnp.int32)[None, :]).astype(jnp.int32)
    padded = (counts + blk - 1) // blk * blk
    pad_end = jnp.cumsum(padded)
    pad_start = pad_end - padded
    dest = (jnp.sum(pad_start[None, :] * onehot, axis=1) + rank).astype(jnp.int32)
    slot_tok = jnp.arange(n_slots, dtype=jnp.int32) // TOP_K
    row_tok = jnp.zeros((n_blocks * blk,), jnp.int32).at[dest].set(slot_tok)
    n_act = (pad_end[-1] // blk).astype(jnp.int32)
    blk_id = jnp.minimum(jnp.arange(n_blocks, dtype=jnp.int32), n_act - 1)
    blk_expert = jnp.minimum(jnp.sum((pad_end[None, :] <= (blk_id * blk)[:, None]).astype(jnp.int32), axis=1),
                             n_experts - 1)
    return row_tok, dest, blk_expert, n_act.reshape(1)


def _expert_kernel(be_ref, tok_ref, nact_ref, x_hbm, wg_ref, bg_ref, wl_ref, bl_ref, wd_ref, bd_ref, o_ref,
                   xg_ref, sem_ref, *, blk):
    b = pl.program_id(0)
    n_act = nact_ref[0]
    slot = lax.rem(b, 2)

    def row_copy(blk_idx, r, dst_slot):
        tok = tok_ref[blk_idx * blk + r]
        return pltpu.make_async_copy(x_hbm.at[pl.ds(tok, 1), :], xg_ref.at[dst_slot, pl.ds(r, 1), :],
                                     sem_ref.at[dst_slot])

    def compute(prefetch_next):
        pltpu.make_async_copy(xg_ref.at[slot], xg_ref.at[slot], sem_ref.at[slot]).wait()
        if prefetch_next:
            for r in range(blk):
                row_copy(b + 1, r, 1 - slot).start()
        xb = xg_ref[slot].astype(BF16)
        hg = jnp.minimum(jnp.dot(xb, wg_ref[0], preferred_element_type=F32) + bg_ref[0], SWIGLU_LIMIT)
        hl = jnp.clip(jnp.dot(xb, wl_ref[0], preferred_element_type=F32) + bl_ref[0], -SWIGLU_LIMIT, SWIGLU_LIMIT)
        hh = hg * jax.nn.sigmoid(SWIGLU_ALPHA * hg) * (hl + 1.0)
        o_ref[...] = jnp.dot(hh.astype(BF16), wd_ref[0], preferred_element_type=F32) + bd_ref[0]

    @pl.when(b == 0)
    def _():
        def body(r, c):
            row_copy(0, r, 0).start()
            return c
        lax.fori_loop(0, blk, body, 0, unroll=8)

    @pl.when(b + 1 < n_act)
    def _():
        compute(True)

    @pl.when(b + 1 == n_act)
    def _():
        compute(False)

    @pl.when(b >= n_act)
    def _():
        o_ref[...] = jnp.zeros(o_ref.shape, o_ref.dtype)


def _expert_ffn(x32, row_tok, blk_expert, n_act, w_glu, b_glu, w_lin, b_lin, w_down, b_down, *, blk):
    t, d = x32.shape
    n_e, _, ff = w_glu.shape
    n_blocks = blk_expert.shape[0]
    single = pl.Buffered(1)
    wmap = lambda i, be, tok, na: (be[i], 0, 0)
    grid_spec = pltpu.PrefetchScalarGridSpec(
        num_scalar_prefetch=3,
        grid=(n_blocks,),
        in_specs=[pl.BlockSpec(memory_space=pl.ANY),
                  pl.BlockSpec((1, d, ff), wmap, pipeline_mode=single),
                  pl.BlockSpec((1, 1, ff), wmap),
                  pl.BlockSpec((1, d, ff), wmap, pipeline_mode=single),
                  pl.BlockSpec((1, 1, ff), wmap),
                  pl.BlockSpec((1, ff, d), wmap, pipeline_mode=single),
                  pl.BlockSpec((1, 1, d), wmap)],
        out_specs=pl.BlockSpec((blk, d), lambda i, be, tok, na: (i, 0)),
        scratch_shapes=[pltpu.VMEM((2, blk, d), F32), pltpu.SemaphoreType.DMA((2,))],
    )
    return pl.pallas_call(
        functools.partial(_expert_kernel, blk=blk),
        grid_spec=grid_spec,
        out_shape=jax.ShapeDtypeStruct((n_blocks * blk, d), F32),
        compiler_params=_cparams(("arbitrary",)),
        name="expert_ffn",
    )(blk_expert, row_tok, n_act, x32, w_glu.astype(BF16), b_glu.astype(F32).reshape(n_e, 1, ff),
      w_lin.astype(BF16), b_lin.astype(F32).reshape(n_e, 1, ff),
      w_down.astype(BF16), b_down.astype(F32).reshape(n_e, 1, d))


def _combine_kernel(dest_ref, ys_hbm, x_ref, gate_ref, g_ref, b_ref, o32_ref, o16_ref, buf_ref, sem_ref, *, bt):
    i = pl.program_id(0)
    n = pl.num_programs(0)

    def row_copy(blk_idx, r, kk, dst_slot):
        row = dest_ref[(blk_idx * bt + r) * TOP_K + kk]
        return pltpu.make_async_copy(ys_hbm.at[pl.ds(row, 1), :], buf_ref.at[dst_slot, kk, pl.ds(r, 1), :],
                                     sem_ref.at[dst_slot])

    @pl.when(i == 0)
    def _():
        def body(r, c):
            for kk in range(TOP_K):
                row_copy(0, r, kk, 0).start()
            return c
        lax.fori_loop(0, bt, body, 0, unroll=2)

    slot = lax.rem(i, 2)
    pltpu.make_async_copy(buf_ref.at[slot], buf_ref.at[slot], sem_ref.at[slot]).wait()
    nxt = jnp.where(i + 1 < n, i + 1, 0)
    for r in range(bt):
        for kk in range(TOP_K):
            row_copy(nxt, r, kk, 1 - slot).start()
    gate = gate_ref[...]
    z = ALPHA * x_ref[...]
    for kk in range(TOP_K):
        z = z + gate[:, kk:kk + 1] * buf_ref[slot, kk]
    mu = jnp.mean(z, axis=-1, keepdims=True)
    zc = z - mu
    var = jnp.mean(zc * zc, axis=-1, keepdims=True)
    y = zc * lax.rsqrt(var + LN_EPS) * g_ref[...] + b_ref[...]
    o32_ref[...] = y
    o16_ref[...] = y.astype(BF16)

    @pl.when(i == n - 1)
    def _():
        pltpu.make_async_copy(buf_ref.at[1 - slot], buf_ref.at[1 - slot], sem_ref.at[1 - slot]).wait()


def _combine_ln(ys, dest, x32, gate, g, b, *, bt):
    t, d = x32.shape
    grid_spec = pltpu.PrefetchScalarGridSpec(
        num_scalar_prefetch=1,
        grid=(t // bt,),
        in_specs=[pl.BlockSpec(memory_space=pl.ANY),
                  pl.BlockSpec((bt, d), lambda i, dr: (i, 0)),
                  pl.BlockSpec((bt, LANES), lambda i, dr: (i, 0)),
                  pl.BlockSpec((1, d), lambda i, dr: (0, 0)),
                  pl.BlockSpec((1, d), lambda i, dr: (0, 0))],
        out_specs=[pl.BlockSpec((bt, d), lambda i, dr: (i, 0)),
                   pl.BlockSpec((bt, d), lambda i, dr: (i, 0))],
        scratch_shapes=[pltpu.VMEM((2, TOP_K, bt, d), F32), pltpu.SemaphoreType.DMA((2,))],
    )
    return pl.pallas_call(
        functools.partial(_combine_kernel, bt=bt),
        grid_spec=grid_spec,
        out_shape=[jax.ShapeDtypeStruct((t, d), F32), jax.ShapeDtypeStruct((t, d), BF16)],
        compiler_params=_cparams(("arbitrary",)),
        name="moe_combine_ln",
    )(dest, ys, x32, gate, g.astype(F32).reshape(1, d), b.astype(F32).reshape(1, d))


def _moe_block(x32, x16, w_router, b_router, w_glu, b_glu, w_lin, b_lin, w_down, b_down, ln_g, ln_b,
               *, blk, bt, router_bm):
    top_e, gate = _router(x16, w_router, b_router, bm=router_bm)
    row_tok, dest, blk_expert, n_act = _dispatch_plan(top_e, w_glu.shape[0], blk, router_bm)
    ys = _expert_ffn(x32, row_tok, blk_expert, n_act, w_glu, b_glu, w_lin, b_lin, w_down, b_down, blk=blk)
    return _combine_ln(ys, dest, x32, gate, ln_g, ln_b, bt=bt)


def kernel(x, w_in, nat_rpb, nat_norm_g, diff_lambda, diff_subln_g, t5_table, ml_conv_w, ml_gate_b, ml_norm_g,
           w_out, ln_mix_g, ln_mix_b, router_w, router_b, exp_w_glu, exp_b_glu, exp_w_lin, exp_b_lin,
           exp_w_down, exp_b_down, ln_ffn_g, ln_ffn_b):
    bsz, t, d = x.shape
    assert bsz == 1
    x32 = x.reshape(t, d)
    x16 = x32.astype(BF16)
    diff_bq, diff_bk = 1024, 1024
    bias_tiles, far = _diff_bias_tiles(t5_table, diff_bq, diff_bk)
    ml_col0 = (3 * NAT_WIDTH + 3 * DIFF_WIDTH) // MLSTM_WIDTH
    for l in range(DEPTH):
        w_gate = jnp.pad(w_in[l, :, MAIN_WIDTH:], ((0, 0), (0, LANES - 4 * MLSTM_HEADS))).astype(BF16)
        u = _inproj(x16, w_in, l, MAIN_WIDTH, bm=1024, bn=1024)
        gates = _matmul(x16, w_gate, bm=1024, bn=LANES, out_dtype=F32, name="inproj_gates")
        nat = _nat_attention(u, nat_rpb[l], nat_norm_g[l], blocks_per_step=16)
        lam_init = 0.8 - 0.6 * math.exp(-0.3 * l)
        dif = _diff_attention(u, bias_tiles, far, diff_lambda[l], diff_subln_g[l], lam_init,
                              bq=diff_bq, bk=diff_bk, rb=128, col0=3 * NAT_WIDTH // HEAD_DIM)
        ml = _mlstm_block(u, gates, ml_conv_w[l], ml_gate_b[l], ml_norm_g[l], lc=256, col0=ml_col0)
        x1_32, x1_16 = _proj_residual_ln([nat, dif, ml], w_out[l].astype(BF16), x32, ln_mix_g[l], ln_mix_b[l],
                                         bm=512, bn=512)
        x32, x16 = _moe_block(x1_32, x1_16, router_w[l], router_b[l], exp_w_glu[l], exp_b_glu[l], exp_w_lin[l],
                              exp_b_lin[l], exp_w_down[l], exp_b_down[l], ln_ffn_g[l], ln_ffn_b[l],
                              blk=256, bt=128, router_bm=512)
    return x32.reshape(bsz, t, d)
```

```python
import functools
import math

import numpy as np
import jax
import jax.numpy as jnp
from jax import lax
from jax.experimental import pallas as pl
from jax.experimental.pallas import tpu as pltpu

DEPTH = 2
HEAD_DIM = 128
NAT_HEADS = 16
DIFF_HEADS = 8
MLSTM_HEADS = 8
NAT_WIDTH = NAT_HEADS * HEAD_DIM
DIFF_WIDTH = DIFF_HEADS * HEAD_DIM
MLSTM_WIDTH = MLSTM_HEADS * HEAD_DIM
DIFF_QK_DIM = HEAD_DIM // 2
MAIN_WIDTH = 3 * NAT_WIDTH + 3 * DIFF_WIDTH + 4 * MLSTM_WIDTH
GRID_W = 64
NAT_WIN_ROWS = 8
NAT_WIN_COLS = 16
NAT_Q_ROWS = 2
NAT_KEY_ROWS = 10
REL_BUCKETS = 32
REL_MAX_DIST = 128
MLSTM_CHUNK = 64
MLSTM_CONV = 5
N_EXPERTS = 32
TOP_K = 4
SWIGLU_ALPHA = 1.702
SWIGLU_LIMIT = 7.0
MOE_BLOCK = 512
LN_EPS = 1e-5
ALPHA = (2.0 * DEPTH) ** 0.25
NEG_BIG = -1e30

LANES = 128
VMEM_LIMIT = 56 * 1024 * 1024

BF16 = jnp.bfloat16
F32 = jnp.float32


def _cparams(sem):
    return pltpu.CompilerParams(dimension_semantics=sem, vmem_limit_bytes=VMEM_LIMIT)


def _mm_kernel(x_ref, w_ref, o_ref):
    o_ref[...] = jnp.dot(x_ref[...], w_ref[...], preferred_element_type=F32).astype(o_ref.dtype)


def _matmul(x, w, *, bm, bn, out_dtype, name):
    m, k = x.shape
    n = w.shape[1]
    return pl.pallas_call(
        _mm_kernel,
        grid=(m // bm, n // bn),
        in_specs=[pl.BlockSpec((bm, k), lambda i, j: (i, 0)),
                  pl.BlockSpec((k, bn), lambda i, j: (0, j))],
        out_specs=pl.BlockSpec((bm, bn), lambda i, j: (i, j)),
        out_shape=jax.ShapeDtypeStruct((m, n), out_dtype),
        compiler_params=_cparams(("parallel", "parallel")),
        name=name,
    )(x, w)


def _proj_ln_kernel(*refs, n_parts, widths, bn, nj):
    part_refs = refs[:n_parts]
    w_ref, x_ref, g_ref, b_ref, o32_ref, o16_ref, mix_ref = refs[n_parts:]
    j = pl.program_id(1)

    @pl.when(j == 0)
    def _():
        off = 0
        for p_ref, wd in zip(part_refs, widths):
            mix_ref[:, off:off + wd] = p_ref[...]
            off += wd

    z = ALPHA * x_ref[...] + jnp.dot(mix_ref[...], w_ref[...], preferred_element_type=F32)
    for jj in range(nj):
        @pl.when(j == jj)
        def _(jj=jj):
            o32_ref[:, jj * bn:(jj + 1) * bn] = z

    @pl.when(j == nj - 1)
    def _():
        zz = o32_ref[...]
        mu = jnp.mean(zz, axis=-1, keepdims=True)
        zc = zz - mu
        var = jnp.mean(zc * zc, axis=-1, keepdims=True)
        y = zc * lax.rsqrt(var + LN_EPS) * g_ref[...] + b_ref[...]
        o32_ref[...] = y
        o16_ref[...] = y.astype(BF16)


def _proj_residual_ln(parts, w, x_res, g, b, *, bm, bn):
    t, d = x_res.shape
    widths = tuple(p.shape[1] for p in parts)
    kdim = sum(widths)
    nj = d // bn
    kern = functools.partial(_proj_ln_kernel, n_parts=len(parts), widths=widths, bn=bn, nj=nj)
    in_specs = [pl.BlockSpec((bm, wd), lambda i, j: (i, 0)) for wd in widths]
    in_specs += [pl.BlockSpec((kdim, bn), lambda i, j: (0, j)),
                 pl.BlockSpec((bm, bn), lambda i, j: (i, j)),
                 pl.BlockSpec((1, d), lambda i, j: (0, 0)),
                 pl.BlockSpec((1, d), lambda i, j: (0, 0))]
    return pl.pallas_call(
        kern,
        grid=(t // bm, nj),
        in_specs=in_specs,
        out_specs=[pl.BlockSpec((bm, d), lambda i, j: (i, 0)),
                   pl.BlockSpec((bm, d), lambda i, j: (i, 0))],
        out_shape=[jax.ShapeDtypeStruct((t, d), F32), jax.ShapeDtypeStruct((t, d), BF16)],
        scratch_shapes=[pltpu.VMEM((bm, kdim), BF16)],
        compiler_params=_cparams(("parallel", "arbitrary")),
        name="proj_ln",
    )(*parts, w, x_res, g.reshape(1, d), b.reshape(1, d))


def _nat_geometry(t):
    rows = t // GRID_W
    win_r = min(NAT_WIN_ROWS, rows)
    kr = NAT_KEY_ROWS
    assert rows >= kr and rows % NAT_Q_ROWS == 0
    n_blk = rows // NAT_Q_ROWS
    col = np.arange(GRID_W)
    c0 = np.clip(col - NAT_WIN_COLS // 2, 0, GRID_W - NAT_WIN_COLS)
    keys, ks_list, ty_list, tables = {}, [], [], []
    for i in range(n_blk):
        qr = i * NAT_Q_ROWS + np.arange(NAT_Q_ROWS)
        r0 = np.clip(qr - win_r // 2, 0, rows - win_r)
        ks = int(np.clip(r0[0], 0, rows - kr))
        key = (tuple(qr - ks), tuple(r0 - ks))
        if key not in keys:
            keys[key] = len(tables)
            q_row = np.repeat(qr, GRID_W)
            q_r0 = np.repeat(r0, GRID_W)
            q_col = np.tile(col, NAT_Q_ROWS)
            q_c0 = np.tile(c0, NAT_Q_ROWS)
            k_row = np.repeat(ks + np.arange(kr), GRID_W)
            k_col = np.tile(col, kr)
            row_ok = (k_row[None, :] >= q_r0[:, None]) & (k_row[None, :] < q_r0[:, None] + win_r)
            col_ok = (k_col[None, :] >= q_c0[:, None]) & (k_col[None, :] < q_c0[:, None] + NAT_WIN_COLS)
            dr = np.clip(k_row[None, :] - q_row[:, None] + NAT_WIN_ROWS - 1, 0, 2 * NAT_WIN_ROWS - 2)
            dc = np.clip(k_col[None, :] - q_col[:, None] + NAT_WIN_COLS - 1, 0, 2 * NAT_WIN_COLS - 2)
            tables.append((dr, dc, row_ok & col_ok))
        ks_list.append(ks)
        ty_list.append(keys[key])
    dr = np.stack([tb[0] for tb in tables]).astype(np.int32)
    dc = np.stack([tb[1] for tb in tables]).astype(np.int32)
    ok = np.stack([tb[2] for tb in tables])
    return np.asarray(ks_list, np.int32), np.asarray(ty_list, np.int32), dr, dc, ok


def _nat_bias_kernel(rp_ref, ok_ref, o_ref, *, dr_small):
    lane = lax.broadcasted_iota(jnp.int32, (GRID_W, LANES), 1)
    for ty, dr_ty in enumerate(dr_small):
        for qrl, dr_row in enumerate(dr_ty):
            for pair in range(len(dr_row) // 2):
                da, db = int(dr_row[2 * pair]), int(dr_row[2 * pair + 1])
                ya = _toeplitz_rows(rp_ref[0, da:da + 1, :], GRID_W, GRID_W - 1)
                yb = _toeplitz_rows(rp_ref[0, db:db + 1, :], GRID_W, LANES - 1)
                rows = slice(qrl * GRID_W, (qrl + 1) * GRID_W)
                cols = slice(pair * LANES, (pair + 1) * LANES)
                o_ref[ty, 0, rows, cols] = jnp.where(ok_ref[ty, rows, cols] > 0.5,
                                                     jnp.where(lane < GRID_W, ya, yb), NEG_BIG)


def _nat_bias_table(rpb, dr_small, ok):
    n_type, nq, kw = ok.shape
    assert 2 * GRID_W == LANES and NAT_KEY_ROWS % 2 == 0
    pad_c = GRID_W - NAT_WIN_COLS
    n_dr = rpb.shape[1]
    rp = jnp.pad(rpb.astype(F32), ((0, 0), (0, -n_dr % 8), (pad_c, LANES - rpb.shape[2] - pad_c)))
    return pl.pallas_call(
        functools.partial(_nat_bias_kernel, dr_small=dr_small.tolist()),
        grid=(NAT_HEADS,),
        in_specs=[pl.BlockSpec((1, rp.shape[1], LANES), lambda h: (h, 0, 0)),
                  pl.BlockSpec((n_type, nq, kw), lambda h: (0, 0, 0))],
        out_specs=pl.BlockSpec((n_type, 1, nq, kw), lambda h: (0, h, 0, 0)),
        out_shape=jax.ShapeDtypeStruct((n_type, NAT_HEADS, nq, kw), F32),
        compiler_params=_cparams(("parallel",)),
        name="nat_bias_table",
    )(rp, jnp.asarray(ok.astype(np.float32)))


def _nat_kernel(ks_ref, ty_ref, q_ref, k_ref, v_ref, bias_ref, g_ref, o_ref, *, blocks_per_step, kw):
    nq = NAT_Q_ROWS * GRID_W
    base = pl.program_id(1) * blocks_per_step
    scale = HEAD_DIM ** -0.5
    blocks = range(blocks_per_step)
    k0s = [pl.multiple_of(ks_ref[base + jj] * GRID_W, GRID_W) for jj in blocks]
    ss = [lax.dot_general(q_ref[jj * nq:(jj + 1) * nq, :], k_ref[pl.ds(k0s[jj], kw), :],
                          (((1,), (1,)), ((), ())), preferred_element_type=F32) * scale
          + bias_ref[ty_ref[base + jj], 0] for jj in blocks]
    ps = [jnp.exp(s - jnp.max(s, axis=-1, keepdims=True)) for s in ss]
    ls = [jnp.sum(p, axis=-1, keepdims=True) for p in ps]
    os = [jnp.dot(ps[jj].astype(BF16), v_ref[pl.ds(k0s[jj], kw), :], preferred_element_type=F32) / ls[jj]
          for jj in blocks]
    for jj in blocks:
        o = os[jj]
        ms = jnp.mean(o * o, axis=-1, keepdims=True)
        o_ref[jj * nq:(jj + 1) * nq, :] = (o * lax.rsqrt(ms + LN_EPS) * g_ref[0]).astype(o_ref.dtype)


def _nat_attention(u, rpb, norm_g, *, blocks_per_step):
    t = u.shape[0]
    ks, ty, dr, dc, ok = _nat_geometry(t)
    n_blk = ks.shape[0]
    kw = NAT_KEY_ROWS * GRID_W
    nq = NAT_Q_ROWS * GRID_W
    n_type = ok.shape[0]
    bias = _nat_bias_table(rpb, dr[:, ::GRID_W, ::GRID_W], ok)
    g = norm_g.astype(F32).reshape(NAT_HEADS, 1, HEAD_DIM)
    rows_per_step = blocks_per_step * nq
    kern = functools.partial(_nat_kernel, blocks_per_step=blocks_per_step, kw=kw)
    grid_spec = pltpu.PrefetchScalarGridSpec(
        num_scalar_prefetch=2,
        grid=(NAT_HEADS, n_blk // blocks_per_step),
        in_specs=[pl.BlockSpec((rows_per_step, HEAD_DIM), lambda h, g_, *_: (g_, h)),
                  pl.BlockSpec((t, HEAD_DIM), lambda h, g_, *_: (0, NAT_HEADS + h)),
                  pl.BlockSpec((t, HEAD_DIM), lambda h, g_, *_: (0, 2 * NAT_HEADS + h)),
                  pl.BlockSpec((n_type, 1, nq, kw), lambda h, g_, *_: (0, h, 0, 0)),
                  pl.BlockSpec((1, 1, HEAD_DIM), lambda h, g_, *_: (h, 0, 0))],
        out_specs=pl.BlockSpec((rows_per_step, HEAD_DIM), lambda h, g_, *_: (g_, h)),
    )
    return pl.pallas_call(
        kern,
        grid_spec=grid_spec,
        out_shape=jax.ShapeDtypeStruct((t, NAT_WIDTH), BF16),
        compiler_params=_cparams(("parallel", "arbitrary")),
        name="nat_attn",
    )(jnp.asarray(ks), jnp.asarray(ty), u, u, u, bias, g)


def _t5_bucket(rel):
    half = REL_BUCKETS // 2
    max_exact = half // 2
    ret = jnp.where(rel > 0, half, 0)
    n = jnp.abs(rel)
    n_f = jnp.maximum(n, 1).astype(jnp.float32)
    large = max_exact + (jnp.log(n_f / max_exact) / math.log(REL_MAX_DIST / max_exact)
                         * (half - max_exact)).astype(jnp.int32)
    large = jnp.minimum(large, half - 1)
    return ret + jnp.where(n < max_exact, n, large)


def _diff_kernel(far_ref, q_ref, k_ref, v_ref, bias_ref, lp_ref, g_ref, o_ref,
                 qs_ref, m_ref, acc_ref, *, bk, ratio, nk, lam_init, rb):
    h = pl.program_id(0)
    qi = pl.program_id(1)
    bq = q_ref.shape[0]
    n_rb = 2 * bq // rb

    q = q_ref[...]
    lane = lax.broadcasted_iota(jnp.int32, q.shape, 1)
    qsc = q * jnp.asarray(DIFF_QK_DIM ** -0.5, q.dtype)
    zero = jnp.zeros_like(qsc)
    qs_ref[0:bq, :] = jnp.where(lane < DIFF_QK_DIM, qsc, zero)
    qs_ref[bq:2 * bq, :] = jnp.where(lane >= DIFF_QK_DIM, qsc, zero)
    m_ref[...] = jnp.full(m_ref.shape, -jnp.inf, F32)
    acc_ref[...] = jnp.zeros(acc_ref.shape, F32)

    def step(ki, near_idx, const):
        k0 = pl.multiple_of(ki * bk, bk)
        k = k_ref[pl.ds(k0, bk), :]
        v = v_ref[pl.ds(k0, bk), :]
        v_ext = jnp.concatenate([v, jnp.ones_like(v)], axis=1)
        for r in range(n_rb):
            rows = slice(r * rb, (r + 1) * rb)
            s = lax.dot_general(qs_ref[rows, :], k, (((1,), (1,)), ((), ())), preferred_element_type=F32)
            if near_idx is not None:
                r0 = (r * rb) % bq
                s = s + bias_ref[0, near_idx, r0:r0 + rb, :]
            mt = jnp.max(s, axis=-1, keepdims=True)
            if const is not None:
                mt = mt + const
            m_prev = m_ref[rows, :]
            m_new = jnp.maximum(m_prev, mt)
            a = jnp.exp(m_prev - m_new)
            shift = m_new if const is None else m_new - const
            p = jnp.exp(s - jnp.tile(shift, (1, bk // LANES)))
            acc_ref[rows, :] = (jnp.tile(a, (1, 2)) * acc_ref[rows, :]
                                + jnp.dot(p.astype(BF16), v_ext, preferred_element_type=F32))
            m_ref[rows, :] = m_new

    c_left = far_ref[h, 0]
    c_right = far_ref[h, 1]
    first_near = qi * ratio - 1
    n_near = ratio + 2

    def left_body(ki, carry):
        step(ki, None, c_left)
        return carry

    lax.fori_loop(0, jnp.clip(first_near, 0, nk), left_body, 0)
    for tt in range(n_near):
        ki = first_near + tt

        @pl.when((ki >= 0) & (ki < nk))
        def _(ki=ki, tt=tt):
            step(ki, tt, None)

    def right_body(ki, carry):
        step(ki, None, c_right)
        return carry

    lax.fori_loop(jnp.clip(first_near + n_near, 0, nk), nk, right_body, 0)

    acc = acc_ref[...]
    o = acc[:, 0:HEAD_DIM] / acc[:, HEAD_DIM:2 * HEAD_DIM]
    lp = lp_ref[...]
    lam = (jnp.exp(jnp.sum(lp[0:1] * lp[1:2], axis=-1, keepdims=True))
           - jnp.exp(jnp.sum(lp[2:3] * lp[3:4], axis=-1, keepdims=True)) + lam_init)
    d = o[0:bq] - lam * o[bq:2 * bq]
    ms = jnp.mean(d * d, axis=-1, keepdims=True)
    o_ref[...] = (d * lax.rsqrt(ms + LN_EPS) * g_ref[...] * (1.0 - lam_init)).astype(o_ref.dtype)


def _diff_bias_tiles(t5_table, bq, bk):
    ratio = bq // bk
    span = bq + bk + 1
    rel = jnp.arange(-span, span + 1, dtype=jnp.int32)
    vec = jnp.take(t5_table.astype(F32), _t5_bucket(rel), axis=0)
    vec_t = vec.T
    diags = jnp.stack([vec_t[:, (tt - 1) * bk + span - (bq - 1):(tt - 1) * bk + span + bk]
                       for tt in range(ratio + 2)], axis=1)
    tiles = _toeplitz(diags, bq, bk)
    far = jnp.stack([vec[0], vec[-1]], axis=1)
    return tiles, far


def _toeplitz_rows(w_row, n_rows, origin):
    width = w_row.shape[1]
    x = jnp.broadcast_to(w_row, (n_rows, width))
    return pltpu.roll(x, (width - origin) % width, 1, stride=1, stride_axis=0)


def _toeplitz_kernel(d_ref, o_ref):
    bq, bk = o_ref.shape[2], o_ref.shape[3]
    o_ref[0, 0] = _toeplitz_rows(d_ref[0, 0], bq, bq - 1)[:, :bk]


def _toeplitz(w, n_rows, n_cols):
    na, nb, ln = w.shape
    width = -(-ln // LANES) * LANES
    wp = jnp.pad(w, ((0, 0), (0, 0), (0, width - ln))).reshape(na, nb, 1, width)
    return pl.pallas_call(
        _toeplitz_kernel,
        grid=(na, nb),
        in_specs=[pl.BlockSpec((1, 1, 1, width), lambda a, b: (a, b, 0, 0))],
        out_specs=pl.BlockSpec((1, 1, n_rows, n_cols), lambda a, b: (a, b, 0, 0)),
        out_shape=jax.ShapeDtypeStruct((na, nb, n_rows, n_cols), F32),
        compiler_params=_cparams(("parallel", "parallel")),
        name="toeplitz_tiles",
    )(wp)


def _diff_attention(u, bias_tiles, far, lam_params, subln_g, lam_init, *, bq, bk, rb, col0):
    t = u.shape[0]
    assert bk >= REL_MAX_DIST and bq % bk == 0
    ratio = bq // bk
    nk = t // bk
    n_near = ratio + 2
    assert bq % rb == 0
    kern = functools.partial(_diff_kernel, bk=bk, ratio=ratio, nk=nk, lam_init=lam_init, rb=rb)
    grid_spec = pltpu.PrefetchScalarGridSpec(
        num_scalar_prefetch=1,
        grid=(DIFF_HEADS, t // bq),
        in_specs=[pl.BlockSpec((bq, HEAD_DIM), lambda h, i, *_: (i, col0 + h)),
                  pl.BlockSpec((t, HEAD_DIM), lambda h, i, *_: (0, col0 + DIFF_HEADS + h)),
                  pl.BlockSpec((t, HEAD_DIM), lambda h, i, *_: (0, col0 + 2 * DIFF_HEADS + h)),
                  pl.BlockSpec((1, n_near, bq, bk), lambda h, i, *_: (h, 0, 0, 0)),
                  pl.BlockSpec((4, DIFF_QK_DIM), lambda h, i, *_: (0, 0)),
                  pl.BlockSpec((1, HEAD_DIM), lambda h, i, *_: (0, 0))],
        out_specs=pl.BlockSpec((bq, HEAD_DIM), lambda h, i, *_: (i, h)),
        scratch_shapes=[pltpu.VMEM((2 * bq, HEAD_DIM), BF16),
                        pltpu.VMEM((2 * bq, LANES), F32),
                        pltpu.VMEM((2 * bq, 2 * HEAD_DIM), F32)],
    )
    return pl.pallas_call(
        kern,
        grid_spec=grid_spec,
        out_shape=jax.ShapeDtypeStruct((t, DIFF_WIDTH), BF16),
        compiler_params=_cparams(("parallel", "arbitrary")),
        name="diff_attn",
    )(far, u, u, u, bias_tiles, lam_params.astype(F32), subln_g.astype(F32).reshape(1, HEAD_DIM))


CONV_HALO = 16


def _conv_silu_kernel(prev_ref, cur_ref, next_ref, w_ref, o_ref, *, k_scale):
    i = pl.program_id(0)
    j = pl.program_id(1)
    bt = cur_ref.shape[0]
    x = cur_ref[...].astype(F32)
    p = jnp.where(i > 0, prev_ref[...].astype(F32), 0.0)
    nx = jnp.where(i < pl.num_programs(0) - 1, next_ref[...].astype(F32), 0.0)
    ext = jnp.concatenate([p, x, nx], axis=0)
    n_ext = bt + 2 * CONV_HALO
    half = MLSTM_CONV // 2
    acc = w_ref[half:half + 1, :] * x
    for tap in range(MLSTM_CONV):
        s = tap - half
        if s == 0:
            continue
        rolled = pltpu.roll(ext, (n_ext - s) % n_ext, 0)
        acc = acc + w_ref[tap:tap + 1, :] * rolled[CONV_HALO:CONV_HALO + bt]
    y = acc * jax.nn.sigmoid(acc)
    o_ref[...] = (y * jnp.where(j == 1, k_scale, 1.0)).astype(o_ref.dtype)


def _conv_silu(u, conv_w, *, bt, col0):
    t = u.shape[0]
    w = MLSTM_WIDTH
    nb = t // bt
    hb = bt // CONV_HALO
    last = t // CONV_HALO - 1
    return pl.pallas_call(
        functools.partial(_conv_silu_kernel, k_scale=HEAD_DIM ** -0.5),
        grid=(nb, 2),
        in_specs=[pl.BlockSpec((CONV_HALO, w), lambda i, j: (jnp.maximum(i * hb - 1, 0), col0 + j)),
                  pl.BlockSpec((bt, w), lambda i, j: (i, col0 + j)),
                  pl.BlockSpec((CONV_HALO, w), lambda i, j: (jnp.minimum((i + 1) * hb, last), col0 + j)),
                  pl.BlockSpec((MLSTM_CONV, w), lambda i, j: (0, j))],
        out_specs=pl.BlockSpec((bt, w), lambda i, j: (i, j)),
        out_shape=jax.ShapeDtypeStruct((t, 2 * w), BF16),
        compiler_params=_cparams(("parallel", "parallel")),
        name="mlstm_conv",
    )(u, u, u, conv_w.astype(F32))


def _split3(x):
    x1 = x.astype(BF16)
    r1 = x - x1.astype(F32)
    x2 = r1.astype(BF16)
    x3 = (r1 - x2.astype(F32)).astype(BF16)
    return x1, x2, x3


def _log_sigmoid(x):
    return jnp.minimum(x, 0.0) - jnp.log(1.0 + jnp.exp(-jnp.abs(x)))


def _mlstm_units(units, c_ref, n_ref, m_ref, lc):
    nt_dims = (((1,), (1,)), ((), ()))
    tn_dims = (((0,), (0,)), ((), ()))
    rd = lambda u, name: u[name][:, u["sl"]]
    dmats = [jnp.where(u["mask"], u["bcol"] - u["brow"] + u["igrow"], -jnp.inf) for u in units]
    m_prevs = [m_ref[u["idx"]] for u in units]
    inters = [u["bcol"] + mp for u, mp in zip(units, m_prevs)]
    m_ts = [jnp.maximum(it, jnp.max(d, axis=-1, keepdims=True)) for it, d in zip(inters, dmats)]
    w_inters = [jnp.exp(it - mt) for it, mt in zip(inters, m_ts)]
    ss = [lax.dot_general(rd(u, "q_ref"), rd(u, "k_ref"), nt_dims, preferred_element_type=F32) for u in units]
    ps = [jnp.exp(d - mt) * s for d, mt, s in zip(dmats, m_ts, ss)]
    nums = [wi * jnp.dot(rd(u, "q_ref"), c_ref[u["idx"]].astype(BF16), preferred_element_type=F32)
            + jnp.dot(p.astype(BF16), rd(u, "v_ref"), preferred_element_type=F32)
            for u, wi, p in zip(units, w_inters, ps)]
    dens = [wi * jnp.sum(rd(u, "q_ref").astype(F32) * n_ref[u["idx"]], axis=-1, keepdims=True)
            + jnp.sum(p, axis=-1, keepdims=True) for u, wi, p in zip(units, w_inters, ps)]
    for u, num, den, mt in zip(units, nums, dens, m_ts):
        u["h_ref"][:, u["sl"]] = num / jnp.maximum(jnp.abs(den), jnp.exp(-mt))
    gs = [u["bcol"][lc - 1:lc] if u["fwd"] else u["bcol"][0:1] for u in units]
    a_s = [g - u["bcol"] + u["igcol"] for u, g in zip(units, gs)]
    m_locs = [jnp.max(a, axis=0, keepdims=True) for a in a_s]
    kws = [rd(u, "k_ref").astype(F32) * jnp.exp(a - ml) for u, a, ml in zip(units, a_s, m_locs)]
    c_locs = [lax.dot_general(kw.astype(BF16), rd(u, "v_ref"), tn_dims, preferred_element_type=F32)
              for u, kw in zip(units, kws)]
    for u, g, mp, ml, kw, c_loc in zip(units, gs, m_prevs, m_locs, kws, c_locs):
        m_new = jnp.maximum(g + mp, ml)
        s_old = jnp.exp(g + mp - m_new)
        s_loc = jnp.exp(ml - m_new)
        c_ref[u["idx"]] = s_old * c_ref[u["idx"]] + s_loc * c_loc
        n_ref[u["idx"]] = s_old * n_ref[u["idx"]] + s_loc * jnp.sum(kw, axis=0, keepdims=True)
        m_ref[u["idx"]] = m_new


def _mlstm_kernel(qf_ref, kf_ref, vf_ref, gf_ref, qb_ref, kb_ref, vb_ref, gb_ref, gbias_ref,
                  hf_ref, hb_ref, c_ref, n_ref, m_ref, *, lc):
    nh = MLSTM_HEADS

    @pl.when(pl.program_id(0) == 0)
    def _():
        c_ref[...] = jnp.zeros(c_ref.shape, F32)
        n_ref[...] = jnp.zeros(n_ref.shape, F32)
        m_ref[...] = jnp.zeros(m_ref.shape, F32)

    t_io = lax.broadcasted_iota(jnp.int32, (lc, lc), 0)
    s_io = lax.broadcasted_iota(jnp.int32, (lc, lc), 1)
    units = []
    for fwd in (True, False):
        g_ref, q_ref, k_ref, v_ref, h_ref = ((gf_ref, qf_ref, kf_ref, vf_ref, hf_ref) if fwd
                                             else (gb_ref, qb_ref, kb_ref, vb_ref, hb_ref))
        gates = g_ref[...] + gbias_ref[...]
        mask = (s_io <= t_io) if fwd else (s_io >= t_io)
        tri = jnp.where(mask, 1.0, 0.0).astype(BF16)
        cum = None
        for piece in _split3(_log_sigmoid(gates)):
            part = jnp.dot(tri, piece, preferred_element_type=F32)
            cum = part if cum is None else cum + part
        gates_t = jnp.transpose(gates)
        cum_t = jnp.transpose(cum)
        lane0 = 0 if fwd else 2 * nh
        for hh in range(nh):
            li, lb = lane0 + hh, lane0 + nh + hh
            units.append(dict(fwd=fwd, idx=(0 if fwd else nh) + hh, sl=slice(hh * HEAD_DIM, (hh + 1) * HEAD_DIM),
                              q_ref=q_ref, k_ref=k_ref, v_ref=v_ref, h_ref=h_ref, mask=mask,
                              bcol=cum[:, lb:lb + 1], igcol=gates[:, li:li + 1],
                              brow=cum_t[lb:lb + 1, :], igrow=gates_t[li:li + 1, :]))
    _mlstm_units(units, c_ref, n_ref, m_ref, lc)


def _mlstm_scan(qk, u, gates, gate_b, *, lc, v_col):
    t = qk.shape[0]
    w = MLSTM_WIDTH
    nc = t // lc
    gbias = jnp.pad(gate_b.astype(F32), (0, LANES - gate_b.shape[0])).reshape(1, LANES)
    fmap = lambda col: (lambda c: (c, col))
    bmap = lambda col: (lambda c: (nc - 1 - c, col))
    nhd = 2 * MLSTM_HEADS
    return pl.pallas_call(
        functools.partial(_mlstm_kernel, lc=lc),
        grid=(nc,),
        in_specs=[pl.BlockSpec((lc, w), fmap(0)), pl.BlockSpec((lc, w), fmap(1)),
                  pl.BlockSpec((lc, w), fmap(v_col)), pl.BlockSpec((lc, LANES), fmap(0)),
                  pl.BlockSpec((lc, w), bmap(0)), pl.BlockSpec((lc, w), bmap(1)),
                  pl.BlockSpec((lc, w), bmap(v_col)), pl.BlockSpec((lc, LANES), bmap(0)),
                  pl.BlockSpec((1, LANES), lambda c: (0, 0))],
        out_specs=[pl.BlockSpec((lc, w), fmap(0)), pl.BlockSpec((lc, w), bmap(0))],
        out_shape=[jax.ShapeDtypeStruct((t, w), F32), jax.ShapeDtypeStruct((t, w), F32)],
        scratch_shapes=[pltpu.VMEM((nhd, HEAD_DIM, HEAD_DIM), F32),
                        pltpu.VMEM((nhd, 1, HEAD_DIM), F32),
                        pltpu.VMEM((nhd, 1, 1), F32)],
        compiler_params=_cparams(("arbitrary",)),
        name="mlstm_scan",
    )(qk, qk, u, gates, qk, qk, u, gates, gbias)


def _mlstm_out_kernel(hf_ref, hb_ref, o_ref, g_ref, out_ref):
    for hh in range(MLSTM_HEADS):
        sl = slice(hh * HEAD_DIM, (hh + 1) * HEAD_DIM)
        x = (hf_ref[:, sl] + hb_ref[:, sl]) * jax.nn.sigmoid(o_ref[:, sl].astype(F32))
        xc = x - jnp.mean(x, axis=-1, keepdims=True)
        y = xc * lax.rsqrt(jnp.mean(xc * xc, axis=-1, keepdims=True) + LN_EPS) * g_ref[:, sl]
        out_ref[:, sl] = y.astype(out_ref.dtype)


def _mlstm_out(hf, hb, u, norm_g, *, bt, o_col):
    t, w = hf.shape
    return pl.pallas_call(
        _mlstm_out_kernel,
        grid=(t // bt,),
        in_specs=[pl.BlockSpec((bt, w), lambda i: (i, 0)), pl.BlockSpec((bt, w), lambda i: (i, 0)),
                  pl.BlockSpec((bt, w), lambda i: (i, o_col)), pl.BlockSpec((1, w), lambda i: (0, 0))],
        out_specs=pl.BlockSpec((bt, w), lambda i: (i, 0)),
        out_shape=jax.ShapeDtypeStruct((t, w), BF16),
        compiler_params=_cparams(("parallel",)),
        name="mlstm_out",
    )(hf, hb, u, norm_g.astype(F32).reshape(1, w))


def _mlstm_block(u, gates, conv_w, gate_b, norm_g, *, lc, col0):
    qk = _conv_silu(u, conv_w, bt=512, col0=col0)
    hf, hb = _mlstm_scan(qk, u, gates, gate_b, lc=lc, v_col=col0 + 2)
    return _mlstm_out(hf, hb, u, norm_g, bt=512, o_col=col0 + 3)


def _router_kernel(x_ref, w_ref, b_ref, e_ref, g_ref):
    logits = jnp.dot(x_ref[...], w_ref[...], preferred_element_type=F32) + b_ref[...]
    lane = lax.broadcasted_iota(jnp.int32, logits.shape, 1)
    work = logits
    e_out = jnp.zeros(logits.shape, jnp.int32)
    vals = []
    for kk in range(TOP_K):
        m = jnp.max(work, axis=-1, keepdims=True)
        idx = jnp.min(jnp.where(work == m, lane, LANES), axis=-1, keepdims=True)
        e_out = jnp.where(lane == kk, idx, e_out)
        vals.append(m)
        work = jnp.where(lane == idx, -jnp.inf, work)
    ex = [jnp.exp(v - vals[0]) for v in vals]
    den = ex[0]
    for e in ex[1:]:
        den = den + e
    g_out = jnp.zeros(logits.shape, F32)
    for kk in range(TOP_K):
        g_out = jnp.where(lane == kk, ex[kk] / den, g_out)
    e_ref[...] = e_out
    g_ref[...] = g_out


def _router(x16, w_router, b_router, *, bm):
    t, d = x16.shape
    n_e = w_router.shape[1]
    w = jnp.pad(w_router, ((0, 0), (0, LANES - n_e))).astype(BF16)
    b = jnp.pad(b_router.astype(F32), (0, LANES - n_e), constant_values=NEG_BIG).reshape(1, LANES)
    return pl.pallas_call(
        _router_kernel,
        grid=(t // bm,),
        in_specs=[pl.BlockSpec((bm, d), lambda i: (i, 0)),
                  pl.BlockSpec((d, LANES), lambda i: (0, 0)),
                  pl.BlockSpec((1, LANES), lambda i: (0, 0))],
        out_specs=[pl.BlockSpec((bm, LANES), lambda i: (i, 0)),
                   pl.BlockSpec((bm, LANES), lambda i: (i, 0))],
        out_shape=[jax.ShapeDtypeStruct((t, LANES), jnp.int32), jax.ShapeDtypeStruct((t, LANES), F32)],
        compiler_params=_cparams(("parallel",)),
        name="router",
    )(x16, w, b)


def _rank_kernel(e_ref, rank_ref, cnt_ref, base_ref):
    @pl.when(pl.program_id(0) == 0)
    def _():
        base_ref[...] = jnp.zeros(base_ref.shape, F32)

    e = e_ref[...]
    bt = e.shape[0]
    lane = lax.broadcasted_iota(jnp.int32, e.shape, 1)
    onehots = [jnp.where(lane == e[:, kk:kk + 1], 1.0, 0.0) for kk in range(TOP_K)]
    tot = onehots[0]
    for oh in onehots[1:]:
        tot = tot + oh
    t_io = lax.broadcasted_iota(jnp.int32, (bt, bt), 0)
    s_io = lax.broadcasted_iota(jnp.int32, (bt, bt), 1)
    earlier = jnp.where(s_io < t_io, 1.0, 0.0).astype(BF16)
    pos = jnp.dot(earlier, tot.astype(BF16), preferred_element_type=F32) + base_ref[...]
    out = jnp.zeros(e.shape, jnp.int32)
    for kk in range(TOP_K):
        rk = jnp.sum(onehots[kk] * pos, axis=-1, keepdims=True)
        out = jnp.where(lane == kk, rk.astype(jnp.int32), out)
    rank_ref[...] = out
    base_ref[...] = base_ref[...] + jnp.sum(tot, axis=0, keepdims=True)
    cnt_ref[...] = base_ref[...]


def _slot_ranks(top_e, *, bt):
    t = top_e.shape[0]
    return pl.pallas_call(
        _rank_kernel,
        grid=(t // bt,),
        in_specs=[pl.BlockSpec((bt, LANES), lambda i: (i, 0))],
        out_specs=[pl.BlockSpec((bt, LANES), lambda i: (i, 0)), pl.BlockSpec((1, LANES), lambda i: (0, 0))],
        out_shape=[jax.ShapeDtypeStruct((t, LANES), jnp.int32), jax.ShapeDtypeStruct((1, LANES), F32)],
        scratch_shapes=[pltpu.VMEM((1, LANES), F32)],
        compiler_params=_cparams(("arbitrary",)),
        name="moe_rank",
    )(top_e)


def _dispatch_plan(top_e_lanes, n_experts, blk, rank_bt):
    t = top_e_lanes.shape[0]
    n_slots = t * TOP_K
    n_blocks = -(-n_slots // blk) + n_experts
    rank_lanes, counts_f = _slot_ranks(top_e_lanes, bt=rank_bt)
    slot_e = top_e_lanes[:, :TOP_K].reshape(-1)
    rank = rank_lanes[:, :TOP_K].reshape(-1)
    counts = counts_f[0, :n_experts].astype(jnp.int32)
    onehot = (slot_e[:, None] == jnp.arange(n_experts, dtype=jnp.int32)[None, :]).astype(jnp.int32)
    padded = (counts + blk - 1) // blk * blk
    pad_end = jnp.cumsum(padded)
    pad_start = pad_end - padded
    dest = (jnp.sum(pad_start[None, :] * onehot, axis=1) + rank).astype(jnp.int32)
    slot_tok = jnp.arange(n_slots, dtype=jnp.int32) // TOP_K
    row_tok = jnp.zeros((n_blocks * blk,), jnp.int32).at[dest].set(slot_tok)
    n_act = (pad_end[-1] // blk).astype(jnp.int32)
    blk_id = jnp.minimum(jnp.arange(n_blocks, dtype=jnp.int32), n_act - 1)
    blk_expert = jnp.minimum(jnp.sum((pad_end[None, :] <= (blk_id * blk)[:, None]).astype(jnp.int32), axis=1),
                             n_experts - 1)
    return row_tok, dest, blk_expert, n_act.reshape(1)


def _expert_kernel(be_ref, tok_ref, nact_ref, x_hbm, wg_hbm, bg_ref, wl_hbm, bl_ref, wd_hbm, bd_ref, o_ref,
                   xg_ref, wg_buf, wl_buf, wd_buf, sem_ref, wsem_ref, *, blk, layer):
    b = pl.program_id(0)
    n_blocks = pl.num_programs(0)
    n_act = nact_ref[0]
    slot = lax.rem(b, 2)
    e_cur = be_ref[b]
    e_prev = be_ref[jnp.maximum(b - 1, 0)]
    e_next = be_ref[jnp.minimum(b + 1, n_blocks - 1)]
    fresh = (b == 0) | (e_cur != e_prev)

    def row_copy(blk_idx, r, dst_slot):
        tok = tok_ref[blk_idx * blk + r]
        return pltpu.make_async_copy(x_hbm.at[pl.ds(tok, 1), :], xg_ref.at[dst_slot, pl.ds(r, 1), :],
                                     sem_ref.at[dst_slot])

    def w_copy(which, e):
        src, dst = ((wg_hbm, wg_buf), (wl_hbm, wl_buf), (wd_hbm, wd_buf))[which]
        return pltpu.make_async_copy(src.at[layer, e], dst, wsem_ref.at[which])

    def compute(has_next):
        pltpu.make_async_copy(xg_ref.at[slot], xg_ref.at[slot], sem_ref.at[slot]).wait()

        @pl.when(fresh)
        def _():
            w_copy(0, e_cur).wait()
            w_copy(1, e_cur).wait()

        if has_next:
            for r in range(blk):
                row_copy(b + 1, r, 1 - slot).start()
        xb = xg_ref[slot].astype(BF16)
        hg = jnp.minimum(jnp.dot(xb, wg_buf[...], preferred_element_type=F32) + bg_ref[0, 0], SWIGLU_LIMIT)
        hl = jnp.clip(jnp.dot(xb, wl_buf[...], preferred_element_type=F32) + bl_ref[0, 0],
                      -SWIGLU_LIMIT, SWIGLU_LIMIT)
        hh = (hg * jax.nn.sigmoid(SWIGLU_ALPHA * hg) * (hl + 1.0)).astype(BF16)

        if has_next:
            @pl.when(e_next != e_cur)
            def _():
                w_copy(0, e_next).start()
                w_copy(1, e_next).start()

        @pl.when(fresh)
        def _():
            w_copy(2, e_cur).wait()

        o_ref[...] = jnp.dot(hh, wd_buf[...], preferred_element_type=F32) + bd_ref[0, 0]

        if has_next:
            @pl.when(e_next != e_cur)
            def _():
                w_copy(2, e_next).start()

    @pl.when(b == 0)
    def _():
        for which in range(3):
            w_copy(which, e_cur).start()

        def body(r, c):
            row_copy(0, r, 0).start()
            return c
        lax.fori_loop(0, blk, body, 0, unroll=8)

    @pl.when(b + 1 < n_act)
    def _():
        compute(True)

    @pl.when(b + 1 == n_act)
    def _():
        compute(False)

    @pl.when(b >= n_act)
    def _():
        o_ref[...] = jnp.zeros(o_ref.shape, o_ref.dtype)


def _expert_ffn(x32, row_tok, blk_expert, n_act, w_glu16, b_glu, w_lin16, b_lin, w_down16, b_down, layer, *, blk):
    t, d = x32.shape
    depth, n_e, _, ff = w_glu16.shape
    n_blocks = blk_expert.shape[0]
    bmap = lambda i, be, tok, na: (layer, be[i], 0, 0)
    any_spec = pl.BlockSpec(memory_space=pl.ANY)
    grid_spec = pltpu.PrefetchScalarGridSpec(
        num_scalar_prefetch=3,
        grid=(n_blocks,),
        in_specs=[any_spec,
                  any_spec, pl.BlockSpec((1, 1, 1, ff), bmap),
                  any_spec, pl.BlockSpec((1, 1, 1, ff), bmap),
                  any_spec, pl.BlockSpec((1, 1, 1, d), bmap)],
        out_specs=pl.BlockSpec((blk, d), lambda i, be, tok, na: (i, 0)),
        scratch_shapes=[pltpu.VMEM((2, blk, d), F32),
                        pltpu.VMEM((d, ff), BF16), pltpu.VMEM((d, ff), BF16), pltpu.VMEM((ff, d), BF16),
                        pltpu.SemaphoreType.DMA((2,)), pltpu.SemaphoreType.DMA((3,))],
    )
    return pl.pallas_call(
        functools.partial(_expert_kernel, blk=blk, layer=layer),
        grid_spec=grid_spec,
        out_shape=jax.ShapeDtypeStruct((n_blocks * blk, d), F32),
        compiler_params=_cparams(("arbitrary",)),
        name="expert_ffn",
    )(blk_expert, row_tok, n_act, x32,
      w_glu16, b_glu.astype(F32).reshape(depth, n_e, 1, ff),
      w_lin16, b_lin.astype(F32).reshape(depth, n_e, 1, ff),
      w_down16, b_down.astype(F32).reshape(depth, n_e, 1, d))


def _combine_kernel(dest_ref, ys_hbm, x_ref, gate_ref, g_ref, b_ref, o32_ref, o16_ref, buf_ref, sem_ref, *, bt):
    i = pl.program_id(0)
    n = pl.num_programs(0)

    def row_copy(blk_idx, r, kk, dst_slot):
        row = dest_ref[(blk_idx * bt + r) * TOP_K + kk]
        return pltpu.make_async_copy(ys_hbm.at[pl.ds(row, 1), :], buf_ref.at[dst_slot, kk, pl.ds(r, 1), :],
                                     sem_ref.at[dst_slot])

    @pl.when(i == 0)
    def _():
        def body(r, c):
            for kk in range(TOP_K):
                row_copy(0, r, kk, 0).start()
            return c
        lax.fori_loop(0, bt, body, 0, unroll=2)

    slot = lax.rem(i, 2)
    pltpu.make_async_copy(buf_ref.at[slot], buf_ref.at[slot], sem_ref.at[slot]).wait()
    nxt = jnp.where(i + 1 < n, i + 1, 0)
    for r in range(bt):
        for kk in range(TOP_K):
            row_copy(nxt, r, kk, 1 - slot).start()
    gate = gate_ref[...]
    z = ALPHA * x_ref[...]
    for kk in range(TOP_K):
        z = z + gate[:, kk:kk + 1] * buf_ref[slot, kk]
    mu = jnp.mean(z, axis=-1, keepdims=True)
    zc = z - mu
    var = jnp.mean(zc * zc, axis=-1, keepdims=True)
    y = zc * lax.rsqrt(var + LN_EPS) * g_ref[...] + b_ref[...]
    o32_ref[...] = y
    o16_ref[...] = y.astype(BF16)

    @pl.when(i == n - 1)
    def _():
        pltpu.make_async_copy(buf_ref.at[1 - slot], buf_ref.at[1 - slot], sem_ref.at[1 - slot]).wait()


def _combine_ln(ys, dest, x32, gate, g, b, *, bt):
    t, d = x32.shape
    grid_spec = pltpu.PrefetchScalarGridSpec(
        num_scalar_prefetch=1,
        grid=(t // bt,),
        in_specs=[pl.BlockSpec(memory_space=pl.ANY),
                  pl.BlockSpec((bt, d), lambda i, dr: (i, 0)),
                  pl.BlockSpec((bt, LANES), lambda i, dr: (i, 0)),
                  pl.BlockSpec((1, d), lambda i, dr: (0, 0)),
                  pl.BlockSpec((1, d), lambda i, dr: (0, 0))],
        out_specs=[pl.BlockSpec((bt, d), lambda i, dr: (i, 0)),
                   pl.BlockSpec((bt, d), lambda i, dr: (i, 0))],
        scratch_shapes=[pltpu.VMEM((2, TOP_K, bt, d), F32), pltpu.SemaphoreType.DMA((2,))],
    )
    return pl.pallas_call(
        functools.partial(_combine_kernel, bt=bt),
        grid_spec=grid_spec,
        out_shape=[jax.ShapeDtypeStruct((t, d), F32), jax.ShapeDtypeStruct((t, d), BF16)],
        compiler_params=_cparams(("arbitrary",)),
        name="moe_combine_ln",
    )(dest, ys, x32, gate, g.astype(F32).reshape(1, d), b.astype(F32).reshape(1, d))


def _moe_block(x32, x16, w_router, b_router, w_glu16, b_glu, w_lin16, b_lin, w_down16, b_down, ln_g, ln_b, layer,
               *, blk, bt, router_bm):
    top_e, gate = _router(x16, w_router, b_router, bm=router_bm)
    row_tok, dest, blk_expert, n_act = _dispatch_plan(top_e, w_glu16.shape[1], blk, router_bm)
    ys = _expert_ffn(x32, row_tok, blk_expert, n_act, w_glu16, b_glu, w_lin16, b_lin, w_down16, b_down, layer,
                     blk=blk)
    return _combine_ln(ys, dest, x32, gate, ln_g, ln_b, bt=bt)


def kernel(x, w_in, nat_rpb, nat_norm_g, diff_lambda, diff_subln_g, t5_table, ml_conv_w, ml_gate_b, ml_norm_g,
           w_out, ln_mix_g, ln_mix_b, router_w, router_b, exp_w_glu, exp_b_glu, exp_w_lin, exp_b_lin,
           exp_w_down, exp_b_down, ln_ffn_g, ln_ffn_b):
    bsz, t, d = x.shape
    assert bsz == 1
    x32 = x.reshape(t, d)
    x16 = x32.astype(BF16)
    diff_bq, diff_bk = 1024, 1024
    bias_tiles, far = _diff_bias_tiles(t5_table, diff_bq, diff_bk)
    ml_col0 = (3 * NAT_WIDTH + 3 * DIFF_WIDTH) // MLSTM_WIDTH
    w_glu16, w_lin16, w_down16 = exp_w_glu.astype(BF16), exp_w_lin.astype(BF16), exp_w_down.astype(BF16)
    for l in range(DEPTH):
        w_main = w_in[l][:, :MAIN_WIDTH].astype(BF16)
        w_gate = jnp.pad(w_in[l][:, MAIN_WIDTH:], ((0, 0), (0, LANES - 4 * MLSTM_HEADS))).astype(BF16)
        u = _matmul(x16, w_main, bm=1024, bn=1024, out_dtype=BF16, name="inproj")
        gates = _matmul(x16, w_gate, bm=1024, bn=LANES, out_dtype=F32, name="inproj_gates")
        nat = _nat_attention(u, nat_rpb[l], nat_norm_g[l], blocks_per_step=16)
        lam_init = 0.8 - 0.6 * math.exp(-0.3 * l)
        dif = _diff_attention(u, bias_tiles, far, diff_lambda[l], diff_subln_g[l], lam_init,
                              bq=diff_bq, bk=diff_bk, rb=128, col0=3 * NAT_WIDTH // HEAD_DIM)
        ml = _mlstm_block(u, gates, ml_conv_w[l], ml_gate_b[l], ml_norm_g[l], lc=256, col0=ml_col0)
        x1_32, x1_16 = _proj_residual_ln([nat, dif, ml], w_out[l].astype(BF16), x32, ln_mix_g[l], ln_mix_b[l],
                                         bm=512, bn=512)
        x32, x16 = _moe_block(x1_32, x1_16, router_w[l], router_b[l], w_glu16, exp_b_glu, w_lin16, exp_b_lin,
                              w_down16, exp_b_down, ln_ffn_g[l], ln_ffn_b[l], l,
                              blk=256, bt=128, router_bm=512)
    return x32.reshape(bsz, t, d)
```

```python
import functools
import math

import numpy as np
import jax
import jax.numpy as jnp
from jax import lax
from jax.experimental import pallas as pl
from jax.experimental.pallas import tpu as pltpu

DEPTH = 2
HEAD_DIM = 128
NAT_HEADS = 16
DIFF_HEADS = 8
MLSTM_HEADS = 8
NAT_WIDTH = NAT_HEADS * HEAD_DIM
DIFF_WIDTH = DIFF_HEADS * HEAD_DIM
MLSTM_WIDTH = MLSTM_HEADS * HEAD_DIM
DIFF_QK_DIM = HEAD_DIM // 2
MAIN_WIDTH = 3 * NAT_WIDTH + 3 * DIFF_WIDTH + 4 * MLSTM_WIDTH
GRID_W = 64
NAT_WIN_ROWS = 8
NAT_WIN_COLS = 16
NAT_Q_ROWS = 2
NAT_KEY_ROWS = 10
REL_BUCKETS = 32
REL_MAX_DIST = 128
MLSTM_CHUNK = 64
MLSTM_CONV = 5
N_EXPERTS = 32
TOP_K = 4
SWIGLU_ALPHA = 1.702
SWIGLU_LIMIT = 7.0
MOE_BLOCK = 512
LN_EPS = 1e-5
ALPHA = (2.0 * DEPTH) ** 0.25
NEG_BIG = -1e30

LANES = 128
VMEM_LIMIT = 56 * 1024 * 1024
GATHER_DEPTH = 2

BF16 = jnp.bfloat16
F32 = jnp.float32


def _cparams(sem):
    return pltpu.CompilerParams(dimension_semantics=sem, vmem_limit_bytes=VMEM_LIMIT)


def _mm_kernel(x_ref, w_ref, o_ref):
    o_ref[...] = jnp.dot(x_ref[...], w_ref[...], preferred_element_type=F32).astype(o_ref.dtype)


def _matmul(x, w, *, bm, bn, out_dtype, name):
    m, k = x.shape
    n = w.shape[1]
    return pl.pallas_call(
        _mm_kernel,
        grid=(m // bm, n // bn),
        in_specs=[pl.BlockSpec((bm, k), lambda i, j: (i, 0)),
                  pl.BlockSpec((k, bn), lambda i, j: (0, j))],
        out_specs=pl.BlockSpec((bm, bn), lambda i, j: (i, j)),
        out_shape=jax.ShapeDtypeStruct((m, n), out_dtype),
        compiler_params=_cparams(("parallel", "parallel")),
        name=name,
    )(x, w)


def _proj_ln_kernel(*refs, n_parts, widths, bn, nj):
    part_refs = refs[:n_parts]
    w_ref, x_ref, g_ref, b_ref, o32_ref, o16_ref, mix_ref = refs[n_parts:]
    j = pl.program_id(1)

    @pl.when(j == 0)
    def _():
        off = 0
        for p_ref, wd in zip(part_refs, widths):
            mix_ref[:, off:off + wd] = p_ref[...]
            off += wd

    z = ALPHA * x_ref[...] + jnp.dot(mix_ref[...], w_ref[...], preferred_element_type=F32)
    for jj in range(nj):
        @pl.when(j == jj)
        def _(jj=jj):
            o32_ref[:, jj * bn:(jj + 1) * bn] = z

    @pl.when(j == nj - 1)
    def _():
        zz = o32_ref[...]
        mu = jnp.mean(zz, axis=-1, keepdims=True)
        zc = zz - mu
        var = jnp.mean(zc * zc, axis=-1, keepdims=True)
        y = zc * lax.rsqrt(var + LN_EPS) * g_ref[...] + b_ref[...]
        o32_ref[...] = y
        o16_ref[...] = y.astype(BF16)


def _proj_residual_ln(parts, w, x_res, g, b, *, bm, bn):
    t, d = x_res.shape
    widths = tuple(p.shape[1] for p in parts)
    kdim = sum(widths)
    nj = d // bn
    kern = functools.partial(_proj_ln_kernel, n_parts=len(parts), widths=widths, bn=bn, nj=nj)
    in_specs = [pl.BlockSpec((bm, wd), lambda i, j: (i, 0)) for wd in widths]
    in_specs += [pl.BlockSpec((kdim, bn), lambda i, j: (0, j)),
                 pl.BlockSpec((bm, bn), lambda i, j: (i, j)),
                 pl.BlockSpec((1, d), lambda i, j: (0, 0)),
                 pl.BlockSpec((1, d), lambda i, j: (0, 0))]
    return pl.pallas_call(
        kern,
        grid=(t // bm, nj),
        in_specs=in_specs,
        out_specs=[pl.BlockSpec((bm, d), lambda i, j: (i, 0)),
                   pl.BlockSpec((bm, d), lambda i, j: (i, 0))],
        out_shape=[jax.ShapeDtypeStruct((t, d), F32), jax.ShapeDtypeStruct((t, d), BF16)],
        scratch_shapes=[pltpu.VMEM((bm, kdim), BF16)],
        compiler_params=_cparams(("parallel", "arbitrary")),
        name="proj_ln",
    )(*parts, w, x_res, g.reshape(1, d), b.reshape(1, d))


def _nat_geometry(t):
    rows = t // GRID_W
    win_r = min(NAT_WIN_ROWS, rows)
    kr = NAT_KEY_ROWS
    assert rows >= kr and rows % NAT_Q_ROWS == 0
    n_blk = rows // NAT_Q_ROWS
    col = np.arange(GRID_W)
    c0 = np.clip(col - NAT_WIN_COLS // 2, 0, GRID_W - NAT_WIN_COLS)
    keys, ks_list, ty_list, tables = {}, [], [], []
    for i in range(n_blk):
        qr = i * NAT_Q_ROWS + np.arange(NAT_Q_ROWS)
        r0 = np.clip(qr - win_r // 2, 0, rows - win_r)
        ks = int(np.clip(r0[0], 0, rows - kr))
        key = (tuple(qr - ks), tuple(r0 - ks))
        if key not in keys:
            keys[key] = len(tables)
            q_row = np.repeat(qr, GRID_W)
            q_r0 = np.repeat(r0, GRID_W)
            q_col = np.tile(col, NAT_Q_ROWS)
            q_c0 = np.tile(c0, NAT_Q_ROWS)
            k_row = np.repeat(ks + np.arange(kr), GRID_W)
            k_col = np.tile(col, kr)
            row_ok = (k_row[None, :] >= q_r0[:, None]) & (k_row[None, :] < q_r0[:, None] + win_r)
            col_ok = (k_col[None, :] >= q_c0[:, None]) & (k_col[None, :] < q_c0[:, None] + NAT_WIN_COLS)
            dr = np.clip(k_row[None, :] - q_row[:, None] + NAT_WIN_ROWS - 1, 0, 2 * NAT_WIN_ROWS - 2)
            dc = np.clip(k_col[None, :] - q_col[:, None] + NAT_WIN_COLS - 1, 0, 2 * NAT_WIN_COLS - 2)
            tables.append((dr, dc, row_ok & col_ok))
        ks_list.append(ks)
        ty_list.append(keys[key])
    dr = np.stack([tb[0] for tb in tables]).astype(np.int32)
    dc = np.stack([tb[1] for tb in tables]).astype(np.int32)
    ok = np.stack([tb[2] for tb in tables])
    return np.asarray(ks_list, np.int32), np.asarray(ty_list, np.int32), dr, dc, ok


def _nat_bias_kernel(rp_ref, ok_ref, o_ref, *, dr_small):
    lane = lax.broadcasted_iota(jnp.int32, (GRID_W, LANES), 1)
    for ty, dr_ty in enumerate(dr_small):
        for qrl, dr_row in enumerate(dr_ty):
            for pair in range(len(dr_row) // 2):
                da, db = int(dr_row[2 * pair]), int(dr_row[2 * pair + 1])
                ya = _toeplitz_rows(rp_ref[0, da:da + 1, :], GRID_W, GRID_W - 1)
                yb = _toeplitz_rows(rp_ref[0, db:db + 1, :], GRID_W, LANES - 1)
                rows = slice(qrl * GRID_W, (qrl + 1) * GRID_W)
                cols = slice(pair * LANES, (pair + 1) * LANES)
                o_ref[ty, 0, rows, cols] = jnp.where(ok_ref[ty, rows, cols] > 0.5,
                                                     jnp.where(lane < GRID_W, ya, yb), NEG_BIG)


def _nat_bias_table(rpb, dr_small, ok):
    n_type, nq, kw = ok.shape
    assert 2 * GRID_W == LANES and NAT_KEY_ROWS % 2 == 0
    pad_c = GRID_W - NAT_WIN_COLS
    n_dr = rpb.shape[1]
    rp = jnp.pad(rpb.astype(F32), ((0, 0), (0, -n_dr % 8), (pad_c, LANES - rpb.shape[2] - pad_c)))
    return pl.pallas_call(
        functools.partial(_nat_bias_kernel, dr_small=dr_small.tolist()),
        grid=(NAT_HEADS,),
        in_specs=[pl.BlockSpec((1, rp.shape[1], LANES), lambda h: (h, 0, 0)),
                  pl.BlockSpec((n_type, nq, kw), lambda h: (0, 0, 0))],
        out_specs=pl.BlockSpec((n_type, 1, nq, kw), lambda h: (0, h, 0, 0)),
        out_shape=jax.ShapeDtypeStruct((n_type, NAT_HEADS, nq, kw), F32),
        compiler_params=_cparams(("parallel",)),
        name="nat_bias_table",
    )(rp, jnp.asarray(ok.astype(np.float32)))


def _nat_kernel(ks_ref, ty_ref, q_ref, k_ref, v_ref, bias_ref, g_ref, o_ref, *, blocks_per_step, kw):
    nq = NAT_Q_ROWS * GRID_W
    base = pl.program_id(1) * blocks_per_step
    scale = HEAD_DIM ** -0.5
    blocks = range(blocks_per_step)
    k0s = [pl.multiple_of(ks_ref[base + jj] * GRID_W, GRID_W) for jj in blocks]
    ss = [lax.dot_general(q_ref[jj * nq:(jj + 1) * nq, :], k_ref[pl.ds(k0s[jj], kw), :],
                          (((1,), (1,)), ((), ())), preferred_element_type=F32) * scale
          + bias_ref[ty_ref[base + jj], 0] for jj in blocks]
    ps = [jnp.exp(s - jnp.max(s, axis=-1, keepdims=True)) for s in ss]
    ls = [jnp.sum(p, axis=-1, keepdims=True) for p in ps]
    os = [jnp.dot(ps[jj].astype(BF16), v_ref[pl.ds(k0s[jj], kw), :], preferred_element_type=F32) / ls[jj]
          for jj in blocks]
    for jj in blocks:
        o = os[jj]
        ms = jnp.mean(o * o, axis=-1, keepdims=True)
        o_ref[jj * nq:(jj + 1) * nq, :] = (o * lax.rsqrt(ms + LN_EPS) * g_ref[0]).astype(o_ref.dtype)


def _nat_attention(u, rpb, norm_g, *, blocks_per_step):
    t = u.shape[0]
    ks, ty, dr, dc, ok = _nat_geometry(t)
    n_blk = ks.shape[0]
    kw = NAT_KEY_ROWS * GRID_W
    nq = NAT_Q_ROWS * GRID_W
    n_type = ok.shape[0]
    bias = _nat_bias_table(rpb, dr[:, ::GRID_W, ::GRID_W], ok)
    g = norm_g.astype(F32).reshape(NAT_HEADS, 1, HEAD_DIM)
    rows_per_step = blocks_per_step * nq
    kern = functools.partial(_nat_kernel, blocks_per_step=blocks_per_step, kw=kw)
    grid_spec = pltpu.PrefetchScalarGridSpec(
        num_scalar_prefetch=2,
        grid=(NAT_HEADS, n_blk // blocks_per_step),
        in_specs=[pl.BlockSpec((rows_per_step, HEAD_DIM), lambda h, g_, *_: (g_, h)),
                  pl.BlockSpec((t, HEAD_DIM), lambda h, g_, *_: (0, NAT_HEADS + h)),
                  pl.BlockSpec((t, HEAD_DIM), lambda h, g_, *_: (0, 2 * NAT_HEADS + h)),
                  pl.BlockSpec((n_type, 1, nq, kw), lambda h, g_, *_: (0, h, 0, 0)),
                  pl.BlockSpec((1, 1, HEAD_DIM), lambda h, g_, *_: (h, 0, 0))],
        out_specs=pl.BlockSpec((rows_per_step, HEAD_DIM), lambda h, g_, *_: (g_, h)),
    )
    return pl.pallas_call(
        kern,
        grid_spec=grid_spec,
        out_shape=jax.ShapeDtypeStruct((t, NAT_WIDTH), BF16),
        compiler_params=_cparams(("parallel", "arbitrary")),
        name="nat_attn",
    )(jnp.asarray(ks), jnp.asarray(ty), u, u, u, bias, g)


def _t5_bucket(rel):
    half = REL_BUCKETS // 2
    max_exact = half // 2
    ret = jnp.where(rel > 0, half, 0)
    n = jnp.abs(rel)
    n_f = jnp.maximum(n, 1).astype(jnp.float32)
    large = max_exact + (jnp.log(n_f / max_exact) / math.log(REL_MAX_DIST / max_exact)
                         * (half - max_exact)).astype(jnp.int32)
    large = jnp.minimum(large, half - 1)
    return ret + jnp.where(n < max_exact, n, large)


def _diff_kernel(far_ref, q_ref, k_ref, v_ref, bias_ref, lp_ref, g_ref, o_ref,
                 qs_ref, m_ref, acc_ref, *, bk, ratio, nk, lam_init, rb):
    h = pl.program_id(0)
    qi = pl.program_id(1)
    bq = q_ref.shape[0]
    n_rb = 2 * bq // rb

    q = q_ref[...]
    lane = lax.broadcasted_iota(jnp.int32, q.shape, 1)
    qsc = q * jnp.asarray(DIFF_QK_DIM ** -0.5, q.dtype)
    zero = jnp.zeros_like(qsc)
    qs_ref[0:bq, :] = jnp.where(lane < DIFF_QK_DIM, qsc, zero)
    qs_ref[bq:2 * bq, :] = jnp.where(lane >= DIFF_QK_DIM, qsc, zero)
    m_ref[...] = jnp.full(m_ref.shape, -jnp.inf, F32)
    acc_ref[...] = jnp.zeros(acc_ref.shape, F32)

    def step(ki, near_idx, const):
        k0 = pl.multiple_of(ki * bk, bk)
        k = k_ref[pl.ds(k0, bk), :]
        v = v_ref[pl.ds(k0, bk), :]
        v_ext = jnp.concatenate([v, jnp.ones_like(v)], axis=1)
        for r in range(n_rb):
            rows = slice(r * rb, (r + 1) * rb)
            s = lax.dot_general(qs_ref[rows, :], k, (((1,), (1,)), ((), ())), preferred_element_type=F32)
            if near_idx is not None:
                r0 = (r * rb) % bq
                s = s + bias_ref[0, near_idx, r0:r0 + rb, :]
            mt = jnp.max(s, axis=-1, keepdims=True)
            if const is not None:
                mt = mt + const
            m_prev = m_ref[rows, :]
            m_new = jnp.maximum(m_prev, mt)
            a = jnp.exp(m_prev - m_new)
            shift = m_new if const is None else m_new - const
            p = jnp.exp(s - jnp.tile(shift, (1, bk // LANES)))
            acc_ref[rows, :] = (jnp.tile(a, (1, 2)) * acc_ref[rows, :]
                                + jnp.dot(p.astype(BF16), v_ext, preferred_element_type=F32))
            m_ref[rows, :] = m_new

    c_left = far_ref[h, 0]
    c_right = far_ref[h, 1]
    first_near = qi * ratio - 1
    n_near = ratio + 2

    def left_body(ki, carry):
        step(ki, None, c_left)
        return carry

    lax.fori_loop(0, jnp.clip(first_near, 0, nk), left_body, 0)
    for tt in range(n_near):
        ki = first_near + tt

        @pl.when((ki >= 0) & (ki < nk))
        def _(ki=ki, tt=tt):
            step(ki, tt, None)

    def right_body(ki, carry):
        step(ki, None, c_right)
        return carry

    lax.fori_loop(jnp.clip(first_near + n_near, 0, nk), nk, right_body, 0)

    acc = acc_ref[...]
    o = acc[:, 0:HEAD_DIM] / acc[:, HEAD_DIM:2 * HEAD_DIM]
    lp = lp_ref[...]
    lam = (jnp.exp(jnp.sum(lp[0:1] * lp[1:2], axis=-1, keepdims=True))
           - jnp.exp(jnp.sum(lp[2:3] * lp[3:4], axis=-1, keepdims=True)) + lam_init)
    d = o[0:bq] - lam * o[bq:2 * bq]
    ms = jnp.mean(d * d, axis=-1, keepdims=True)
    o_ref[...] = (d * lax.rsqrt(ms + LN_EPS) * g_ref[...] * (1.0 - lam_init)).astype(o_ref.dtype)


def _diff_bias_tiles(t5_table, bq, bk):
    ratio = bq // bk
    span = bq + bk + 1
    rel = jnp.arange(-span, span + 1, dtype=jnp.int32)
    vec = jnp.take(t5_table.astype(F32), _t5_bucket(rel), axis=0)
    vec_t = vec.T
    diags = jnp.stack([vec_t[:, (tt - 1) * bk + span - (bq - 1):(tt - 1) * bk + span + bk]
                       for tt in range(ratio + 2)], axis=1)
    tiles = _toeplitz(diags, bq, bk)
    far = jnp.stack([vec[0], vec[-1]], axis=1)
    return tiles, far


def _toeplitz_rows(w_row, n_rows, origin):
    width = w_row.shape[1]
    x = jnp.broadcast_to(w_row, (n_rows, width))
    return pltpu.roll(x, (width - origin) % width, 1, stride=1, stride_axis=0)


def _toeplitz_kernel(d_ref, o_ref):
    bq, bk = o_ref.shape[2], o_ref.shape[3]
    o_ref[0, 0] = _toeplitz_rows(d_ref[0, 0], bq, bq - 1)[:, :bk]


def _toeplitz(w, n_rows, n_cols):
    na, nb, ln = w.shape
    width = -(-ln // LANES) * LANES
    wp = jnp.pad(w, ((0, 0), (0, 0), (0, width - ln))).reshape(na, nb, 1, width)
    return pl.pallas_call(
        _toeplitz_kernel,
        grid=(na, nb),
        in_specs=[pl.BlockSpec((1, 1, 1, width), lambda a, b: (a, b, 0, 0))],
        out_specs=pl.BlockSpec((1, 1, n_rows, n_cols), lambda a, b: (a, b, 0, 0)),
        out_shape=jax.ShapeDtypeStruct((na, nb, n_rows, n_cols), F32),
        compiler_params=_cparams(("parallel", "parallel")),
        name="toeplitz_tiles",
    )(wp)


def _diff_attention(u, bias_tiles, far, lam_params, subln_g, lam_init, *, bq, bk, rb, col0):
    t = u.shape[0]
    assert bk >= REL_MAX_DIST and bq % bk == 0
    ratio = bq // bk
    nk = t // bk
    n_near = ratio + 2
    assert bq % rb == 0
    kern = functools.partial(_diff_kernel, bk=bk, ratio=ratio, nk=nk, lam_init=lam_init, rb=rb)
    grid_spec = pltpu.PrefetchScalarGridSpec(
        num_scalar_prefetch=1,
        grid=(DIFF_HEADS, t // bq),
        in_specs=[pl.BlockSpec((bq, HEAD_DIM), lambda h, i, *_: (i, col0 + h)),
                  pl.BlockSpec((t, HEAD_DIM), lambda h, i, *_: (0, col0 + DIFF_HEADS + h)),
                  pl.BlockSpec((t, HEAD_DIM), lambda h, i, *_: (0, col0 + 2 * DIFF_HEADS + h)),
                  pl.BlockSpec((1, n_near, bq, bk), lambda h, i, *_: (h, 0, 0, 0)),
                  pl.BlockSpec((4, DIFF_QK_DIM), lambda h, i, *_: (0, 0)),
                  pl.BlockSpec((1, HEAD_DIM), lambda h, i, *_: (0, 0))],
        out_specs=pl.BlockSpec((bq, HEAD_DIM), lambda h, i, *_: (i, h)),
        scratch_shapes=[pltpu.VMEM((2 * bq, HEAD_DIM), BF16),
                        pltpu.VMEM((2 * bq, LANES), F32),
                        pltpu.VMEM((2 * bq, 2 * HEAD_DIM), F32)],
    )
    return pl.pallas_call(
        kern,
        grid_spec=grid_spec,
        out_shape=jax.ShapeDtypeStruct((t, DIFF_WIDTH), BF16),
        compiler_params=_cparams(("parallel", "arbitrary")),
        name="diff_attn",
    )(far, u, u, u, bias_tiles, lam_params.astype(F32), subln_g.astype(F32).reshape(1, HEAD_DIM))


CONV_HALO = 16


def _conv_silu_kernel(prev_ref, cur_ref, next_ref, w_ref, o_ref, *, k_scale):
    i = pl.program_id(0)
    j = pl.program_id(1)
    bt = cur_ref.shape[0]
    x = cur_ref[...].astype(F32)
    p = jnp.where(i > 0, prev_ref[...].astype(F32), 0.0)
    nx = jnp.where(i < pl.num_programs(0) - 1, next_ref[...].astype(F32), 0.0)
    ext = jnp.concatenate([p, x, nx], axis=0)
    n_ext = bt + 2 * CONV_HALO
    half = MLSTM_CONV // 2
    acc = w_ref[half:half + 1, :] * x
    for tap in range(MLSTM_CONV):
        s = tap - half
        if s == 0:
            continue
        rolled = pltpu.roll(ext, (n_ext - s) % n_ext, 0)
        acc = acc + w_ref[tap:tap + 1, :] * rolled[CONV_HALO:CONV_HALO + bt]
    y = acc * jax.nn.sigmoid(acc)
    o_ref[...] = (y * jnp.where(j == 1, k_scale, 1.0)).astype(o_ref.dtype)


def _conv_silu(u, conv_w, *, bt, col0):
    t = u.shape[0]
    w = MLSTM_WIDTH
    nb = t // bt
    hb = bt // CONV_HALO
    last = t // CONV_HALO - 1
    return pl.pallas_call(
        functools.partial(_conv_silu_kernel, k_scale=HEAD_DIM ** -0.5),
        grid=(nb, 2),
        in_specs=[pl.BlockSpec((CONV_HALO, w), lambda i, j: (jnp.maximum(i * hb - 1, 0), col0 + j)),
                  pl.BlockSpec((bt, w), lambda i, j: (i, col0 + j)),
                  pl.BlockSpec((CONV_HALO, w), lambda i, j: (jnp.minimum((i + 1) * hb, last), col0 + j)),
                  pl.BlockSpec((MLSTM_CONV, w), lambda i, j: (0, j))],
        out_specs=pl.BlockSpec((bt, w), lambda i, j: (i, j)),
        out_shape=jax.ShapeDtypeStruct((t, 2 * w), BF16),
        compiler_params=_cparams(("parallel", "parallel")),
        name="mlstm_conv",
    )(u, u, u, conv_w.astype(F32))


def _split3(x):
    x1 = x.astype(BF16)
    r1 = x - x1.astype(F32)
    x2 = r1.astype(BF16)
    x3 = (r1 - x2.astype(F32)).astype(BF16)
    return x1, x2, x3


def _log_sigmoid(x):
    return jnp.minimum(x, 0.0) - jnp.log(1.0 + jnp.exp(-jnp.abs(x)))


def _mlstm_units(units, c_ref, n_ref, m_ref, lc):
    nt_dims = (((1,), (1,)), ((), ()))
    tn_dims = (((0,), (0,)), ((), ()))
    rd = lambda u, name: u[name][:, u["sl"]]
    dmats = [jnp.where(u["mask"], u["bcol"] - u["brow"] + u["igrow"], -jnp.inf) for u in units]
    m_prevs = [m_ref[u["idx"]] for u in units]
    inters = [u["bcol"] + mp for u, mp in zip(units, m_prevs)]
    m_ts = [jnp.maximum(it, jnp.max(d, axis=-1, keepdims=True)) for it, d in zip(inters, dmats)]
    w_inters = [jnp.exp(it - mt) for it, mt in zip(inters, m_ts)]
    ss = [lax.dot_general(rd(u, "q_ref"), rd(u, "k_ref"), nt_dims, preferred_element_type=F32) for u in units]
    ps = [jnp.exp(d - mt) * s for d, mt, s in zip(dmats, m_ts, ss)]
    nums = [wi * jnp.dot(rd(u, "q_ref"), c_ref[u["idx"]].astype(BF16), preferred_element_type=F32)
            + jnp.dot(p.astype(BF16), rd(u, "v_ref"), preferred_element_type=F32)
            for u, wi, p in zip(units, w_inters, ps)]
    dens = [wi * jnp.sum(rd(u, "q_ref").astype(F32) * n_ref[u["idx"]], axis=-1, keepdims=True)
            + jnp.sum(p, axis=-1, keepdims=True) for u, wi, p in zip(units, w_inters, ps)]
    for u, num, den, mt in zip(units, nums, dens, m_ts):
        u["h_ref"][:, u["sl"]] = num / jnp.maximum(jnp.abs(den), jnp.exp(-mt))
    gs = [u["bcol"][lc - 1:lc] if u["fwd"] else u["bcol"][0:1] for u in units]
    a_s = [g - u["bcol"] + u["igcol"] for u, g in zip(units, gs)]
    m_locs = [jnp.max(a, axis=0, keepdims=True) for a in a_s]
    kws = [rd(u, "k_ref").astype(F32) * jnp.exp(a - ml) for u, a, ml in zip(units, a_s, m_locs)]
    c_locs = [lax.dot_general(kw.astype(BF16), rd(u, "v_ref"), tn_dims, preferred_element_type=F32)
              for u, kw in zip(units, kws)]
    for u, g, mp, ml, kw, c_loc in zip(units, gs, m_prevs, m_locs, kws, c_locs):
        m_new = jnp.maximum(g + mp, ml)
        s_old = jnp.exp(g + mp - m_new)
        s_loc = jnp.exp(ml - m_new)
        c_ref[u["idx"]] = s_old * c_ref[u["idx"]] + s_loc * c_loc
        n_ref[u["idx"]] = s_old * n_ref[u["idx"]] + s_loc * jnp.sum(kw, axis=0, keepdims=True)
        m_ref[u["idx"]] = m_new


def _mlstm_kernel(qf_ref, kf_ref, vf_ref, gf_ref, qb_ref, kb_ref, vb_ref, gb_ref, gbias_ref,
                  hf_ref, hb_ref, c_ref, n_ref, m_ref, *, lc):
    nh = MLSTM_HEADS

    @pl.when(pl.program_id(0) == 0)
    def _():
        c_ref[...] = jnp.zeros(c_ref.shape, F32)
        n_ref[...] = jnp.zeros(n_ref.shape, F32)
        m_ref[...] = jnp.zeros(m_ref.shape, F32)

    t_io = lax.broadcasted_iota(jnp.int32, (lc, lc), 0)
    s_io = lax.broadcasted_iota(jnp.int32, (lc, lc), 1)
    units = []
    for fwd in (True, False):
        g_ref, q_ref, k_ref, v_ref, h_ref = ((gf_ref, qf_ref, kf_ref, vf_ref, hf_ref) if fwd
                                             else (gb_ref, qb_ref, kb_ref, vb_ref, hb_ref))
        gates = g_ref[...] + gbias_ref[...]
        mask = (s_io <= t_io) if fwd else (s_io >= t_io)
        tri = jnp.where(mask, 1.0, 0.0).astype(BF16)
        cum = None
        for piece in _split3(_log_sigmoid(gates)):
            part = jnp.dot(tri, piece, preferred_element_type=F32)
            cum = part if cum is None else cum + part
        gates_t = jnp.transpose(gates)
        cum_t = jnp.transpose(cum)
        lane0 = 0 if fwd else 2 * nh
        for hh in range(nh):
            li, lb = lane0 + hh, lane0 + nh + hh
            units.append(dict(fwd=fwd, idx=(0 if fwd else nh) + hh, sl=slice(hh * HEAD_DIM, (hh + 1) * HEAD_DIM),
                              q_ref=q_ref, k_ref=k_ref, v_ref=v_ref, h_ref=h_ref, mask=mask,
                              bcol=cum[:, lb:lb + 1], igcol=gates[:, li:li + 1],
                              brow=cum_t[lb:lb + 1, :], igrow=gates_t[li:li + 1, :]))
    _mlstm_units(units, c_ref, n_ref, m_ref, lc)


def _mlstm_scan(qk, u, gates, gate_b, *, lc, v_col):
    t = qk.shape[0]
    w = MLSTM_WIDTH
    nc = t // lc
    gbias = jnp.pad(gate_b.astype(F32), (0, LANES - gate_b.shape[0])).reshape(1, LANES)
    fmap = lambda col: (lambda c: (c, col))
    bmap = lambda col: (lambda c: (nc - 1 - c, col))
    nhd = 2 * MLSTM_HEADS
    return pl.pallas_call(
        functools.partial(_mlstm_kernel, lc=lc),
        grid=(nc,),
        in_specs=[pl.BlockSpec((lc, w), fmap(0)), pl.BlockSpec((lc, w), fmap(1)),
                  pl.BlockSpec((lc, w), fmap(v_col)), pl.BlockSpec((lc, LANES), fmap(0)),
                  pl.BlockSpec((lc, w), bmap(0)), pl.BlockSpec((lc, w), bmap(1)),
                  pl.BlockSpec((lc, w), bmap(v_col)), pl.BlockSpec((lc, LANES), bmap(0)),
                  pl.BlockSpec((1, LANES), lambda c: (0, 0))],
        out_specs=[pl.BlockSpec((lc, w), fmap(0)), pl.BlockSpec((lc, w), bmap(0))],
        out_shape=[jax.ShapeDtypeStruct((t, w), F32), jax.ShapeDtypeStruct((t, w), F32)],
        scratch_shapes=[pltpu.VMEM((nhd, HEAD_DIM, HEAD_DIM), F32),
                        pltpu.VMEM((nhd, 1, HEAD_DIM), F32),
                        pltpu.VMEM((nhd, 1, 1), F32)],
        compiler_params=_cparams(("arbitrary",)),
        name="mlstm_scan",
    )(qk, qk, u, gates, qk, qk, u, gates, gbias)


def _mlstm_out_kernel(hf_ref, hb_ref, o_ref, g_ref, out_ref):
    for hh in range(MLSTM_HEADS):
        sl = slice(hh * HEAD_DIM, (hh + 1) * HEAD_DIM)
        x = (hf_ref[:, sl] + hb_ref[:, sl]) * jax.nn.sigmoid(o_ref[:, sl].astype(F32))
        xc = x - jnp.mean(x, axis=-1, keepdims=True)
        y = xc * lax.rsqrt(jnp.mean(xc * xc, axis=-1, keepdims=True) + LN_EPS) * g_ref[:, sl]
        out_ref[:, sl] = y.astype(out_ref.dtype)


def _mlstm_out(hf, hb, u, norm_g, *, bt, o_col):
    t, w = hf.shape
    return pl.pallas_call(
        _mlstm_out_kernel,
        grid=(t // bt,),
        in_specs=[pl.BlockSpec((bt, w), lambda i: (i, 0)), pl.BlockSpec((bt, w), lambda i: (i, 0)),
                  pl.BlockSpec((bt, w), lambda i: (i, o_col)), pl.BlockSpec((1, w), lambda i: (0, 0))],
        out_specs=pl.BlockSpec((bt, w), lambda i: (i, 0)),
        out_shape=jax.ShapeDtypeStruct((t, w), BF16),
        compiler_params=_cparams(("parallel",)),
        name="mlstm_out",
    )(hf, hb, u, norm_g.astype(F32).reshape(1, w))


def _mlstm_block(u, gates, conv_w, gate_b, norm_g, *, lc, col0):
    qk = _conv_silu(u, conv_w, bt=512, col0=col0)
    hf, hb = _mlstm_scan(qk, u, gates, gate_b, lc=lc, v_col=col0 + 2)
    return _mlstm_out(hf, hb, u, norm_g, bt=512, o_col=col0 + 3)


def _router_kernel(x_ref, w_ref, b_ref, e_ref, g_ref):
    logits = jnp.dot(x_ref[...], w_ref[...], preferred_element_type=F32) + b_ref[...]
    lane = lax.broadcasted_iota(jnp.int32, logits.shape, 1)
    work = logits
    e_out = jnp.zeros(logits.shape, jnp.int32)
    vals = []
    for kk in range(TOP_K):
        m = jnp.max(work, axis=-1, keepdims=True)
        idx = jnp.min(jnp.where(work == m, lane, LANES), axis=-1, keepdims=True)
        e_out = jnp.where(lane == kk, idx, e_out)
        vals.append(m)
        work = jnp.where(lane == idx, -jnp.inf, work)
    ex = [jnp.exp(v - vals[0]) for v in vals]
    den = ex[0]
    for e in ex[1:]:
        den = den + e
    g_out = jnp.zeros(logits.shape, F32)
    for kk in range(TOP_K):
        g_out = jnp.where(lane == kk, ex[kk] / den, g_out)
    e_ref[...] = e_out
    g_ref[...] = g_out


def _router(x16, w_router, b_router, *, bm):
    t, d = x16.shape
    n_e = w_router.shape[1]
    w = jnp.pad(w_router, ((0, 0), (0, LANES - n_e))).astype(BF16)
    b = jnp.pad(b_router.astype(F32), (0, LANES - n_e), constant_values=NEG_BIG).reshape(1, LANES)
    return pl.pallas_call(
        _router_kernel,
        grid=(t // bm,),
        in_specs=[pl.BlockSpec((bm, d), lambda i: (i, 0)),
                  pl.BlockSpec((d, LANES), lambda i: (0, 0)),
                  pl.BlockSpec((1, LANES), lambda i: (0, 0))],
        out_specs=[pl.BlockSpec((bm, LANES), lambda i: (i, 0)),
                   pl.BlockSpec((bm, LANES), lambda i: (i, 0))],
        out_shape=[jax.ShapeDtypeStruct((t, LANES), jnp.int32), jax.ShapeDtypeStruct((t, LANES), F32)],
        compiler_params=_cparams(("parallel",)),
        name="router",
    )(x16, w, b)


def _rank_kernel(e_ref, rank_ref, cnt_ref, base_ref):
    @pl.when(pl.program_id(0) == 0)
    def _():
        base_ref[...] = jnp.zeros(base_ref.shape, F32)

    e = e_ref[...]
    bt = e.shape[0]
    lane = lax.broadcasted_iota(jnp.int32, e.shape, 1)
    onehots = [jnp.where(lane == e[:, kk:kk + 1], 1.0, 0.0) for kk in range(TOP_K)]
    tot = onehots[0]
    for oh in onehots[1:]:
        tot = tot + oh
    t_io = lax.broadcasted_iota(jnp.int32, (bt, bt), 0)
    s_io = lax.broadcasted_iota(jnp.int32, (bt, bt), 1)
    earlier = jnp.where(s_io < t_io, 1.0, 0.0).astype(BF16)
    pos = jnp.dot(earlier, tot.astype(BF16), preferred_element_type=F32) + base_ref[...]
    out = jnp.zeros(e.shape, jnp.int32)
    for kk in range(TOP_K):
        rk = jnp.sum(onehots[kk] * pos, axis=-1, keepdims=True)
        out = jnp.where(lane == kk, rk.astype(jnp.int32), out)
    rank_ref[...] = out
    base_ref[...] = base_ref[...] + jnp.sum(tot, axis=0, keepdims=True)
    cnt_ref[...] = base_ref[...]


def _slot_ranks(top_e, *, bt):
    t = top_e.shape[0]
    return pl.pallas_call(
        _rank_kernel,
        grid=(t // bt,),
        in_specs=[pl.BlockSpec((bt, LANES), lambda i: (i, 0))],
        out_specs=[pl.BlockSpec((bt, LANES), lambda i: (i, 0)), pl.BlockSpec((1, LANES), lambda i: (0, 0))],
        out_shape=[jax.ShapeDtypeStruct((t, LANES), jnp.int32), jax.ShapeDtypeStruct((1, LANES), F32)],
        scratch_shapes=[pltpu.VMEM((1, LANES), F32)],
        compiler_params=_cparams(("arbitrary",)),
        name="moe_rank",
    )(top_e)


def _dispatch_plan(top_e_lanes, n_experts, blk, rank_bt):
    t = top_e_lanes.shape[0]
    n_slots = t * TOP_K
    n_blocks = -(-n_slots // blk) + n_experts
    rank_lanes, counts_f = _slot_ranks(top_e_lanes, bt=rank_bt)
    slot_e = top_e_lanes[:, :TOP_K].reshape(-1)
    rank = rank_lanes[:, :TOP_K].reshape(-1)
    counts = counts_f[0, :n_experts].astype(jnp.int32)
    onehot = (slot_e[:, None] == jnp.arange(n_experts, dtype=jnp.int32)[None, :]).astype(jnp.int32)
    padded = (counts + blk - 1) // blk * blk
    pad_end = jnp.cumsum(padded)
    pad_start = pad_end - padded
    dest = (jnp.sum(pad_start[None, :] * onehot, axis=1) + rank).astype(jnp.int32)
    slot_tok = jnp.arange(n_slots, dtype=jnp.int32) // TOP_K
    row_tok = jnp.zeros((n_blocks * blk,), jnp.int32).at[dest].set(slot_tok)
    n_act = (pad_end[-1] // blk).astype(jnp.int32)
    blk_id = jnp.minimum(jnp.arange(n_blocks, dtype=jnp.int32), n_act - 1)
    blk_expert = jnp.minimum(jnp.sum((pad_end[None, :] <= (blk_id * blk)[:, None]).astype(jnp.int32), axis=1),
                             n_experts - 1)
    return row_tok, dest, blk_expert, n_act.reshape(1)


def _expert_kernel(be_ref, tok_ref, nact_ref, x_hbm, wg_hbm, bg_ref, wl_hbm, bl_ref, wd_hbm, bd_ref, o_ref,
                   xg_ref, wg_buf, wl_buf, wd_buf, sem_ref, wsem_ref, *, blk, layer):
    b = pl.program_id(0)
    n_blocks = pl.num_programs(0)
    n_act = nact_ref[0]
    n_slots = GATHER_DEPTH + 1
    slot = lax.rem(b, n_slots)
    e_cur = be_ref[b]
    e_prev = be_ref[jnp.maximum(b - 1, 0)]
    e_next = be_ref[jnp.minimum(b + 1, n_blocks - 1)]
    fresh = (b == 0) | (e_cur != e_prev)

    def row_copy(blk_idx, r, dst_slot):
        tok = tok_ref[blk_idx * blk + r]
        return pltpu.make_async_copy(x_hbm.at[pl.ds(tok, 1), :], xg_ref.at[dst_slot, pl.ds(r, 1), :],
                                     sem_ref.at[dst_slot])

    def w_copy(which, e):
        src, dst = ((wg_hbm, wg_buf), (wl_hbm, wl_buf), (wd_hbm, wd_buf))[which]
        return pltpu.make_async_copy(src.at[layer, e], dst, wsem_ref.at[which])

    def compute(gather_ahead, has_next):
        pltpu.make_async_copy(xg_ref.at[slot], xg_ref.at[slot], sem_ref.at[slot]).wait()

        @pl.when(fresh)
        def _():
            w_copy(0, e_cur).wait()
            w_copy(1, e_cur).wait()

        if gather_ahead:
            ahead_slot = lax.rem(b + GATHER_DEPTH, n_slots)
            for r in range(blk):
                row_copy(b + GATHER_DEPTH, r, ahead_slot).start()
        xb = xg_ref[slot].astype(BF16)
        hg = jnp.minimum(jnp.dot(xb, wg_buf[...], preferred_element_type=F32) + bg_ref[0, 0], SWIGLU_LIMIT)
        hl = jnp.clip(jnp.dot(xb, wl_buf[...], preferred_element_type=F32) + bl_ref[0, 0],
                      -SWIGLU_LIMIT, SWIGLU_LIMIT)
        hh = (hg * jax.nn.sigmoid(SWIGLU_ALPHA * hg) * (hl + 1.0)).astype(BF16)

        if has_next:
            @pl.when(e_next != e_cur)
            def _():
                w_copy(0, e_next).start()
                w_copy(1, e_next).start()

        @pl.when(fresh)
        def _():
            w_copy(2, e_cur).wait()

        o_ref[...] = jnp.dot(hh, wd_buf[...], preferred_element_type=F32) + bd_ref[0, 0]

        if has_next:
            @pl.when(e_next != e_cur)
            def _():
                w_copy(2, e_next).start()

    @pl.when(b == 0)
    def _():
        for which in range(3):
            w_copy(which, e_cur).start()

        for first in range(GATHER_DEPTH):
            @pl.when(first < n_act)
            def _(first=first):
                def body(r, c):
                    row_copy(first, r, first).start()
                    return c
                lax.fori_loop(0, blk, body, 0, unroll=8)

    @pl.when(b + GATHER_DEPTH < n_act)
    def _():
        compute(True, True)

    @pl.when((b + 1 < n_act) & (b + GATHER_DEPTH >= n_act))
    def _():
        compute(False, True)

    @pl.when(b + 1 == n_act)
    def _():
        compute(False, False)

    @pl.when(b >= n_act)
    def _():
        o_ref[...] = jnp.zeros(o_ref.shape, o_ref.dtype)


def _expert_ffn(x32, row_tok, blk_expert, n_act, w_glu16, b_glu, w_lin16, b_lin, w_down16, b_down, layer, *, blk):
    t, d = x32.shape
    depth, n_e, _, ff = w_glu16.shape
    n_blocks = blk_expert.shape[0]
    bmap = lambda i, be, tok, na: (layer, be[i], 0, 0)
    any_spec = pl.BlockSpec(memory_space=pl.ANY)
    grid_spec = pltpu.PrefetchScalarGridSpec(
        num_scalar_prefetch=3,
        grid=(n_blocks,),
        in_specs=[any_spec,
                  any_spec, pl.BlockSpec((1, 1, 1, ff), bmap),
                  any_spec, pl.BlockSpec((1, 1, 1, ff), bmap),
                  any_spec, pl.BlockSpec((1, 1, 1, d), bmap)],
        out_specs=pl.BlockSpec((blk, d), lambda i, be, tok, na: (i, 0)),
        scratch_shapes=[pltpu.VMEM((GATHER_DEPTH + 1, blk, d), F32),
                        pltpu.VMEM((d, ff), BF16), pltpu.VMEM((d, ff), BF16), pltpu.VMEM((ff, d), BF16),
                        pltpu.SemaphoreType.DMA((GATHER_DEPTH + 1,)), pltpu.SemaphoreType.DMA((3,))],
    )
    return pl.pallas_call(
        functools.partial(_expert_kernel, blk=blk, layer=layer),
        grid_spec=grid_spec,
        out_shape=jax.ShapeDtypeStruct((n_blocks * blk, d), F32),
        compiler_params=_cparams(("arbitrary",)),
        name="expert_ffn",
    )(blk_expert, row_tok, n_act, x32,
      w_glu16, b_glu.astype(F32).reshape(depth, n_e, 1, ff),
      w_lin16, b_lin.astype(F32).reshape(depth, n_e, 1, ff),
      w_down16, b_down.astype(F32).reshape(depth, n_e, 1, d))


def _combine_kernel(dest_ref, ys_hbm, x_ref, gate_ref, g_ref, b_ref, o32_ref, o16_ref, buf_ref, sem_ref, *, bt):
    i = pl.program_id(0)
    n = pl.num_programs(0)

    def row_copy(blk_idx, r, kk, dst_slot):
        row = dest_ref[(blk_idx * bt + r) * TOP_K + kk]
        return pltpu.make_async_copy(ys_hbm.at[pl.ds(row, 1), :], buf_ref.at[dst_slot, kk, pl.ds(r, 1), :],
                                     sem_ref.at[dst_slot])

    def slot_wait(s):
        pltpu.make_async_copy(buf_ref.at[s], buf_ref.at[s], sem_ref.at[s]).wait()

    @pl.when(i == 0)
    def _():
        def body(r, c):
            for blk_idx in range(GATHER_DEPTH):
                for kk in range(TOP_K):
                    row_copy(jnp.minimum(blk_idx, n - 1), r, kk, blk_idx).start()
            return c
        lax.fori_loop(0, bt, body, 0)

    n_slots = GATHER_DEPTH + 1
    slot = lax.rem(i, n_slots)
    slot_wait(slot)
    ahead = jnp.where(i + GATHER_DEPTH < n, i + GATHER_DEPTH, 0)
    ahead_slot = lax.rem(i + GATHER_DEPTH, n_slots)
    for r in range(bt):
        for kk in range(TOP_K):
            row_copy(ahead, r, kk, ahead_slot).start()
    gate = gate_ref[...]
    z = ALPHA * x_ref[...]
    for kk in range(TOP_K):
        z = z + gate[:, kk:kk + 1] * buf_ref[slot, kk]
    mu = jnp.mean(z, axis=-1, keepdims=True)
    zc = z - mu
    var = jnp.mean(zc * zc, axis=-1, keepdims=True)
    y = zc * lax.rsqrt(var + LN_EPS) * g_ref[...] + b_ref[...]
    o32_ref[...] = y
    o16_ref[...] = y.astype(BF16)

    @pl.when(i == n - 1)
    def _():
        for extra in range(1, GATHER_DEPTH + 1):
            slot_wait(lax.rem(i + extra, n_slots))


def _combine_ln(ys, dest, x32, gate, g, b, *, bt):
    t, d = x32.shape
    grid_spec = pltpu.PrefetchScalarGridSpec(
        num_scalar_prefetch=1,
        grid=(t // bt,),
        in_specs=[pl.BlockSpec(memory_space=pl.ANY),
                  pl.BlockSpec((bt, d), lambda i, dr: (i, 0)),
                  pl.BlockSpec((bt, LANES), lambda i, dr: (i, 0)),
                  pl.BlockSpec((1, d), lambda i, dr: (0, 0)),
                  pl.BlockSpec((1, d), lambda i, dr: (0, 0))],
        out_specs=[pl.BlockSpec((bt, d), lambda i, dr: (i, 0)),
                   pl.BlockSpec((bt, d), lambda i, dr: (i, 0))],
        scratch_shapes=[pltpu.VMEM((GATHER_DEPTH + 1, TOP_K, bt, d), F32),
                        pltpu.SemaphoreType.DMA((GATHER_DEPTH + 1,))],
    )
    return pl.pallas_call(
        functools.partial(_combine_kernel, bt=bt),
        grid_spec=grid_spec,
        out_shape=[jax.ShapeDtypeStruct((t, d), F32), jax.ShapeDtypeStruct((t, d), BF16)],
        compiler_params=_cparams(("arbitrary",)),
        name="moe_combine_ln",
    )(dest, ys, x32, gate, g.astype(F32).reshape(1, d), b.astype(F32).reshape(1, d))


def _moe_block(x32, x16, w_router, b_router, w_glu16, b_glu, w_lin16, b_lin, w_down16, b_down, ln_g, ln_b, layer,
               *, blk, bt, router_bm):
    top_e, gate = _router(x16, w_router, b_router, bm=router_bm)
    row_tok, dest, blk_expert, n_act = _dispatch_plan(top_e, w_glu16.shape[1], blk, router_bm)
    ys = _expert_ffn(x32, row_tok, blk_expert, n_act, w_glu16, b_glu, w_lin16, b_lin, w_down16, b_down, layer,
                     blk=blk)
    return _combine_ln(ys, dest, x32, gate, ln_g, ln_b, bt=bt)


def kernel(x, w_in, nat_rpb, nat_norm_g, diff_lambda, diff_subln_g, t5_table, ml_conv_w, ml_gate_b, ml_norm_g,
           w_out, ln_mix_g, ln_mix_b, router_w, router_b, exp_w_glu, exp_b_glu, exp_w_lin, exp_b_lin,
           exp_w_down, exp_b_down, ln_ffn_g, ln_ffn_b):
    bsz, t, d = x.shape
    assert bsz == 1
    x32 = x.reshape(t, d)
    x16 = x32.astype(BF16)
    diff_bq, diff_bk = 1024, 1024
    bias_tiles, far = _diff_bias_tiles(t5_table, diff_bq, diff_bk)
    ml_col0 = (3 * NAT_WIDTH + 3 * DIFF_WIDTH) // MLSTM_WIDTH
    w_glu16, w_lin16, w_down16 = exp_w_glu.astype(BF16), exp_w_lin.astype(BF16), exp_w_down.astype(BF16)
    for l in range(DEPTH):
        w_main = w_in[l][:, :MAIN_WIDTH].astype(BF16)
        w_gate = jnp.pad(w_in[l][:, MAIN_WIDTH:], ((0, 0), (0, LANES - 4 * MLSTM_HEADS))).astype(BF16)
        u = _matmul(x16, w_main, bm=1024, bn=1024, out_dtype=BF16, name="inproj")
        gates = _matmul(x16, w_gate, bm=1024, bn=LANES, out_dtype=F32, name="inproj_gates")
        nat = _nat_attention(u, nat_rpb[l], nat_norm_g[l], blocks_per_step=16)
        lam_init = 0.8 - 0.6 * math.exp(-0.3 * l)
        dif = _diff_attention(u, bias_tiles, far, diff_lambda[l], diff_subln_g[l], lam_init,
                              bq=diff_bq, bk=diff_bk, rb=128, col0=3 * NAT_WIDTH // HEAD_DIM)
        ml = _mlstm_block(u, gates, ml_conv_w[l], ml_gate_b[l], ml_norm_g[l], lc=256, col0=ml_col0)
        x1_32, x1_16 = _proj_residual_ln([nat, dif, ml], w_out[l].astype(BF16), x32, ln_mix_g[l], ln_mix_b[l],
                                         bm=512, bn=512)
        x32, x16 = _moe_block(x1_32, x1_16, router_w[l], router_b[l], w_glu16, exp_b_glu, w_lin16, exp_b_lin,
                              w_down16, exp_b_down, ln_ffn_g[l], ln_ffn_b[l], l,
                              blk=256, bt=128, router_bm=512)
    return x32.reshape(bsz, t, d)
```

```python
import functools
import math

import numpy as np
import jax
import jax.numpy as jnp
from jax import lax
from jax.experimental import pallas as pl
from jax.experimental.pallas import tpu as pltpu

DEPTH = 2
HEAD_DIM = 128
NAT_HEADS = 16
DIFF_HEADS = 8
MLSTM_HEADS = 8
NAT_WIDTH = NAT_HEADS * HEAD_DIM
DIFF_WIDTH = DIFF_HEADS * HEAD_DIM
MLSTM_WIDTH = MLSTM_HEADS * HEAD_DIM
DIFF_QK_DIM = HEAD_DIM // 2
MAIN_WIDTH = 3 * NAT_WIDTH + 3 * DIFF_WIDTH + 4 * MLSTM_WIDTH
GRID_W = 64
NAT_WIN_ROWS = 8
NAT_WIN_COLS = 16
NAT_Q_ROWS = 2
NAT_KEY_ROWS = 10
REL_BUCKETS = 32
REL_MAX_DIST = 128
MLSTM_CHUNK = 64
MLSTM_CONV = 5
N_EXPERTS = 32
TOP_K = 4
SWIGLU_ALPHA = 1.702
SWIGLU_LIMIT = 7.0
MOE_BLOCK = 512
LN_EPS = 1e-5
ALPHA = (2.0 * DEPTH) ** 0.25
NEG_BIG = -1e30

LANES = 128
VMEM_LIMIT = 56 * 1024 * 1024
GATHER_DEPTH = 2
FAR_UNROLL = 4

BF16 = jnp.bfloat16
F32 = jnp.float32


def _cparams(sem):
    return pltpu.CompilerParams(dimension_semantics=sem, vmem_limit_bytes=VMEM_LIMIT)


def _mm_kernel(x_ref, w_ref, o_ref):
    o_ref[...] = jnp.dot(x_ref[...], w_ref[...], preferred_element_type=F32).astype(o_ref.dtype)


def _matmul(x, w, *, bm, bn, out_dtype, name):
    m, k = x.shape
    n = w.shape[1]
    return pl.pallas_call(
        _mm_kernel,
        grid=(m // bm, n // bn),
        in_specs=[pl.BlockSpec((bm, k), lambda i, j: (i, 0)),
                  pl.BlockSpec((k, bn), lambda i, j: (0, j))],
        out_specs=pl.BlockSpec((bm, bn), lambda i, j: (i, j)),
        out_shape=jax.ShapeDtypeStruct((m, n), out_dtype),
        compiler_params=_cparams(("parallel", "parallel")),
        name=name,
    )(x, w)


def _proj_ln_kernel(*refs, n_parts, widths, bn, nj):
    part_refs = refs[:n_parts]
    w_ref, x_ref, g_ref, b_ref, o32_ref, o16_ref, mix_ref = refs[n_parts:]
    j = pl.program_id(1)

    @pl.when(j == 0)
    def _():
        off = 0
        for p_ref, wd in zip(part_refs, widths):
            mix_ref[:, off:off + wd] = p_ref[...]
            off += wd

    z = ALPHA * x_ref[...] + jnp.dot(mix_ref[...], w_ref[...], preferred_element_type=F32)
    for jj in range(nj):
        @pl.when(j == jj)
        def _(jj=jj):
            o32_ref[:, jj * bn:(jj + 1) * bn] = z

    @pl.when(j == nj - 1)
    def _():
        zz = o32_ref[...]
        mu = jnp.mean(zz, axis=-1, keepdims=True)
        zc = zz - mu
        var = jnp.mean(zc * zc, axis=-1, keepdims=True)
        y = zc * lax.rsqrt(var + LN_EPS) * g_ref[...] + b_ref[...]
        o32_ref[...] = y
        o16_ref[...] = y.astype(BF16)


def _proj_residual_ln(parts, w, x_res, g, b, *, bm, bn):
    t, d = x_res.shape
    widths = tuple(p.shape[1] for p in parts)
    kdim = sum(widths)
    nj = d // bn
    kern = functools.partial(_proj_ln_kernel, n_parts=len(parts), widths=widths, bn=bn, nj=nj)
    in_specs = [pl.BlockSpec((bm, wd), lambda i, j: (i, 0)) for wd in widths]
    in_specs += [pl.BlockSpec((kdim, bn), lambda i, j: (0, j)),
                 pl.BlockSpec((bm, bn), lambda i, j: (i, j)),
                 pl.BlockSpec((1, d), lambda i, j: (0, 0)),
                 pl.BlockSpec((1, d), lambda i, j: (0, 0))]
    return pl.pallas_call(
        kern,
        grid=(t // bm, nj),
        in_specs=in_specs,
        out_specs=[pl.BlockSpec((bm, d), lambda i, j: (i, 0)),
                   pl.BlockSpec((bm, d), lambda i, j: (i, 0))],
        out_shape=[jax.ShapeDtypeStruct((t, d), F32), jax.ShapeDtypeStruct((t, d), BF16)],
        scratch_shapes=[pltpu.VMEM((bm, kdim), BF16)],
        compiler_params=_cparams(("parallel", "arbitrary")),
        name="proj_ln",
    )(*parts, w, x_res, g.reshape(1, d), b.reshape(1, d))


def _nat_geometry(t):
    rows = t // GRID_W
    win_r = min(NAT_WIN_ROWS, rows)
    kr = NAT_KEY_ROWS
    assert rows >= kr and rows % NAT_Q_ROWS == 0
    n_blk = rows // NAT_Q_ROWS
    col = np.arange(GRID_W)
    c0 = np.clip(col - NAT_WIN_COLS // 2, 0, GRID_W - NAT_WIN_COLS)
    keys, ks_list, ty_list, tables = {}, [], [], []
    for i in range(n_blk):
        qr = i * NAT_Q_ROWS + np.arange(NAT_Q_ROWS)
        r0 = np.clip(qr - win_r // 2, 0, rows - win_r)
        ks = int(np.clip(r0[0], 0, rows - kr))
        key = (tuple(qr - ks), tuple(r0 - ks))
        if key not in keys:
            keys[key] = len(tables)
            q_row = np.repeat(qr, GRID_W)
            q_r0 = np.repeat(r0, GRID_W)
            q_col = np.tile(col, NAT_Q_ROWS)
            q_c0 = np.tile(c0, NAT_Q_ROWS)
            k_row = np.repeat(ks + np.arange(kr), GRID_W)
            k_col = np.tile(col, kr)
            row_ok = (k_row[None, :] >= q_r0[:, None]) & (k_row[None, :] < q_r0[:, None] + win_r)
            col_ok = (k_col[None, :] >= q_c0[:, None]) & (k_col[None, :] < q_c0[:, None] + NAT_WIN_COLS)
            dr = np.clip(k_row[None, :] - q_row[:, None] + NAT_WIN_ROWS - 1, 0, 2 * NAT_WIN_ROWS - 2)
            dc = np.clip(k_col[None, :] - q_col[:, None] + NAT_WIN_COLS - 1, 0, 2 * NAT_WIN_COLS - 2)
            tables.append((dr, dc, row_ok & col_ok))
        ks_list.append(ks)
        ty_list.append(keys[key])
    dr = np.stack([tb[0] for tb in tables]).astype(np.int32)
    dc = np.stack([tb[1] for tb in tables]).astype(np.int32)
    ok = np.stack([tb[2] for tb in tables])
    return np.asarray(ks_list, np.int32), np.asarray(ty_list, np.int32), dr, dc, ok


def _nat_bias_kernel(rp_ref, ok_ref, o_ref, *, dr_small):
    lane = lax.broadcasted_iota(jnp.int32, (GRID_W, LANES), 1)
    for ty, dr_ty in enumerate(dr_small):
        for qrl, dr_row in enumerate(dr_ty):
            for pair in range(len(dr_row) // 2):
                da, db = int(dr_row[2 * pair]), int(dr_row[2 * pair + 1])
                ya = _toeplitz_rows(rp_ref[0, da:da + 1, :], GRID_W, GRID_W - 1)
                yb = _toeplitz_rows(rp_ref[0, db:db + 1, :], GRID_W, LANES - 1)
                rows = slice(qrl * GRID_W, (qrl + 1) * GRID_W)
                cols = slice(pair * LANES, (pair + 1) * LANES)
                o_ref[ty, 0, rows, cols] = jnp.where(ok_ref[ty, rows, cols] > 0.5,
                                                     jnp.where(lane < GRID_W, ya, yb), NEG_BIG)


def _nat_bias_table(rpb, dr_small, ok):
    n_type, nq, kw = ok.shape
    assert 2 * GRID_W == LANES and NAT_KEY_ROWS % 2 == 0
    pad_c = GRID_W - NAT_WIN_COLS
    n_dr = rpb.shape[1]
    rp = jnp.pad(rpb.astype(F32), ((0, 0), (0, -n_dr % 8), (pad_c, LANES - rpb.shape[2] - pad_c)))
    return pl.pallas_call(
        functools.partial(_nat_bias_kernel, dr_small=dr_small.tolist()),
        grid=(NAT_HEADS,),
        in_specs=[pl.BlockSpec((1, rp.shape[1], LANES), lambda h: (h, 0, 0)),
                  pl.BlockSpec((n_type, nq, kw), lambda h: (0, 0, 0))],
        out_specs=pl.BlockSpec((n_type, 1, nq, kw), lambda h: (0, h, 0, 0)),
        out_shape=jax.ShapeDtypeStruct((n_type, NAT_HEADS, nq, kw), F32),
        compiler_params=_cparams(("parallel",)),
        name="nat_bias_table",
    )(rp, jnp.asarray(ok.astype(np.float32)))


def _nat_kernel(ks_ref, ty_ref, q_ref, k_ref, v_ref, bias_ref, g_ref, o_ref, *, blocks_per_step, kw):
    nq = NAT_Q_ROWS * GRID_W
    base = pl.program_id(1) * blocks_per_step
    scale = HEAD_DIM ** -0.5
    blocks = range(blocks_per_step)
    k0s = [pl.multiple_of(ks_ref[base + jj] * GRID_W, GRID_W) for jj in blocks]
    ss = [lax.dot_general(q_ref[jj * nq:(jj + 1) * nq, :], k_ref[pl.ds(k0s[jj], kw), :],
                          (((1,), (1,)), ((), ())), preferred_element_type=F32) * scale
          + bias_ref[ty_ref[base + jj], 0] for jj in blocks]
    ps = [jnp.exp(s - jnp.max(s, axis=-1, keepdims=True)) for s in ss]
    ls = [jnp.sum(p, axis=-1, keepdims=True) for p in ps]
    os = [jnp.dot(ps[jj].astype(BF16), v_ref[pl.ds(k0s[jj], kw), :], preferred_element_type=F32) / ls[jj]
          for jj in blocks]
    for jj in blocks:
        o = os[jj]
        ms = jnp.mean(o * o, axis=-1, keepdims=True)
        o_ref[jj * nq:(jj + 1) * nq, :] = (o * lax.rsqrt(ms + LN_EPS) * g_ref[0]).astype(o_ref.dtype)


def _nat_attention(u, rpb, norm_g, *, blocks_per_step):
    t = u.shape[0]
    ks, ty, dr, dc, ok = _nat_geometry(t)
    n_blk = ks.shape[0]
    kw = NAT_KEY_ROWS * GRID_W
    nq = NAT_Q_ROWS * GRID_W
    n_type = ok.shape[0]
    bias = _nat_bias_table(rpb, dr[:, ::GRID_W, ::GRID_W], ok)
    g = norm_g.astype(F32).reshape(NAT_HEADS, 1, HEAD_DIM)
    rows_per_step = blocks_per_step * nq
    kern = functools.partial(_nat_kernel, blocks_per_step=blocks_per_step, kw=kw)
    grid_spec = pltpu.PrefetchScalarGridSpec(
        num_scalar_prefetch=2,
        grid=(NAT_HEADS, n_blk // blocks_per_step),
        in_specs=[pl.BlockSpec((rows_per_step, HEAD_DIM), lambda h, g_, *_: (g_, h)),
                  pl.BlockSpec((t, HEAD_DIM), lambda h, g_, *_: (0, NAT_HEADS + h)),
                  pl.BlockSpec((t, HEAD_DIM), lambda h, g_, *_: (0, 2 * NAT_HEADS + h)),
                  pl.BlockSpec((n_type, 1, nq, kw), lambda h, g_, *_: (0, h, 0, 0)),
                  pl.BlockSpec((1, 1, HEAD_DIM), lambda h, g_, *_: (h, 0, 0))],
        out_specs=pl.BlockSpec((rows_per_step, HEAD_DIM), lambda h, g_, *_: (g_, h)),
    )
    return pl.pallas_call(
        kern,
        grid_spec=grid_spec,
        out_shape=jax.ShapeDtypeStruct((t, NAT_WIDTH), BF16),
        compiler_params=_cparams(("parallel", "arbitrary")),
        name="nat_attn",
    )(jnp.asarray(ks), jnp.asarray(ty), u, u, u, bias, g)


def _t5_bucket(rel):
    half = REL_BUCKETS // 2
    max_exact = half // 2
    ret = jnp.where(rel > 0, half, 0)
    n = jnp.abs(rel)
    n_f = jnp.maximum(n, 1).astype(jnp.float32)
    large = max_exact + (jnp.log(n_f / max_exact) / math.log(REL_MAX_DIST / max_exact)
                         * (half - max_exact)).astype(jnp.int32)
    large = jnp.minimum(large, half - 1)
    return ret + jnp.where(n < max_exact, n, large)


def _diff_kernel(far_ref, q_ref, k_ref, v_ref, bias_ref, lp_ref, g_ref, o_ref,
                 qs_ref, m_ref, acc_ref, *, bk, ratio, nk, lam_init, rb):
    h = pl.program_id(0)
    qi = pl.program_id(1)
    bq = q_ref.shape[0]
    n_rb = 2 * bq // rb

    q = q_ref[...]
    lane = lax.broadcasted_iota(jnp.int32, q.shape, 1)
    qsc = q * jnp.asarray(DIFF_QK_DIM ** -0.5, q.dtype)
    zero = jnp.zeros_like(qsc)
    qs_ref[0:bq, :] = jnp.where(lane < DIFF_QK_DIM, qsc, zero)
    qs_ref[bq:2 * bq, :] = jnp.where(lane >= DIFF_QK_DIM, qsc, zero)
    m_ref[...] = jnp.full(m_ref.shape, -jnp.inf, F32)
    acc_ref[...] = jnp.zeros(acc_ref.shape, F32)

    def step(ki, near_idx, const):
        k0 = pl.multiple_of(ki * bk, bk)
        k = k_ref[pl.ds(k0, bk), :]
        v = v_ref[pl.ds(k0, bk), :]
        v_ext = jnp.concatenate([v, jnp.ones_like(v)], axis=1)
        for r in range(n_rb):
            rows = slice(r * rb, (r + 1) * rb)
            s = lax.dot_general(qs_ref[rows, :], k, (((1,), (1,)), ((), ())), preferred_element_type=F32)
            if near_idx is not None:
                r0 = (r * rb) % bq
                s = s + bias_ref[0, near_idx, r0:r0 + rb, :]
            mt = jnp.max(s, axis=-1, keepdims=True)
            if const is not None:
                mt = mt + const
            m_prev = m_ref[rows, :]
            m_new = jnp.maximum(m_prev, mt)
            a = jnp.exp(m_prev - m_new)
            shift = m_new if const is None else m_new - const
            p = jnp.exp(s - jnp.tile(shift, (1, bk // LANES)))
            acc_ref[rows, :] = (jnp.tile(a, (1, 2)) * acc_ref[rows, :]
                                + jnp.dot(p.astype(BF16), v_ext, preferred_element_type=F32))
            m_ref[rows, :] = m_new

    c_left = far_ref[h, 0]
    c_right = far_ref[h, 1]
    first_near = qi * ratio - 1
    n_near = ratio + 2

    def far_tiles(start, stop, const):
        count = stop - start

        def group_body(it, carry):
            for j in range(FAR_UNROLL):
                step(start + FAR_UNROLL * it + j, None, const)
            return carry

        lax.fori_loop(0, count // FAR_UNROLL, group_body, 0)
        done = start + (count // FAR_UNROLL) * FAR_UNROLL
        size = FAR_UNROLL // 2
        while size >= 1:
            @pl.when(lax.rem(count, 2 * size) >= size)
            def _(done=done, size=size):
                for j in range(size):
                    step(done + j, None, const)
            done = done + jnp.where(lax.rem(count, 2 * size) >= size, size, 0)
            size //= 2

    far_tiles(0, jnp.clip(first_near, 0, nk), c_left)
    for tt in range(n_near):
        ki = first_near + tt

        @pl.when((ki >= 0) & (ki < nk))
        def _(ki=ki, tt=tt):
            step(ki, tt, None)

    far_tiles(jnp.clip(first_near + n_near, 0, nk), nk, c_right)

    acc = acc_ref[...]
    o = acc[:, 0:HEAD_DIM] / acc[:, HEAD_DIM:2 * HEAD_DIM]
    lp = lp_ref[...]
    lam = (jnp.exp(jnp.sum(lp[0:1] * lp[1:2], axis=-1, keepdims=True))
           - jnp.exp(jnp.sum(lp[2:3] * lp[3:4], axis=-1, keepdims=True)) + lam_init)
    d = o[0:bq] - lam * o[bq:2 * bq]
    ms = jnp.mean(d * d, axis=-1, keepdims=True)
    o_ref[...] = (d * lax.rsqrt(ms + LN_EPS) * g_ref[...] * (1.0 - lam_init)).astype(o_ref.dtype)


def _diff_bias_tiles(t5_table, bq, bk):
    ratio = bq // bk
    span = bq + bk + 1
    rel = jnp.arange(-span, span + 1, dtype=jnp.int32)
    vec = jnp.take(t5_table.astype(F32), _t5_bucket(rel), axis=0)
    vec_t = vec.T
    diags = jnp.stack([vec_t[:, (tt - 1) * bk + span - (bq - 1):(tt - 1) * bk + span + bk]
                       for tt in range(ratio + 2)], axis=1)
    tiles = _toeplitz(diags, bq, bk)
    far = jnp.stack([vec[0], vec[-1]], axis=1)
    return tiles, far


def _toeplitz_rows(w_row, n_rows, origin):
    width = w_row.shape[1]
    x = jnp.broadcast_to(w_row, (n_rows, width))
    return pltpu.roll(x, (width - origin) % width, 1, stride=1, stride_axis=0)


def _toeplitz_kernel(d_ref, o_ref):
    bq, bk = o_ref.shape[2], o_ref.shape[3]
    o_ref[0, 0] = _toeplitz_rows(d_ref[0, 0], bq, bq - 1)[:, :bk]


def _toeplitz(w, n_rows, n_cols):
    na, nb, ln = w.shape
    width = -(-ln // LANES) * LANES
    wp = jnp.pad(w, ((0, 0), (0, 0), (0, width - ln))).reshape(na, nb, 1, width)
    return pl.pallas_call(
        _toeplitz_kernel,
        grid=(na, nb),
        in_specs=[pl.BlockSpec((1, 1, 1, width), lambda a, b: (a, b, 0, 0))],
        out_specs=pl.BlockSpec((1, 1, n_rows, n_cols), lambda a, b: (a, b, 0, 0)),
        out_shape=jax.ShapeDtypeStruct((na, nb, n_rows, n_cols), F32),
        compiler_params=_cparams(("parallel", "parallel")),
        name="toeplitz_tiles",
    )(wp)


def _diff_attention(u, bias_tiles, far, lam_params, subln_g, lam_init, *, bq, bk, rb, col0):
    t = u.shape[0]
    assert bk >= REL_MAX_DIST and bq % bk == 0
    ratio = bq // bk
    nk = t // bk
    n_near = ratio + 2
    assert bq % rb == 0
    kern = functools.partial(_diff_kernel, bk=bk, ratio=ratio, nk=nk, lam_init=lam_init, rb=rb)
    grid_spec = pltpu.PrefetchScalarGridSpec(
        num_scalar_prefetch=1,
        grid=(DIFF_HEADS, t // bq),
        in_specs=[pl.BlockSpec((bq, HEAD_DIM), lambda h, i, *_: (i, col0 + h)),
                  pl.BlockSpec((t, HEAD_DIM), lambda h, i, *_: (0, col0 + DIFF_HEADS + h)),
                  pl.BlockSpec((t, HEAD_DIM), lambda h, i, *_: (0, col0 + 2 * DIFF_HEADS + h)),
                  pl.BlockSpec((1, n_near, bq, bk), lambda h, i, *_: (h, 0, 0, 0)),
                  pl.BlockSpec((4, DIFF_QK_DIM), lambda h, i, *_: (0, 0)),
                  pl.BlockSpec((1, HEAD_DIM), lambda h, i, *_: (0, 0))],
        out_specs=pl.BlockSpec((bq, HEAD_DIM), lambda h, i, *_: (i, h)),
        scratch_shapes=[pltpu.VMEM((2 * bq, HEAD_DIM), BF16),
                        pltpu.VMEM((2 * bq, LANES), F32),
                        pltpu.VMEM((2 * bq, 2 * HEAD_DIM), F32)],
    )
    return pl.pallas_call(
        kern,
        grid_spec=grid_spec,
        out_shape=jax.ShapeDtypeStruct((t, DIFF_WIDTH), BF16),
        compiler_params=_cparams(("parallel", "arbitrary")),
        name="diff_attn",
    )(far, u, u, u, bias_tiles, lam_params.astype(F32), subln_g.astype(F32).reshape(1, HEAD_DIM))


CONV_HALO = 16


def _conv_silu_kernel(prev_ref, cur_ref, next_ref, w_ref, o_ref, *, k_scale):
    i = pl.program_id(0)
    j = pl.program_id(1)
    bt = cur_ref.shape[0]
    x = cur_ref[...].astype(F32)
    p = jnp.where(i > 0, prev_ref[...].astype(F32), 0.0)
    nx = jnp.where(i < pl.num_programs(0) - 1, next_ref[...].astype(F32), 0.0)
    ext = jnp.concatenate([p, x, nx], axis=0)
    n_ext = bt + 2 * CONV_HALO
    half = MLSTM_CONV // 2
    acc = w_ref[half:half + 1, :] * x
    for tap in range(MLSTM_CONV):
        s = tap - half
        if s == 0:
            continue
        rolled = pltpu.roll(ext, (n_ext - s) % n_ext, 0)
        acc = acc + w_ref[tap:tap + 1, :] * rolled[CONV_HALO:CONV_HALO + bt]
    y = acc * jax.nn.sigmoid(acc)
    o_ref[...] = (y * jnp.where(j == 1, k_scale, 1.0)).astype(o_ref.dtype)


def _conv_silu(u, conv_w, *, bt, col0):
    t = u.shape[0]
    w = MLSTM_WIDTH
    nb = t // bt
    hb = bt // CONV_HALO
    last = t // CONV_HALO - 1
    return pl.pallas_call(
        functools.partial(_conv_silu_kernel, k_scale=HEAD_DIM ** -0.5),
        grid=(nb, 2),
        in_specs=[pl.BlockSpec((CONV_HALO, w), lambda i, j: (jnp.maximum(i * hb - 1, 0), col0 + j)),
                  pl.BlockSpec((bt, w), lambda i, j: (i, col0 + j)),
                  pl.BlockSpec((CONV_HALO, w), lambda i, j: (jnp.minimum((i + 1) * hb, last), col0 + j)),
                  pl.BlockSpec((MLSTM_CONV, w), lambda i, j: (0, j))],
        out_specs=pl.BlockSpec((bt, w), lambda i, j: (i, j)),
        out_shape=jax.ShapeDtypeStruct((t, 2 * w), BF16),
        compiler_params=_cparams(("parallel", "parallel")),
        name="mlstm_conv",
    )(u, u, u, conv_w.astype(F32))


def _split3(x):
    x1 = x.astype(BF16)
    r1 = x - x1.astype(F32)
    x2 = r1.astype(BF16)
    x3 = (r1 - x2.astype(F32)).astype(BF16)
    return x1, x2, x3


def _log_sigmoid(x):
    return jnp.minimum(x, 0.0) - jnp.log(1.0 + jnp.exp(-jnp.abs(x)))


def _mlstm_units(units, c_ref, n_ref, m_ref, lc):
    nt_dims = (((1,), (1,)), ((), ()))
    tn_dims = (((0,), (0,)), ((), ()))
    rd = lambda u, name: u[name][:, u["sl"]]
    dmats = [jnp.where(u["mask"], u["bcol"] - u["brow"] + u["igrow"], -jnp.inf) for u in units]
    m_prevs = [m_ref[u["idx"]] for u in units]
    inters = [u["bcol"] + mp for u, mp in zip(units, m_prevs)]
    m_ts = [jnp.maximum(it, jnp.max(d, axis=-1, keepdims=True)) for it, d in zip(inters, dmats)]
    w_inters = [jnp.exp(it - mt) for it, mt in zip(inters, m_ts)]
    ss = [lax.dot_general(rd(u, "q_ref"), rd(u, "k_ref"), nt_dims, preferred_element_type=F32) for u in units]
    ps = [jnp.exp(d - mt) * s for d, mt, s in zip(dmats, m_ts, ss)]
    nums = [wi * jnp.dot(rd(u, "q_ref"), c_ref[u["idx"]].astype(BF16), preferred_element_type=F32)
            + jnp.dot(p.astype(BF16), rd(u, "v_ref"), preferred_element_type=F32)
            for u, wi, p in zip(units, w_inters, ps)]
    dens = [wi * jnp.sum(rd(u, "q_ref").astype(F32) * n_ref[u["idx"]], axis=-1, keepdims=True)
            + jnp.sum(p, axis=-1, keepdims=True) for u, wi, p in zip(units, w_inters, ps)]
    for u, num, den, mt in zip(units, nums, dens, m_ts):
        u["h_ref"][:, u["sl"]] = num / jnp.maximum(jnp.abs(den), jnp.exp(-mt))
    gs = [u["bcol"][lc - 1:lc] if u["fwd"] else u["bcol"][0:1] for u in units]
    a_s = [g - u["bcol"] + u["igcol"] for u, g in zip(units, gs)]
    m_locs = [jnp.max(a, axis=0, keepdims=True) for a in a_s]
    kws = [rd(u, "k_ref").astype(F32) * jnp.exp(a - ml) for u, a, ml in zip(units, a_s, m_locs)]
    c_locs = [lax.dot_general(kw.astype(BF16), rd(u, "v_ref"), tn_dims, preferred_element_type=F32)
              for u, kw in zip(units, kws)]
    for u, g, mp, ml, kw, c_loc in zip(units, gs, m_prevs, m_locs, kws, c_locs):
        m_new = jnp.maximum(g + mp, ml)
        s_old = jnp.exp(g + mp - m_new)
        s_loc = jnp.exp(ml - m_new)
        c_ref[u["idx"]] = s_old * c_ref[u["idx"]] + s_loc * c_loc
        n_ref[u["idx"]] = s_old * n_ref[u["idx"]] + s_loc * jnp.sum(kw, axis=0, keepdims=True)
        m_ref[u["idx"]] = m_new


def _mlstm_kernel(qf_ref, kf_ref, vf_ref, gf_ref, qb_ref, kb_ref, vb_ref, gb_ref, gbias_ref,
                  hf_ref, hb_ref, c_ref, n_ref, m_ref, *, lc):
    nh = MLSTM_HEADS

    @pl.when(pl.program_id(0) == 0)
    def _():
        c_ref[...] = jnp.zeros(c_ref.shape, F32)
        n_ref[...] = jnp.zeros(n_ref.shape, F32)
        m_ref[...] = jnp.zeros(m_ref.shape, F32)

    t_io = lax.broadcasted_iota(jnp.int32, (lc, lc), 0)
    s_io = lax.broadcasted_iota(jnp.int32, (lc, lc), 1)
    units = []
    for fwd in (True, False):
        g_ref, q_ref, k_ref, v_ref, h_ref = ((gf_ref, qf_ref, kf_ref, vf_ref, hf_ref) if fwd
                                             else (gb_ref, qb_ref, kb_ref, vb_ref, hb_ref))
        gates = g_ref[...] + gbias_ref[...]
        mask = (s_io <= t_io) if fwd else (s_io >= t_io)
        tri = jnp.where(mask, 1.0, 0.0).astype(BF16)
        cum = None
        for piece in _split3(_log_sigmoid(gates)):
            part = jnp.dot(tri, piece, preferred_element_type=F32)
            cum = part if cum is None else cum + part
        gates_t = jnp.transpose(gates)
        cum_t = jnp.transpose(cum)
        lane0 = 0 if fwd else 2 * nh
        for hh in range(nh):
            li, lb = lane0 + hh, lane0 + nh + hh
            units.append(dict(fwd=fwd, idx=(0 if fwd else nh) + hh, sl=slice(hh * HEAD_DIM, (hh + 1) * HEAD_DIM),
                              q_ref=q_ref, k_ref=k_ref, v_ref=v_ref, h_ref=h_ref, mask=mask,
                              bcol=cum[:, lb:lb + 1], igcol=gates[:, li:li + 1],
                              brow=cum_t[lb:lb + 1, :], igrow=gates_t[li:li + 1, :]))
    _mlstm_units(units, c_ref, n_ref, m_ref, lc)


def _mlstm_scan(qk, u, gates, gate_b, *, lc, v_col):
    t = qk.shape[0]
    w = MLSTM_WIDTH
    nc = t // lc
    gbias = jnp.pad(gate_b.astype(F32), (0, LANES - gate_b.shape[0])).reshape(1, LANES)
    fmap = lambda col: (lambda c: (c, col))
    bmap = lambda col: (lambda c: (nc - 1 - c, col))
    nhd = 2 * MLSTM_HEADS
    return pl.pallas_call(
        functools.partial(_mlstm_kernel, lc=lc),
        grid=(nc,),
        in_specs=[pl.BlockSpec((lc, w), fmap(0)), pl.BlockSpec((lc, w), fmap(1)),
                  pl.BlockSpec((lc, w), fmap(v_col)), pl.BlockSpec((lc, LANES), fmap(0)),
                  pl.BlockSpec((lc, w), bmap(0)), pl.BlockSpec((lc, w), bmap(1)),
                  pl.BlockSpec((lc, w), bmap(v_col)), pl.BlockSpec((lc, LANES), bmap(0)),
                  pl.BlockSpec((1, LANES), lambda c: (0, 0))],
        out_specs=[pl.BlockSpec((lc, w), fmap(0)), pl.BlockSpec((lc, w), bmap(0))],
        out_shape=[jax.ShapeDtypeStruct((t, w), F32), jax.ShapeDtypeStruct((t, w), F32)],
        scratch_shapes=[pltpu.VMEM((nhd, HEAD_DIM, HEAD_DIM), F32),
                        pltpu.VMEM((nhd, 1, HEAD_DIM), F32),
                        pltpu.VMEM((nhd, 1, 1), F32)],
        compiler_params=_cparams(("arbitrary",)),
        name="mlstm_scan",
    )(qk, qk, u, gates, qk, qk, u, gates, gbias)


def _mlstm_out_kernel(hf_ref, hb_ref, o_ref, g_ref, out_ref):
    for hh in range(MLSTM_HEADS):
        sl = slice(hh * HEAD_DIM, (hh + 1) * HEAD_DIM)
        x = (hf_ref[:, sl] + hb_ref[:, sl]) * jax.nn.sigmoid(o_ref[:, sl].astype(F32))
        xc = x - jnp.mean(x, axis=-1, keepdims=True)
        y = xc * lax.rsqrt(jnp.mean(xc * xc, axis=-1, keepdims=True) + LN_EPS) * g_ref[:, sl]
        out_ref[:, sl] = y.astype(out_ref.dtype)


def _mlstm_out(hf, hb, u, norm_g, *, bt, o_col):
    t, w = hf.shape
    return pl.pallas_call(
        _mlstm_out_kernel,
        grid=(t // bt,),
        in_specs=[pl.BlockSpec((bt, w), lambda i: (i, 0)), pl.BlockSpec((bt, w), lambda i: (i, 0)),
                  pl.BlockSpec((bt, w), lambda i: (i, o_col)), pl.BlockSpec((1, w), lambda i: (0, 0))],
        out_specs=pl.BlockSpec((bt, w), lambda i: (i, 0)),
        out_shape=jax.ShapeDtypeStruct((t, w), BF16),
        compiler_params=_cparams(("parallel",)),
        name="mlstm_out",
    )(hf, hb, u, norm_g.astype(F32).reshape(1, w))


def _mlstm_block(u, gates, conv_w, gate_b, norm_g, *, lc, col0):
    qk = _conv_silu(u, conv_w, bt=512, col0=col0)
    hf, hb = _mlstm_scan(qk, u, gates, gate_b, lc=lc, v_col=col0 + 2)
    return _mlstm_out(hf, hb, u, norm_g, bt=512, o_col=col0 + 3)


def _router_kernel(x_ref, w_ref, b_ref, e_ref, g_ref):
    logits = jnp.dot(x_ref[...], w_ref[...], preferred_element_type=F32) + b_ref[...]
    lane = lax.broadcasted_iota(jnp.int32, logits.shape, 1)
    work = logits
    e_out = jnp.zeros(logits.shape, jnp.int32)
    vals = []
    for kk in range(TOP_K):
        m = jnp.max(work, axis=-1, keepdims=True)
        idx = jnp.min(jnp.where(work == m, lane, LANES), axis=-1, keepdims=True)
        e_out = jnp.where(lane == kk, idx, e_out)
        vals.append(m)
        work = jnp.where(lane == idx, -jnp.inf, work)
    ex = [jnp.exp(v - vals[0]) for v in vals]
    den = ex[0]
    for e in ex[1:]:
        den = den + e
    g_out = jnp.zeros(logits.shape, F32)
    for kk in range(TOP_K):
        g_out = jnp.where(lane == kk, ex[kk] / den, g_out)
    e_ref[...] = e_out
    g_ref[...] = g_out


def _router(x16, w_router, b_router, *, bm):
    t, d = x16.shape
    n_e = w_router.shape[1]
    w = jnp.pad(w_router, ((0, 0), (0, LANES - n_e))).astype(BF16)
    b = jnp.pad(b_router.astype(F32), (0, LANES - n_e), constant_values=NEG_BIG).reshape(1, LANES)
    return pl.pallas_call(
        _router_kernel,
        grid=(t // bm,),
        in_specs=[pl.BlockSpec((bm, d), lambda i: (i, 0)),
                  pl.BlockSpec((d, LANES), lambda i: (0, 0)),
                  pl.BlockSpec((1, LANES), lambda i: (0, 0))],
        out_specs=[pl.BlockSpec((bm, LANES), lambda i: (i, 0)),
                   pl.BlockSpec((bm, LANES), lambda i: (i, 0))],
        out_shape=[jax.ShapeDtypeStruct((t, LANES), jnp.int32), jax.ShapeDtypeStruct((t, LANES), F32)],
        compiler_params=_cparams(("parallel",)),
        name="router",
    )(x16, w, b)


def _rank_kernel(e_ref, rank_ref, cnt_ref, base_ref):
    @pl.when(pl.program_id(0) == 0)
    def _():
        base_ref[...] = jnp.zeros(base_ref.shape, F32)

    e = e_ref[...]
    bt = e.shape[0]
    lane = lax.broadcasted_iota(jnp.int32, e.shape, 1)
    onehots = [jnp.where(lane == e[:, kk:kk + 1], 1.0, 0.0) for kk in range(TOP_K)]
    tot = onehots[0]
    for oh in onehots[1:]:
        tot = tot + oh
    t_io = lax.broadcasted_iota(jnp.int32, (bt, bt), 0)
    s_io = lax.broadcasted_iota(jnp.int32, (bt, bt), 1)
    earlier = jnp.where(s_io < t_io, 1.0, 0.0).astype(BF16)
    pos = jnp.dot(earlier, tot.astype(BF16), preferred_element_type=F32) + base_ref[...]
    out = jnp.zeros(e.shape, jnp.int32)
    for kk in range(TOP_K):
        rk = jnp.sum(onehots[kk] * pos, axis=-1, keepdims=True)
        out = jnp.where(lane == kk, rk.astype(jnp.int32), out)
    rank_ref[...] = out
    base_ref[...] = base_ref[...] + jnp.sum(tot, axis=0, keepdims=True)
    cnt_ref[...] = base_ref[...]


def _slot_ranks(top_e, *, bt):
    t = top_e.shape[0]
    return pl.pallas_call(
        _rank_kernel,
        grid=(t // bt,),
        in_specs=[pl.BlockSpec((bt, LANES), lambda i: (i, 0))],
        out_specs=[pl.BlockSpec((bt, LANES), lambda i: (i, 0)), pl.BlockSpec((1, LANES), lambda i: (0, 0))],
        out_shape=[jax.ShapeDtypeStruct((t, LANES), jnp.int32), jax.ShapeDtypeStruct((1, LANES), F32)],
        scratch_shapes=[pltpu.VMEM((1, LANES), F32)],
        compiler_params=_cparams(("arbitrary",)),
        name="moe_rank",
    )(top_e)


def _dispatch_plan(top_e_lanes, n_experts, blk, rank_bt):
    t = top_e_lanes.shape[0]
    n_slots = t * TOP_K
    n_blocks = -(-n_slots // blk) + n_experts
    rank_lanes, counts_f = _slot_ranks(top_e_lanes, bt=rank_bt)
    slot_e = top_e_lanes[:, :TOP_K].reshape(-1)
    rank = rank_lanes[:, :TOP_K].reshape(-1)
    counts = counts_f[0, :n_experts].astype(jnp.int32)
    onehot = (slot_e[:, None] == jnp.arange(n_experts, dtype=jnp.int32)[None, :]).astype(jnp.int32)
    padded = (counts + blk - 1) // blk * blk
    pad_end = jnp.cumsum(padded)
    pad_start = pad_end - padded
    dest = (jnp.sum(pad_start[None, :] * onehot, axis=1) + rank).astype(jnp.int32)
    slot_tok = jnp.arange(n_slots, dtype=jnp.int32) // TOP_K
    row_tok = jnp.zeros((n_blocks * blk,), jnp.int32).at[dest].set(slot_tok)
    n_act = (pad_end[-1] // blk).astype(jnp.int32)
    blk_id = jnp.minimum(jnp.arange(n_blocks, dtype=jnp.int32), n_act - 1)
    blk_expert = jnp.minimum(jnp.sum((pad_end[None, :] <= (blk_id * blk)[:, None]).astype(jnp.int32), axis=1),
                             n_experts - 1)
    return row_tok, dest, blk_expert, n_act.reshape(1)


def _expert_kernel(be_ref, tok_ref, nact_ref, x_hbm, wg_hbm, bg_ref, wl_hbm, bl_ref, wd_hbm, bd_ref, o_ref,
                   xg_ref, wg_buf, wl_buf, wd_buf, sem_ref, wsem_ref, *, blk, layer):
    b = pl.program_id(0)
    n_blocks = pl.num_programs(0)
    n_act = nact_ref[0]
    n_slots = GATHER_DEPTH + 1
    slot = lax.rem(b, n_slots)
    e_cur = be_ref[b]
    e_prev = be_ref[jnp.maximum(b - 1, 0)]
    e_next = be_ref[jnp.minimum(b + 1, n_blocks - 1)]
    fresh = (b == 0) | (e_cur != e_prev)

    def row_copy(blk_idx, r, dst_slot):
        tok = tok_ref[blk_idx * blk + r]
        return pltpu.make_async_copy(x_hbm.at[pl.ds(tok, 1), :], xg_ref.at[dst_slot, pl.ds(r, 1), :],
                                     sem_ref.at[dst_slot])

    def w_copy(which, e):
        src, dst = ((wg_hbm, wg_buf), (wl_hbm, wl_buf), (wd_hbm, wd_buf))[which]
        return pltpu.make_async_copy(src.at[layer, e], dst, wsem_ref.at[which])

    def compute(gather_ahead, has_next):
        pltpu.make_async_copy(xg_ref.at[slot], xg_ref.at[slot], sem_ref.at[slot]).wait()

        @pl.when(fresh)
        def _():
            w_copy(0, e_cur).wait()
            w_copy(1, e_cur).wait()

        if gather_ahead:
            ahead_slot = lax.rem(b + GATHER_DEPTH, n_slots)
            for r in range(blk):
                row_copy(b + GATHER_DEPTH, r, ahead_slot).start()
        xb = xg_ref[slot].astype(BF16)
        hg = jnp.minimum(jnp.dot(xb, wg_buf[...], preferred_element_type=F32) + bg_ref[0, 0], SWIGLU_LIMIT)
        hl = jnp.clip(jnp.dot(xb, wl_buf[...], preferred_element_type=F32) + bl_ref[0, 0],
                      -SWIGLU_LIMIT, SWIGLU_LIMIT)
        hh = (hg * jax.nn.sigmoid(SWIGLU_ALPHA * hg) * (hl + 1.0)).astype(BF16)

        if has_next:
            @pl.when(e_next != e_cur)
            def _():
                w_copy(0, e_next).start()
                w_copy(1, e_next).start()

        @pl.when(fresh)
        def _():
            w_copy(2, e_cur).wait()

        o_ref[...] = jnp.dot(hh, wd_buf[...], preferred_element_type=F32) + bd_ref[0, 0]

        if has_next:
            @pl.when(e_next != e_cur)
            def _():
                w_copy(2, e_next).start()

    @pl.when(b == 0)
    def _():
        for which in range(3):
            w_copy(which, e_cur).start()

        for first in range(GATHER_DEPTH):
            @pl.when(first < n_act)
            def _(first=first):
                def body(r, c):
                    row_copy(first, r, first).start()
                    return c
                lax.fori_loop(0, blk, body, 0, unroll=8)

    @pl.when(b + GATHER_DEPTH < n_act)
    def _():
        compute(True, True)

    @pl.when((b + 1 < n_act) & (b + GATHER_DEPTH >= n_act))
    def _():
        compute(False, True)

    @pl.when(b + 1 == n_act)
    def _():
        compute(False, False)

    @pl.when(b >= n_act)
    def _():
        o_ref[...] = jnp.zeros(o_ref.shape, o_ref.dtype)


def _expert_ffn(x32, row_tok, blk_expert, n_act, w_glu16, b_glu, w_lin16, b_lin, w_down16, b_down, layer, *, blk):
    t, d = x32.shape
    depth, n_e, _, ff = w_glu16.shape
    n_blocks = blk_expert.shape[0]
    bmap = lambda i, be, tok, na: (layer, be[i], 0, 0)
    any_spec = pl.BlockSpec(memory_space=pl.ANY)
    grid_spec = pltpu.PrefetchScalarGridSpec(
        num_scalar_prefetch=3,
        grid=(n_blocks,),
        in_specs=[any_spec,
                  any_spec, pl.BlockSpec((1, 1, 1, ff), bmap),
                  any_spec, pl.BlockSpec((1, 1, 1, ff), bmap),
                  any_spec, pl.BlockSpec((1, 1, 1, d), bmap)],
        out_specs=pl.BlockSpec((blk, d), lambda i, be, tok, na: (i, 0)),
        scratch_shapes=[pltpu.VMEM((GATHER_DEPTH + 1, blk, d), F32),
                        pltpu.VMEM((d, ff), BF16), pltpu.VMEM((d, ff), BF16), pltpu.VMEM((ff, d), BF16),
                        pltpu.SemaphoreType.DMA((GATHER_DEPTH + 1,)), pltpu.SemaphoreType.DMA((3,))],
    )
    return pl.pallas_call(
        functools.partial(_expert_kernel, blk=blk, layer=layer),
        grid_spec=grid_spec,
        out_shape=jax.ShapeDtypeStruct((n_blocks * blk, d), F32),
        compiler_params=_cparams(("arbitrary",)),
        name="expert_ffn",
    )(blk_expert, row_tok, n_act, x32,
      w_glu16, b_glu.astype(F32).reshape(depth, n_e, 1, ff),
      w_lin16, b_lin.astype(F32).reshape(depth, n_e, 1, ff),
      w_down16, b_down.astype(F32).reshape(depth, n_e, 1, d))


def _combine_kernel(dest_ref, ys_hbm, x_ref, gate_ref, g_ref, b_ref, o32_ref, o16_ref, buf_ref, sem_ref, *, bt):
    i = pl.program_id(0)
    n = pl.num_programs(0)

    def row_copy(blk_idx, r, kk, dst_slot):
        row = dest_ref[(blk_idx * bt + r) * TOP_K + kk]
        return pltpu.make_async_copy(ys_hbm.at[pl.ds(row, 1), :], buf_ref.at[dst_slot, kk, pl.ds(r, 1), :],
                                     sem_ref.at[dst_slot])

    def slot_wait(s):
        pltpu.make_async_copy(buf_ref.at[s], buf_ref.at[s], sem_ref.at[s]).wait()

    @pl.when(i == 0)
    def _():
        def body(r, c):
            for blk_idx in range(GATHER_DEPTH):
                for kk in range(TOP_K):
                    row_copy(jnp.minimum(blk_idx, n - 1), r, kk, blk_idx).start()
            return c
        lax.fori_loop(0, bt, body, 0)

    n_slots = GATHER_DEPTH + 1
    slot = lax.rem(i, n_slots)
    slot_wait(slot)
    ahead = jnp.where(i + GATHER_DEPTH < n, i + GATHER_DEPTH, 0)
    ahead_slot = lax.rem(i + GATHER_DEPTH, n_slots)
    for r in range(bt):
        for kk in range(TOP_K):
            row_copy(ahead, r, kk, ahead_slot).start()
    gate = gate_ref[...]
    z = ALPHA * x_ref[...]
    for kk in range(TOP_K):
        z = z + gate[:, kk:kk + 1] * buf_ref[slot, kk]
    mu = jnp.mean(z, axis=-1, keepdims=True)
    zc = z - mu
    var = jnp.mean(zc * zc, axis=-1, keepdims=True)
    y = zc * lax.rsqrt(var + LN_EPS) * g_ref[...] + b_ref[...]
    o32_ref[...] = y
    o16_ref[...] = y.astype(BF16)

    @pl.when(i == n - 1)
    def _():
        for extra in range(1, GATHER_DEPTH + 1):
            slot_wait(lax.rem(i + extra, n_slots))


def _combine_ln(ys, dest, x32, gate, g, b, *, bt):
    t, d = x32.shape
    grid_spec = pltpu.PrefetchScalarGridSpec(
        num_scalar_prefetch=1,
        grid=(t // bt,),
        in_specs=[pl.BlockSpec(memory_space=pl.ANY),
                  pl.BlockSpec((bt, d), lambda i, dr: (i, 0)),
                  pl.BlockSpec((bt, LANES), lambda i, dr: (i, 0)),
                  pl.BlockSpec((1, d), lambda i, dr: (0, 0)),
                  pl.BlockSpec((1, d), lambda i, dr: (0, 0))],
        out_specs=[pl.BlockSpec((bt, d), lambda i, dr: (i, 0)),
                   pl.BlockSpec((bt, d), lambda i, dr: (i, 0))],
        scratch_shapes=[pltpu.VMEM((GATHER_DEPTH + 1, TOP_K, bt, d), F32),
                        pltpu.SemaphoreType.DMA((GATHER_DEPTH + 1,))],
    )
    return pl.pallas_call(
        functools.partial(_combine_kernel, bt=bt),
        grid_spec=grid_spec,
        out_shape=[jax.ShapeDtypeStruct((t, d), F32), jax.ShapeDtypeStruct((t, d), BF16)],
        compiler_params=_cparams(("arbitrary",)),
        name="moe_combine_ln",
    )(dest, ys, x32, gate, g.astype(F32).reshape(1, d), b.astype(F32).reshape(1, d))


def _moe_block(x32, x16, w_router, b_router, w_glu16, b_glu, w_lin16, b_lin, w_down16, b_down, ln_g, ln_b, layer,
               *, blk, bt, router_bm):
    top_e, gate = _router(x16, w_router, b_router, bm=router_bm)
    row_tok, dest, blk_expert, n_act = _dispatch_plan(top_e, w_glu16.shape[1], blk, router_bm)
    ys = _expert_ffn(x32, row_tok, blk_expert, n_act, w_glu16, b_glu, w_lin16, b_lin, w_down16, b_down, layer,
                     blk=blk)
    return _combine_ln(ys, dest, x32, gate, ln_g, ln_b, bt=bt)


def kernel(x, w_in, nat_rpb, nat_norm_g, diff_lambda, diff_subln_g, t5_table, ml_conv_w, ml_gate_b, ml_norm_g,
           w_out, ln_mix_g, ln_mix_b, router_w, router_b, exp_w_glu, exp_b_glu, exp_w_lin, exp_b_lin,
           exp_w_down, exp_b_down, ln_ffn_g, ln_ffn_b):
    bsz, t, d = x.shape
    assert bsz == 1
    x32 = x.reshape(t, d)
    x16 = x32.astype(BF16)
    diff_bq, diff_bk = 1024, 1024
    bias_tiles, far = _diff_bias_tiles(t5_table, diff_bq, diff_bk)
    ml_col0 = (3 * NAT_WIDTH + 3 * DIFF_WIDTH) // MLSTM_WIDTH
    w_glu16, w_lin16, w_down16 = exp_w_glu.astype(BF16), exp_w_lin.astype(BF16), exp_w_down.astype(BF16)
    for l in range(DEPTH):
        w_main = w_in[l][:, :MAIN_WIDTH].astype(BF16)
        w_gate = jnp.pad(w_in[l][:, MAIN_WIDTH:], ((0, 0), (0, LANES - 4 * MLSTM_HEADS))).astype(BF16)
        u = _matmul(x16, w_main, bm=1024, bn=1024, out_dtype=BF16, name="inproj")
        gates = _matmul(x16, w_gate, bm=1024, bn=LANES, out_dtype=F32, name="inproj_gates")
        nat = _nat_attention(u, nat_rpb[l], nat_norm_g[l], blocks_per_step=16)
        lam_init = 0.8 - 0.6 * math.exp(-0.3 * l)
        dif = _diff_attention(u, bias_tiles, far, diff_lambda[l], diff_subln_g[l], lam_init,
                              bq=diff_bq, bk=diff_bk, rb=128, col0=3 * NAT_WIDTH // HEAD_DIM)
        ml = _mlstm_block(u, gates, ml_conv_w[l], ml_gate_b[l], ml_norm_g[l], lc=256, col0=ml_col0)
        x1_32, x1_16 = _proj_residual_ln([nat, dif, ml], w_out[l].astype(BF16), x32, ln_mix_g[l], ln_mix_b[l],
                                         bm=512, bn=512)
        x32, x16 = _moe_block(x1_32, x1_16, router_w[l], router_b[l], w_glu16, exp_b_glu, w_lin16, exp_b_lin,
                              w_down16, exp_b_down, ln_ffn_g[l], ln_ffn_b[l], l,
                              blk=256, bt=128, router_bm=512)
    return x32.reshape(bsz, t, d)
```

```python
import functools
import math

import numpy as np
import jax
import jax.numpy as jnp
from jax import lax
from jax.experimental import pallas as pl
from jax.experimental.pallas import tpu as pltpu

DEPTH = 2
HEAD_DIM = 128
NAT_HEADS = 16
DIFF_HEADS = 8
MLSTM_HEADS = 8
NAT_WIDTH = NAT_HEADS * HEAD_DIM
DIFF_WIDTH = DIFF_HEADS * HEAD_DIM
MLSTM_WIDTH = MLSTM_HEADS * HEAD_DIM
DIFF_QK_DIM = HEAD_DIM // 2
MAIN_WIDTH = 3 * NAT_WIDTH + 3 * DIFF_WIDTH + 4 * MLSTM_WIDTH
GRID_W = 64
NAT_WIN_ROWS = 8
NAT_WIN_COLS = 16
NAT_Q_ROWS = 2
NAT_KEY_ROWS = 10
REL_BUCKETS = 32
REL_MAX_DIST = 128
MLSTM_CHUNK = 64
MLSTM_CONV = 5
N_EXPERTS = 32
TOP_K = 4
SWIGLU_ALPHA = 1.702
SWIGLU_LIMIT = 7.0
MOE_BLOCK = 512
LN_EPS = 1e-5
ALPHA = (2.0 * DEPTH) ** 0.25
NEG_BIG = -1e30

LANES = 128
VMEM_LIMIT = 56 * 1024 * 1024
GATHER_DEPTH = 2

BF16 = jnp.bfloat16
F32 = jnp.float32


def _cparams(sem):
    return pltpu.CompilerParams(dimension_semantics=sem, vmem_limit_bytes=VMEM_LIMIT)


def _mm_kernel(x_ref, w_ref, o_ref):
    o_ref[...] = jnp.dot(x_ref[...], w_ref[...], preferred_element_type=F32).astype(o_ref.dtype)


def _matmul(x, w, *, bm, bn, out_dtype, name):
    m, k = x.shape
    n = w.shape[1]
    return pl.pallas_call(
        _mm_kernel,
        grid=(m // bm, n // bn),
        in_specs=[pl.BlockSpec((bm, k), lambda i, j: (i, 0)),
                  pl.BlockSpec((k, bn), lambda i, j: (0, j))],
        out_specs=pl.BlockSpec((bm, bn), lambda i, j: (i, j)),
        out_shape=jax.ShapeDtypeStruct((m, n), out_dtype),
        compiler_params=_cparams(("parallel", "parallel")),
        name=name,
    )(x, w)


def _proj_ln_kernel(*refs, n_parts, widths, bn, nj):
    part_refs = refs[:n_parts]
    w_ref, x_ref, g_ref, b_ref, o32_ref, o16_ref, mix_ref = refs[n_parts:]
    j = pl.program_id(1)

    @pl.when(j == 0)
    def _():
        off = 0
        for p_ref, wd in zip(part_refs, widths):
            mix_ref[:, off:off + wd] = p_ref[...]
            off += wd

    z = ALPHA * x_ref[...] + jnp.dot(mix_ref[...], w_ref[...], preferred_element_type=F32)
    for jj in range(nj):
        @pl.when(j == jj)
        def _(jj=jj):
            o32_ref[:, jj * bn:(jj + 1) * bn] = z

    @pl.when(j == nj - 1)
    def _():
        zz = o32_ref[...]
        mu = jnp.mean(zz, axis=-1, keepdims=True)
        zc = zz - mu
        var = jnp.mean(zc * zc, axis=-1, keepdims=True)
        y = zc * lax.rsqrt(var + LN_EPS) * g_ref[...] + b_ref[...]
        o32_ref[...] = y
        o16_ref[...] = y.astype(BF16)


def _proj_residual_ln(parts, w, x_res, g, b, *, bm, bn):
    t, d = x_res.shape
    widths = tuple(p.shape[1] for p in parts)
    kdim = sum(widths)
    nj = d // bn
    kern = functools.partial(_proj_ln_kernel, n_parts=len(parts), widths=widths, bn=bn, nj=nj)
    in_specs = [pl.BlockSpec((bm, wd), lambda i, j: (i, 0)) for wd in widths]
    in_specs += [pl.BlockSpec((kdim, bn), lambda i, j: (0, j)),
                 pl.BlockSpec((bm, bn), lambda i, j: (i, j)),
                 pl.BlockSpec((1, d), lambda i, j: (0, 0)),
                 pl.BlockSpec((1, d), lambda i, j: (0, 0))]
    return pl.pallas_call(
        kern,
        grid=(t // bm, nj),
        in_specs=in_specs,
        out_specs=[pl.BlockSpec((bm, d), lambda i, j: (i, 0)),
                   pl.BlockSpec((bm, d), lambda i, j: (i, 0))],
        out_shape=[jax.ShapeDtypeStruct((t, d), F32), jax.ShapeDtypeStruct((t, d), BF16)],
        scratch_shapes=[pltpu.VMEM((bm, kdim), BF16)],
        compiler_params=_cparams(("parallel", "arbitrary")),
        name="proj_ln",
    )(*parts, w, x_res, g.reshape(1, d), b.reshape(1, d))


def _nat_geometry(t):
    rows = t // GRID_W
    win_r = min(NAT_WIN_ROWS, rows)
    kr = NAT_KEY_ROWS
    assert rows >= kr and rows % NAT_Q_ROWS == 0
    n_blk = rows // NAT_Q_ROWS
    col = np.arange(GRID_W)
    c0 = np.clip(col - NAT_WIN_COLS // 2, 0, GRID_W - NAT_WIN_COLS)
    keys, ks_list, ty_list, tables = {}, [], [], []
    for i in range(n_blk):
        qr = i * NAT_Q_ROWS + np.arange(NAT_Q_ROWS)
        r0 = np.clip(qr - win_r // 2, 0, rows - win_r)
        ks = int(np.clip(r0[0], 0, rows - kr))
        key = (tuple(qr - ks), tuple(r0 - ks))
        if key not in keys:
            keys[key] = len(tables)
            q_row = np.repeat(qr, GRID_W)
            q_r0 = np.repeat(r0, GRID_W)
            q_col = np.tile(col, NAT_Q_ROWS)
            q_c0 = np.tile(c0, NAT_Q_ROWS)
            k_row = np.repeat(ks + np.arange(kr), GRID_W)
            k_col = np.tile(col, kr)
            row_ok = (k_row[None, :] >= q_r0[:, None]) & (k_row[None, :] < q_r0[:, None] + win_r)
            col_ok = (k_col[None, :] >= q_c0[:, None]) & (k_col[None, :] < q_c0[:, None] + NAT_WIN_COLS)
            dr = np.clip(k_row[None, :] - q_row[:, None] + NAT_WIN_ROWS - 1, 0, 2 * NAT_WIN_ROWS - 2)
            dc = np.clip(k_col[None, :] - q_col[:, None] + NAT_WIN_COLS - 1, 0, 2 * NAT_WIN_COLS - 2)
            tables.append((dr, dc, row_ok & col_ok))
        ks_list.append(ks)
        ty_list.append(keys[key])
    dr = np.stack([tb[0] for tb in tables]).astype(np.int32)
    dc = np.stack([tb[1] for tb in tables]).astype(np.int32)
    ok = np.stack([tb[2] for tb in tables])
    return np.asarray(ks_list, np.int32), np.asarray(ty_list, np.int32), dr, dc, ok


def _nat_bias_kernel(rp_ref, ok_ref, o_ref, *, dr_small):
    lane = lax.broadcasted_iota(jnp.int32, (GRID_W, LANES), 1)
    for ty, dr_ty in enumerate(dr_small):
        for qrl, dr_row in enumerate(dr_ty):
            for pair in range(len(dr_row) // 2):
                da, db = int(dr_row[2 * pair]), int(dr_row[2 * pair + 1])
                ya = _toeplitz_rows(rp_ref[0, da:da + 1, :], GRID_W, GRID_W - 1)
                yb = _toeplitz_rows(rp_ref[0, db:db + 1, :], GRID_W, LANES - 1)
                rows = slice(qrl * GRID_W, (qrl + 1) * GRID_W)
                cols = slice(pair * LANES, (pair + 1) * LANES)
                o_ref[ty, 0, rows, cols] = jnp.where(ok_ref[ty, rows, cols] > 0.5,
                                                     jnp.where(lane < GRID_W, ya, yb), NEG_BIG)


def _nat_bias_table(rpb, dr_small, ok):
    n_type, nq, kw = ok.shape
    assert 2 * GRID_W == LANES and NAT_KEY_ROWS % 2 == 0
    pad_c = GRID_W - NAT_WIN_COLS
    n_dr = rpb.shape[1]
    rp = jnp.pad(rpb.astype(F32), ((0, 0), (0, -n_dr % 8), (pad_c, LANES - rpb.shape[2] - pad_c)))
    return pl.pallas_call(
        functools.partial(_nat_bias_kernel, dr_small=dr_small.tolist()),
        grid=(NAT_HEADS,),
        in_specs=[pl.BlockSpec((1, rp.shape[1], LANES), lambda h: (h, 0, 0)),
                  pl.BlockSpec((n_type, nq, kw), lambda h: (0, 0, 0))],
        out_specs=pl.BlockSpec((n_type, 1, nq, kw), lambda h: (0, h, 0, 0)),
        out_shape=jax.ShapeDtypeStruct((n_type, NAT_HEADS, nq, kw), F32),
        compiler_params=_cparams(("parallel",)),
        name="nat_bias_table",
    )(rp, jnp.asarray(ok.astype(np.float32)))


def _nat_kernel(ks_ref, ty_ref, q_ref, k_ref, v_ref, bias_ref, g_ref, o_ref, *, blocks_per_step, kw):
    nq = NAT_Q_ROWS * GRID_W
    base = pl.program_id(1) * blocks_per_step
    scale = HEAD_DIM ** -0.5
    blocks = range(blocks_per_step)
    k0s = [pl.multiple_of(ks_ref[base + jj] * GRID_W, GRID_W) for jj in blocks]
    ss = [lax.dot_general(q_ref[jj * nq:(jj + 1) * nq, :], k_ref[pl.ds(k0s[jj], kw), :],
                          (((1,), (1,)), ((), ())), preferred_element_type=F32) * scale
          + bias_ref[ty_ref[base + jj], 0] for jj in blocks]
    ps = [jnp.exp(s - jnp.max(s, axis=-1, keepdims=True)) for s in ss]
    ls = [jnp.sum(p, axis=-1, keepdims=True) for p in ps]
    os = [jnp.dot(ps[jj].astype(BF16), v_ref[pl.ds(k0s[jj], kw), :], preferred_element_type=F32) / ls[jj]
          for jj in blocks]
    for jj in blocks:
        o = os[jj]
        ms = jnp.mean(o * o, axis=-1, keepdims=True)
        o_ref[jj * nq:(jj + 1) * nq, :] = (o * lax.rsqrt(ms + LN_EPS) * g_ref[0]).astype(o_ref.dtype)


def _nat_attention(u, rpb, norm_g, *, blocks_per_step):
    t = u.shape[0]
    ks, ty, dr, dc, ok = _nat_geometry(t)
    n_blk = ks.shape[0]
    kw = NAT_KEY_ROWS * GRID_W
    nq = NAT_Q_ROWS * GRID_W
    n_type = ok.shape[0]
    bias = _nat_bias_table(rpb, dr[:, ::GRID_W, ::GRID_W], ok)
    g = norm_g.astype(F32).reshape(NAT_HEADS, 1, HEAD_DIM)
    rows_per_step = blocks_per_step * nq
    kern = functools.partial(_nat_kernel, blocks_per_step=blocks_per_step, kw=kw)
    grid_spec = pltpu.PrefetchScalarGridSpec(
        num_scalar_prefetch=2,
        grid=(NAT_HEADS, n_blk // blocks_per_step),
        in_specs=[pl.BlockSpec((rows_per_step, HEAD_DIM), lambda h, g_, *_: (g_, h)),
                  pl.BlockSpec((t, HEAD_DIM), lambda h, g_, *_: (0, NAT_HEADS + h)),
                  pl.BlockSpec((t, HEAD_DIM), lambda h, g_, *_: (0, 2 * NAT_HEADS + h)),
                  pl.BlockSpec((n_type, 1, nq, kw), lambda h, g_, *_: (0, h, 0, 0)),
                  pl.BlockSpec((1, 1, HEAD_DIM), lambda h, g_, *_: (h, 0, 0))],
        out_specs=pl.BlockSpec((rows_per_step, HEAD_DIM), lambda h, g_, *_: (g_, h)),
    )
    return pl.pallas_call(
        kern,
        grid_spec=grid_spec,
        out_shape=jax.ShapeDtypeStruct((t, NAT_WIDTH), BF16),
        compiler_params=_cparams(("parallel", "arbitrary")),
        name="nat_attn",
    )(jnp.asarray(ks), jnp.asarray(ty), u, u, u, bias, g)


def _t5_bucket(rel):
    half = REL_BUCKETS // 2
    max_exact = half // 2
    ret = jnp.where(rel > 0, half, 0)
    n = jnp.abs(rel)
    n_f = jnp.maximum(n, 1).astype(jnp.float32)
    large = max_exact + (jnp.log(n_f / max_exact) / math.log(REL_MAX_DIST / max_exact)
                         * (half - max_exact)).astype(jnp.int32)
    large = jnp.minimum(large, half - 1)
    return ret + jnp.where(n < max_exact, n, large)


def _diff_kernel(far_ref, q_ref, k_ref, v_ref, bias_ref, lp_ref, g_ref, o_ref,
                 qs_ref, m_ref, acc_ref, *, bk, ratio, nk, lam_init, rb):
    h = pl.program_id(0)
    qi = pl.program_id(1)
    bq = q_ref.shape[0]
    n_rb = 2 * bq // rb

    q = q_ref[...]
    lane = lax.broadcasted_iota(jnp.int32, q.shape, 1)
    qsc = q * jnp.asarray(DIFF_QK_DIM ** -0.5, q.dtype)
    zero = jnp.zeros_like(qsc)
    qs_ref[0:bq, :] = jnp.where(lane < DIFF_QK_DIM, qsc, zero)
    qs_ref[bq:2 * bq, :] = jnp.where(lane >= DIFF_QK_DIM, qsc, zero)
    m_ref[...] = jnp.full(m_ref.shape, -jnp.inf, F32)
    acc_ref[...] = jnp.zeros(acc_ref.shape, F32)

    def step(ki, near_idx, const):
        k0 = pl.multiple_of(ki * bk, bk)
        k = k_ref[pl.ds(k0, bk), :]
        v = v_ref[pl.ds(k0, bk), :]
        v_ext = jnp.concatenate([v, jnp.ones_like(v)], axis=1)
        for r in range(n_rb):
            rows = slice(r * rb, (r + 1) * rb)
            s = lax.dot_general(qs_ref[rows, :], k, (((1,), (1,)), ((), ())), preferred_element_type=F32)
            if near_idx is not None:
                r0 = (r * rb) % bq
                s = s + bias_ref[0, near_idx, r0:r0 + rb, :]
            mt = jnp.max(s, axis=-1, keepdims=True)
            if const is not None:
                mt = mt + const
            m_prev = m_ref[rows, :]
            m_new = jnp.maximum(m_prev, mt)
            a = jnp.exp(m_prev - m_new)
            shift = m_new if const is None else m_new - const
            p = jnp.exp(s - jnp.tile(shift, (1, bk // LANES)))
            acc_ref[rows, :] = (jnp.tile(a, (1, 2)) * acc_ref[rows, :]
                                + jnp.dot(p.astype(BF16), v_ext, preferred_element_type=F32))
            m_ref[rows, :] = m_new

    c_left = far_ref[h, 0]
    c_right = far_ref[h, 1]
    first_near = qi * ratio - 1
    n_near = ratio + 2

    n_left = jnp.clip(first_near, 0, nk)
    right_start = jnp.clip(first_near + n_near, 0, nk)
    n_far = n_left + nk - right_start

    def far_step(it):
        is_left = it < n_left
        step(jnp.where(is_left, it, it - n_left + right_start), None, jnp.where(is_left, c_left, c_right))

    def pair_body(p, carry):
        far_step(2 * p)
        far_step(2 * p + 1)
        return carry

    lax.fori_loop(0, n_far // 2, pair_body, 0)

    @pl.when(lax.rem(n_far, 2) == 1)
    def _():
        far_step(n_far - 1)

    for tt in range(n_near):
        ki = first_near + tt

        @pl.when((ki >= 0) & (ki < nk))
        def _(ki=ki, tt=tt):
            step(ki, tt, None)

    acc = acc_ref[...]
    o = acc[:, 0:HEAD_DIM] / acc[:, HEAD_DIM:2 * HEAD_DIM]
    lp = lp_ref[...]
    lam = (jnp.exp(jnp.sum(lp[0:1] * lp[1:2], axis=-1, keepdims=True))
           - jnp.exp(jnp.sum(lp[2:3] * lp[3:4], axis=-1, keepdims=True)) + lam_init)
    d = o[0:bq] - lam * o[bq:2 * bq]
    ms = jnp.mean(d * d, axis=-1, keepdims=True)
    o_ref[...] = (d * lax.rsqrt(ms + LN_EPS) * g_ref[...] * (1.0 - lam_init)).astype(o_ref.dtype)


def _diff_bias_tiles(t5_table, bq, bk):
    ratio = bq // bk
    span = bq + bk + 1
    rel = jnp.arange(-span, span + 1, dtype=jnp.int32)
    vec = jnp.take(t5_table.astype(F32), _t5_bucket(rel), axis=0)
    vec_t = vec.T
    diags = jnp.stack([vec_t[:, (tt - 1) * bk + span - (bq - 1):(tt - 1) * bk + span + bk]
                       for tt in range(ratio + 2)], axis=1)
    tiles = _toeplitz(diags, bq, bk)
    far = jnp.stack([vec[0], vec[-1]], axis=1)
    return tiles, far


def _toeplitz_rows(w_row, n_rows, origin):
    width = w_row.shape[1]
    x = jnp.broadcast_to(w_row, (n_rows, width))
    return pltpu.roll(x, (width - origin) % width, 1, stride=1, stride_axis=0)


def _toeplitz_kernel(d_ref, o_ref):
    bq, bk = o_ref.shape[2], o_ref.shape[3]
    o_ref[0, 0] = _toeplitz_rows(d_ref[0, 0], bq, bq - 1)[:, :bk]


def _toeplitz(w, n_rows, n_cols):
    na, nb, ln = w.shape
    width = -(-ln // LANES) * LANES
    wp = jnp.pad(w, ((0, 0), (0, 0), (0, width - ln))).reshape(na, nb, 1, width)
    return pl.pallas_call(
        _toeplitz_kernel,
        grid=(na, nb),
        in_specs=[pl.BlockSpec((1, 1, 1, width), lambda a, b: (a, b, 0, 0))],
        out_specs=pl.BlockSpec((1, 1, n_rows, n_cols), lambda a, b: (a, b, 0, 0)),
        out_shape=jax.ShapeDtypeStruct((na, nb, n_rows, n_cols), F32),
        compiler_params=_cparams(("parallel", "parallel")),
        name="toeplitz_tiles",
    )(wp)


def _diff_attention(u, bias_tiles, far, lam_params, subln_g, lam_init, *, bq, bk, rb, col0):
    t = u.shape[0]
    assert bk >= REL_MAX_DIST and bq % bk == 0
    ratio = bq // bk
    nk = t // bk
    n_near = ratio + 2
    assert bq % rb == 0
    kern = functools.partial(_diff_kernel, bk=bk, ratio=ratio, nk=nk, lam_init=lam_init, rb=rb)
    grid_spec = pltpu.PrefetchScalarGridSpec(
        num_scalar_prefetch=1,
        grid=(DIFF_HEADS, t // bq),
        in_specs=[pl.BlockSpec((bq, HEAD_DIM), lambda h, i, *_: (i, col0 + h)),
                  pl.BlockSpec((t, HEAD_DIM), lambda h, i, *_: (0, col0 + DIFF_HEADS + h)),
                  pl.BlockSpec((t, HEAD_DIM), lambda h, i, *_: (0, col0 + 2 * DIFF_HEADS + h)),
                  pl.BlockSpec((1, n_near, bq, bk), lambda h, i, *_: (h, 0, 0, 0)),
                  pl.BlockSpec((4, DIFF_QK_DIM), lambda h, i, *_: (0, 0)),
                  pl.BlockSpec((1, HEAD_DIM), lambda h, i, *_: (0, 0))],
        out_specs=pl.BlockSpec((bq, HEAD_DIM), lambda h, i, *_: (i, h)),
        scratch_shapes=[pltpu.VMEM((2 * bq, HEAD_DIM), BF16),
                        pltpu.VMEM((2 * bq, LANES), F32),
                        pltpu.VMEM((2 * bq, 2 * HEAD_DIM), F32)],
    )
    return pl.pallas_call(
        kern,
        grid_spec=grid_spec,
        out_shape=jax.ShapeDtypeStruct((t, DIFF_WIDTH), BF16),
        compiler_params=_cparams(("parallel", "arbitrary")),
        name="diff_attn",
    )(far, u, u, u, bias_tiles, lam_params.astype(F32), subln_g.astype(F32).reshape(1, HEAD_DIM))


CONV_HALO = 16


def _conv_silu_kernel(prev_ref, cur_ref, next_ref, w_ref, o_ref, *, k_scale):
    i = pl.program_id(0)
    j = pl.program_id(1)
    bt = cur_ref.shape[0]
    x = cur_ref[...].astype(F32)
    p = jnp.where(i > 0, prev_ref[...].astype(F32), 0.0)
    nx = jnp.where(i < pl.num_programs(0) - 1, next_ref[...].astype(F32), 0.0)
    ext = jnp.concatenate([p, x, nx], axis=0)
    n_ext = bt + 2 * CONV_HALO
    half = MLSTM_CONV // 2
    acc = w_ref[half:half + 1, :] * x
    for tap in range(MLSTM_CONV):
        s = tap - half
        if s == 0:
            continue
        rolled = pltpu.roll(ext, (n_ext - s) % n_ext, 0)
        acc = acc + w_ref[tap:tap + 1, :] * rolled[CONV_HALO:CONV_HALO + bt]
    y = acc * jax.nn.sigmoid(acc)
    o_ref[...] = (y * jnp.where(j == 1, k_scale, 1.0)).astype(o_ref.dtype)


def _conv_silu(u, conv_w, *, bt, col0):
    t = u.shape[0]
    w = MLSTM_WIDTH
    nb = t // bt
    hb = bt // CONV_HALO
    last = t // CONV_HALO - 1
    return pl.pallas_call(
        functools.partial(_conv_silu_kernel, k_scale=HEAD_DIM ** -0.5),
        grid=(nb, 2),
        in_specs=[pl.BlockSpec((CONV_HALO, w), lambda i, j: (jnp.maximum(i * hb - 1, 0), col0 + j)),
                  pl.BlockSpec((bt, w), lambda i, j: (i, col0 + j)),
                  pl.BlockSpec((CONV_HALO, w), lambda i, j: (jnp.minimum((i + 1) * hb, last), col0 + j)),
                  pl.BlockSpec((MLSTM_CONV, w), lambda i, j: (0, j))],
        out_specs=pl.BlockSpec((bt, w), lambda i, j: (i, j)),
        out_shape=jax.ShapeDtypeStruct((t, 2 * w), BF16),
        compiler_params=_cparams(("parallel", "parallel")),
        name="mlstm_conv",
    )(u, u, u, conv_w.astype(F32))


def _split3(x):
    x1 = x.astype(BF16)
    r1 = x - x1.astype(F32)
    x2 = r1.astype(BF16)
    x3 = (r1 - x2.astype(F32)).astype(BF16)
    return x1, x2, x3


def _log_sigmoid(x):
    return jnp.minimum(x, 0.0) - jnp.log(1.0 + jnp.exp(-jnp.abs(x)))


def _mlstm_units(units, c_ref, n_ref, m_ref, lc):
    nt_dims = (((1,), (1,)), ((), ()))
    tn_dims = (((0,), (0,)), ((), ()))
    rd = lambda u, name: u[name][:, u["sl"]]
    dmats = [jnp.where(u["mask"], u["bcol"] - u["brow"] + u["igrow"], -jnp.inf) for u in units]
    m_prevs = [m_ref[u["idx"]] for u in units]
    inters = [u["bcol"] + mp for u, mp in zip(units, m_prevs)]
    m_ts = [jnp.maximum(it, jnp.max(d, axis=-1, keepdims=True)) for it, d in zip(inters, dmats)]
    w_inters = [jnp.exp(it - mt) for it, mt in zip(inters, m_ts)]
    ss = [lax.dot_general(rd(u, "q_ref"), rd(u, "k_ref"), nt_dims, preferred_element_type=F32) for u in units]
    ps = [jnp.exp(d - mt) * s for d, mt, s in zip(dmats, m_ts, ss)]
    nums = [wi * jnp.dot(rd(u, "q_ref"), c_ref[u["idx"]].astype(BF16), preferred_element_type=F32)
            + jnp.dot(p.astype(BF16), rd(u, "v_ref"), preferred_element_type=F32)
            for u, wi, p in zip(units, w_inters, ps)]
    dens = [wi * jnp.sum(rd(u, "q_ref").astype(F32) * n_ref[u["idx"]], axis=-1, keepdims=True)
            + jnp.sum(p, axis=-1, keepdims=True) for u, wi, p in zip(units, w_inters, ps)]
    for u, num, den, mt in zip(units, nums, dens, m_ts):
        u["h_ref"][:, u["sl"]] = num / jnp.maximum(jnp.abs(den), jnp.exp(-mt))
    gs = [u["bcol"][lc - 1:lc] if u["fwd"] else u["bcol"][0:1] for u in units]
    a_s = [g - u["bcol"] + u["igcol"] for u, g in zip(units, gs)]
    m_locs = [jnp.max(a, axis=0, keepdims=True) for a in a_s]
    kws = [rd(u, "k_ref").astype(F32) * jnp.exp(a - ml) for u, a, ml in zip(units, a_s, m_locs)]
    c_locs = [lax.dot_general(kw.astype(BF16), rd(u, "v_ref"), tn_dims, preferred_element_type=F32)
              for u, kw in zip(units, kws)]
    for u, g, mp, ml, kw, c_loc in zip(units, gs, m_prevs, m_locs, kws, c_locs):
        m_new = jnp.maximum(g + mp, ml)
        s_old = jnp.exp(g + mp - m_new)
        s_loc = jnp.exp(ml - m_new)
        c_ref[u["idx"]] = s_old * c_ref[u["idx"]] + s_loc * c_loc
        n_ref[u["idx"]] = s_old * n_ref[u["idx"]] + s_loc * jnp.sum(kw, axis=0, keepdims=True)
        m_ref[u["idx"]] = m_new


def _mlstm_kernel(qf_ref, kf_ref, vf_ref, gf_ref, qb_ref, kb_ref, vb_ref, gb_ref, gbias_ref,
                  hf_ref, hb_ref, c_ref, n_ref, m_ref, *, lc):
    nh = MLSTM_HEADS

    @pl.when(pl.program_id(0) == 0)
    def _():
        c_ref[...] = jnp.zeros(c_ref.shape, F32)
        n_ref[...] = jnp.zeros(n_ref.shape, F32)
        m_ref[...] = jnp.zeros(m_ref.shape, F32)

    t_io = lax.broadcasted_iota(jnp.int32, (lc, lc), 0)
    s_io = lax.broadcasted_iota(jnp.int32, (lc, lc), 1)
    units = []
    for fwd in (True, False):
        g_ref, q_ref, k_ref, v_ref, h_ref = ((gf_ref, qf_ref, kf_ref, vf_ref, hf_ref) if fwd
                                             else (gb_ref, qb_ref, kb_ref, vb_ref, hb_ref))
        gates = g_ref[...] + gbias_ref[...]
        mask = (s_io <= t_io) if fwd else (s_io >= t_io)
        tri = jnp.where(mask, 1.0, 0.0).astype(BF16)
        cum = None
        for piece in _split3(_log_sigmoid(gates)):
            part = jnp.dot(tri, piece, preferred_element_type=F32)
            cum = part if cum is None else cum + part
        gates_t = jnp.transpose(gates)
        cum_t = jnp.transpose(cum)
        lane0 = 0 if fwd else 2 * nh
        for hh in range(nh):
            li, lb = lane0 + hh, lane0 + nh + hh
            units.append(dict(fwd=fwd, idx=(0 if fwd else nh) + hh, sl=slice(hh * HEAD_DIM, (hh + 1) * HEAD_DIM),
                              q_ref=q_ref, k_ref=k_ref, v_ref=v_ref, h_ref=h_ref, mask=mask,
                              bcol=cum[:, lb:lb + 1], igcol=gates[:, li:li + 1],
                              brow=cum_t[lb:lb + 1, :], igrow=gates_t[li:li + 1, :]))
    _mlstm_units(units, c_ref, n_ref, m_ref, lc)


def _mlstm_scan(qk, u, gates, gate_b, *, lc, v_col):
    t = qk.shape[0]
    w = MLSTM_WIDTH
    nc = t // lc
    gbias = jnp.pad(gate_b.astype(F32), (0, LANES - gate_b.shape[0])).reshape(1, LANES)
    fmap = lambda col: (lambda c: (c, col))
    bmap = lambda col: (lambda c: (nc - 1 - c, col))
    nhd = 2 * MLSTM_HEADS
    return pl.pallas_call(
        functools.partial(_mlstm_kernel, lc=lc),
        grid=(nc,),
        in_specs=[pl.BlockSpec((lc, w), fmap(0)), pl.BlockSpec((lc, w), fmap(1)),
                  pl.BlockSpec((lc, w), fmap(v_col)), pl.BlockSpec((lc, LANES), fmap(0)),
                  pl.BlockSpec((lc, w), bmap(0)), pl.BlockSpec((lc, w), bmap(1)),
                  pl.BlockSpec((lc, w), bmap(v_col)), pl.BlockSpec((lc, LANES), bmap(0)),
                  pl.BlockSpec((1, LANES), lambda c: (0, 0))],
        out_specs=[pl.BlockSpec((lc, w), fmap(0)), pl.BlockSpec((lc, w), bmap(0))],
        out_shape=[jax.ShapeDtypeStruct((t, w), F32), jax.ShapeDtypeStruct((t, w), F32)],
        scratch_shapes=[pltpu.VMEM((nhd, HEAD_DIM, HEAD_DIM), F32),
                        pltpu.VMEM((nhd, 1, HEAD_DIM), F32),
                        pltpu.VMEM((nhd, 1, 1), F32)],
        compiler_params=_cparams(("arbitrary",)),
        name="mlstm_scan",
    )(qk, qk, u, gates, qk, qk, u, gates, gbias)


def _mlstm_out_kernel(hf_ref, hb_ref, o_ref, g_ref, out_ref):
    for hh in range(MLSTM_HEADS):
        sl = slice(hh * HEAD_DIM, (hh + 1) * HEAD_DIM)
        x = (hf_ref[:, sl] + hb_ref[:, sl]) * jax.nn.sigmoid(o_ref[:, sl].astype(F32))
        xc = x - jnp.mean(x, axis=-1, keepdims=True)
        y = xc * lax.rsqrt(jnp.mean(xc * xc, axis=-1, keepdims=True) + LN_EPS) * g_ref[:, sl]
        out_ref[:, sl] = y.astype(out_ref.dtype)


def _mlstm_out(hf, hb, u, norm_g, *, bt, o_col):
    t, w = hf.shape
    return pl.pallas_call(
        _mlstm_out_kernel,
        grid=(t // bt,),
        in_specs=[pl.BlockSpec((bt, w), lambda i: (i, 0)), pl.BlockSpec((bt, w), lambda i: (i, 0)),
                  pl.BlockSpec((bt, w), lambda i: (i, o_col)), pl.BlockSpec((1, w), lambda i: (0, 0))],
        out_specs=pl.BlockSpec((bt, w), lambda i: (i, 0)),
        out_shape=jax.ShapeDtypeStruct((t, w), BF16),
        compiler_params=_cparams(("parallel",)),
        name="mlstm_out",
    )(hf, hb, u, norm_g.astype(F32).reshape(1, w))


def _mlstm_block(u, gates, conv_w, gate_b, norm_g, *, lc, col0):
    qk = _conv_silu(u, conv_w, bt=512, col0=col0)
    hf, hb = _mlstm_scan(qk, u, gates, gate_b, lc=lc, v_col=col0 + 2)
    return _mlstm_out(hf, hb, u, norm_g, bt=512, o_col=col0 + 3)


def _router_kernel(x_ref, w_ref, b_ref, e_ref, g_ref):
    logits = jnp.dot(x_ref[...], w_ref[...], preferred_element_type=F32) + b_ref[...]
    lane = lax.broadcasted_iota(jnp.int32, logits.shape, 1)
    work = logits
    e_out = jnp.zeros(logits.shape, jnp.int32)
    vals = []
    for kk in range(TOP_K):
        m = jnp.max(work, axis=-1, keepdims=True)
        idx = jnp.min(jnp.where(work == m, lane, LANES), axis=-1, keepdims=True)
        e_out = jnp.where(lane == kk, idx, e_out)
        vals.append(m)
        work = jnp.where(lane == idx, -jnp.inf, work)
    ex = [jnp.exp(v - vals[0]) for v in vals]
    den = ex[0]
    for e in ex[1:]:
        den = den + e
    g_out = jnp.zeros(logits.shape, F32)
    for kk in range(TOP_K):
        g_out = jnp.where(lane == kk, ex[kk] / den, g_out)
    e_ref[...] = e_out
    g_ref[...] = g_out


def _router(x16, w_router, b_router, *, bm):
    t, d = x16.shape
    n_e = w_router.shape[1]
    w = jnp.pad(w_router, ((0, 0), (0, LANES - n_e))).astype(BF16)
    b = jnp.pad(b_router.astype(F32), (0, LANES - n_e), constant_values=NEG_BIG).reshape(1, LANES)
    return pl.pallas_call(
        _router_kernel,
        grid=(t // bm,),
        in_specs=[pl.BlockSpec((bm, d), lambda i: (i, 0)),
                  pl.BlockSpec((d, LANES), lambda i: (0, 0)),
                  pl.BlockSpec((1, LANES), lambda i: (0, 0))],
        out_specs=[pl.BlockSpec((bm, LANES), lambda i: (i, 0)),
                   pl.BlockSpec((bm, LANES), lambda i: (i, 0))],
        out_shape=[jax.ShapeDtypeStruct((t, LANES), jnp.int32), jax.ShapeDtypeStruct((t, LANES), F32)],
        compiler_params=_cparams(("parallel",)),
        name="router",
    )(x16, w, b)


def _rank_kernel(e_ref, rank_ref, cnt_ref, base_ref):
    @pl.when(pl.program_id(0) == 0)
    def _():
        base_ref[...] = jnp.zeros(base_ref.shape, F32)

    e = e_ref[...]
    bt = e.shape[0]
    lane = lax.broadcasted_iota(jnp.int32, e.shape, 1)
    onehots = [jnp.where(lane == e[:, kk:kk + 1], 1.0, 0.0) for kk in range(TOP_K)]
    tot = onehots[0]
    for oh in onehots[1:]:
        tot = tot + oh
    t_io = lax.broadcasted_iota(jnp.int32, (bt, bt), 0)
    s_io = lax.broadcasted_iota(jnp.int32, (bt, bt), 1)
    earlier = jnp.where(s_io < t_io, 1.0, 0.0).astype(BF16)
    pos = jnp.dot(earlier, tot.astype(BF16), preferred_element_type=F32) + base_ref[...]
    out = jnp.zeros(e.shape, jnp.int32)
    for kk in range(TOP_K):
        rk = jnp.sum(onehots[kk] * pos, axis=-1, keepdims=True)
        out = jnp.where(lane == kk, rk.astype(jnp.int32), out)
    rank_ref[...] = out
    base_ref[...] = base_ref[...] + jnp.sum(tot, axis=0, keepdims=True)
    cnt_ref[...] = base_ref[...]


def _slot_ranks(top_e, *, bt):
    t = top_e.shape[0]
    return pl.pallas_call(
        _rank_kernel,
        grid=(t // bt,),
        in_specs=[pl.BlockSpec((bt, LANES), lambda i: (i, 0))],
        out_specs=[pl.BlockSpec((bt, LANES), lambda i: (i, 0)), pl.BlockSpec((1, LANES), lambda i: (0, 0))],
        out_shape=[jax.ShapeDtypeStruct((t, LANES), jnp.int32), jax.ShapeDtypeStruct((1, LANES), F32)],
        scratch_shapes=[pltpu.VMEM((1, LANES), F32)],
        compiler_params=_cparams(("arbitrary",)),
        name="moe_rank",
    )(top_e)


def _dispatch_plan(top_e_lanes, n_experts, blk, rank_bt):
    t = top_e_lanes.shape[0]
    n_slots = t * TOP_K
    n_blocks = -(-n_slots // blk) + n_experts
    rank_lanes, counts_f = _slot_ranks(top_e_lanes, bt=rank_bt)
    slot_e = top_e_lanes[:, :TOP_K].reshape(-1)
    rank = rank_lanes[:, :TOP_K].reshape(-1)
    counts = counts_f[0, :n_experts].astype(jnp.int32)
    onehot = (slot_e[:, None] == jnp.arange(n_experts, dtype=jnp.int32)[None, :]).astype(jnp.int32)
    padded = (counts + blk - 1) // blk * blk
    pad_end = jnp.cumsum(padded)
    pad_start = pad_end - padded
    dest = (jnp.sum(pad_start[None, :] * onehot, axis=1) + rank).astype(jnp.int32)
    slot_tok = jnp.arange(n_slots, dtype=jnp.int32) // TOP_K
    row_tok = jnp.zeros((n_blocks * blk,), jnp.int32).at[dest].set(slot_tok)
    n_act = (pad_end[-1] // blk).astype(jnp.int32)
    blk_id = jnp.minimum(jnp.arange(n_blocks, dtype=jnp.int32), n_act - 1)
    blk_expert = jnp.minimum(jnp.sum((pad_end[None, :] <= (blk_id * blk)[:, None]).astype(jnp.int32), axis=1),
                             n_experts - 1)
    return row_tok, dest, blk_expert, n_act.reshape(1)


def _expert_kernel(be_ref, tok_ref, nact_ref, x_hbm, wg_hbm, bg_ref, wl_hbm, bl_ref, wd_hbm, bd_ref, o_ref,
                   xg_ref, wg_buf, wl_buf, wd_buf, sem_ref, wsem_ref, *, blk, layer):
    b = pl.program_id(0)
    n_blocks = pl.num_programs(0)
    n_act = nact_ref[0]
    n_slots = GATHER_DEPTH + 1
    slot = lax.rem(b, n_slots)
    e_cur = be_ref[b]
    e_prev = be_ref[jnp.maximum(b - 1, 0)]
    e_next = be_ref[jnp.minimum(b + 1, n_blocks - 1)]
    fresh = (b == 0) | (e_cur != e_prev)

    def row_copy(blk_idx, r, dst_slot):
        tok = tok_ref[blk_idx * blk + r]
        return pltpu.make_async_copy(x_hbm.at[pl.ds(tok, 1), :], xg_ref.at[dst_slot, pl.ds(r, 1), :],
                                     sem_ref.at[dst_slot])

    def w_copy(which, e):
        src, dst = ((wg_hbm, wg_buf), (wl_hbm, wl_buf), (wd_hbm, wd_buf))[which]
        return pltpu.make_async_copy(src.at[layer, e], dst, wsem_ref.at[which])

    def compute(gather_ahead, has_next):
        pltpu.make_async_copy(xg_ref.at[slot], xg_ref.at[slot], sem_ref.at[slot]).wait()

        @pl.when(fresh)
        def _():
            w_copy(0, e_cur).wait()
            w_copy(1, e_cur).wait()

        if gather_ahead:
            ahead_slot = lax.rem(b + GATHER_DEPTH, n_slots)
            for r in range(blk):
                row_copy(b + GATHER_DEPTH, r, ahead_slot).start()
        xb = xg_ref[slot].astype(BF16)
        hg = jnp.minimum(jnp.dot(xb, wg_buf[...], preferred_element_type=F32) + bg_ref[0, 0], SWIGLU_LIMIT)
        hl = jnp.clip(jnp.dot(xb, wl_buf[...], preferred_element_type=F32) + bl_ref[0, 0],
                      -SWIGLU_LIMIT, SWIGLU_LIMIT)
        hh = (hg * jax.nn.sigmoid(SWIGLU_ALPHA * hg) * (hl + 1.0)).astype(BF16)

        if has_next:
            @pl.when(e_next != e_cur)
            def _():
                w_copy(0, e_next).start()
                w_copy(1, e_next).start()

        @pl.when(fresh)
        def _():
            w_copy(2, e_cur).wait()

        o_ref[...] = jnp.dot(hh, wd_buf[...], preferred_element_type=F32) + bd_ref[0, 0]

        if has_next:
            @pl.when(e_next != e_cur)
            def _():
                w_copy(2, e_next).start()

    @pl.when(b == 0)
    def _():
        for which in range(3):
            w_copy(which, e_cur).start()

        for first in range(GATHER_DEPTH):
            @pl.when(first < n_act)
            def _(first=first):
                def body(r, c):
                    row_copy(first, r, first).start()
                    return c
                lax.fori_loop(0, blk, body, 0, unroll=8)

    @pl.when(b + GATHER_DEPTH < n_act)
    def _():
        compute(True, True)

    @pl.when((b + 1 < n_act) & (b + GATHER_DEPTH >= n_act))
    def _():
        compute(False, True)

    @pl.when(b + 1 == n_act)
    def _():
        compute(False, False)

    @pl.when(b >= n_act)
    def _():
        o_ref[...] = jnp.zeros(o_ref.shape, o_ref.dtype)


def _expert_ffn(x32, row_tok, blk_expert, n_act, w_glu16, b_glu, w_lin16, b_lin, w_down16, b_down, layer, *, blk):
    t, d = x32.shape
    depth, n_e, _, ff = w_glu16.shape
    n_blocks = blk_expert.shape[0]
    bmap = lambda i, be, tok, na: (layer, be[i], 0, 0)
    any_spec = pl.BlockSpec(memory_space=pl.ANY)
    grid_spec = pltpu.PrefetchScalarGridSpec(
        num_scalar_prefetch=3,
        grid=(n_blocks,),
        in_specs=[any_spec,
                  any_spec, pl.BlockSpec((1, 1, 1, ff), bmap),
                  any_spec, pl.BlockSpec((1, 1, 1, ff), bmap),
                  any_spec, pl.BlockSpec((1, 1, 1, d), bmap)],
        out_specs=pl.BlockSpec((blk, d), lambda i, be, tok, na: (i, 0)),
        scratch_shapes=[pltpu.VMEM((GATHER_DEPTH + 1, blk, d), F32),
                        pltpu.VMEM((d, ff), BF16), pltpu.VMEM((d, ff), BF16), pltpu.VMEM((ff, d), BF16),
                        pltpu.SemaphoreType.DMA((GATHER_DEPTH + 1,)), pltpu.SemaphoreType.DMA((3,))],
    )
    return pl.pallas_call(
        functools.partial(_expert_kernel, blk=blk, layer=layer),
        grid_spec=grid_spec,
        out_shape=jax.ShapeDtypeStruct((n_blocks * blk, d), F32),
        compiler_params=_cparams(("arbitrary",)),
        name="expert_ffn",
    )(blk_expert, row_tok, n_act, x32,
      w_glu16, b_glu.astype(F32).reshape(depth, n_e, 1, ff),
      w_lin16, b_lin.astype(F32).reshape(depth, n_e, 1, ff),
      w_down16, b_down.astype(F32).reshape(depth, n_e, 1, d))


def _combine_kernel(dest_ref, ys_hbm, x_ref, gate_ref, g_ref, b_ref, o32_ref, o16_ref, buf_ref, sem_ref, *, bt):
    i = pl.program_id(0)
    n = pl.num_programs(0)

    def row_copy(blk_idx, r, kk, dst_slot):
        row = dest_ref[(blk_idx * bt + r) * TOP_K + kk]
        return pltpu.make_async_copy(ys_hbm.at[pl.ds(row, 1), :], buf_ref.at[dst_slot, kk, pl.ds(r, 1), :],
                                     sem_ref.at[dst_slot])

    def slot_wait(s):
        pltpu.make_async_copy(buf_ref.at[s], buf_ref.at[s], sem_ref.at[s]).wait()

    @pl.when(i == 0)
    def _():
        def body(r, c):
            for blk_idx in range(GATHER_DEPTH):
                for kk in range(TOP_K):
                    row_copy(jnp.minimum(blk_idx, n - 1), r, kk, blk_idx).start()
            return c
        lax.fori_loop(0, bt, body, 0)

    n_slots = GATHER_DEPTH + 1
    slot = lax.rem(i, n_slots)
    slot_wait(slot)
    ahead = jnp.where(i + GATHER_DEPTH < n, i + GATHER_DEPTH, 0)
    ahead_slot = lax.rem(i + GATHER_DEPTH, n_slots)
    for r in range(bt):
        for kk in range(TOP_K):
            row_copy(ahead, r, kk, ahead_slot).start()
    gate = gate_ref[...]
    z = ALPHA * x_ref[...]
    for kk in range(TOP_K):
        z = z + gate[:, kk:kk + 1] * buf_ref[slot, kk]
    mu = jnp.mean(z, axis=-1, keepdims=True)
    zc = z - mu
    var = jnp.mean(zc * zc, axis=-1, keepdims=True)
    y = zc * lax.rsqrt(var + LN_EPS) * g_ref[...] + b_ref[...]
    o32_ref[...] = y
    o16_ref[...] = y.astype(BF16)

    @pl.when(i == n - 1)
    def _():
        for extra in range(1, GATHER_DEPTH + 1):
            slot_wait(lax.rem(i + extra, n_slots))


def _combine_ln(ys, dest, x32, gate, g, b, *, bt):
    t, d = x32.shape
    grid_spec = pltpu.PrefetchScalarGridSpec(
        num_scalar_prefetch=1,
        grid=(t // bt,),
        in_specs=[pl.BlockSpec(memory_space=pl.ANY),
                  pl.BlockSpec((bt, d), lambda i, dr: (i, 0)),
                  pl.BlockSpec((bt, LANES), lambda i, dr: (i, 0)),
                  pl.BlockSpec((1, d), lambda i, dr: (0, 0)),
                  pl.BlockSpec((1, d), lambda i, dr: (0, 0))],
        out_specs=[pl.BlockSpec((bt, d), lambda i, dr: (i, 0)),
                   pl.BlockSpec((bt, d), lambda i, dr: (i, 0))],
        scratch_shapes=[pltpu.VMEM((GATHER_DEPTH + 1, TOP_K, bt, d), F32),
                        pltpu.SemaphoreType.DMA((GATHER_DEPTH + 1,))],
    )
    return pl.pallas_call(
        functools.partial(_combine_kernel, bt=bt),
        grid_spec=grid_spec,
        out_shape=[jax.ShapeDtypeStruct((t, d), F32), jax.ShapeDtypeStruct((t, d), BF16)],
        compiler_params=_cparams(("arbitrary",)),
        name="moe_combine_ln",
    )(dest, ys, x32, gate, g.astype(F32).reshape(1, d), b.astype(F32).reshape(1, d))


def _moe_block(x32, x16, w_router, b_router, w_glu16, b_glu, w_lin16, b_lin, w_down16, b_down, ln_g, ln_b, layer,
               *, blk, bt, router_bm):
    top_e, gate = _router(x16, w_router, b_router, bm=router_bm)
    row_tok, dest, blk_expert, n_act = _dispatch_plan(top_e, w_glu16.shape[1], blk, router_bm)
    ys = _expert_ffn(x32, row_tok, blk_expert, n_act, w_glu16, b_glu, w_lin16, b_lin, w_down16, b_down, layer,
                     blk=blk)
    return _combine_ln(ys, dest, x32, gate, ln_g, ln_b, bt=bt)


def kernel(x, w_in, nat_rpb, nat_norm_g, diff_lambda, diff_subln_g, t5_table, ml_conv_w, ml_gate_b, ml_norm_g,
           w_out, ln_mix_g, ln_mix_b, router_w, router_b, exp_w_glu, exp_b_glu, exp_w_lin, exp_b_lin,
           exp_w_down, exp_b_down, ln_ffn_g, ln_ffn_b):
    bsz, t, d = x.shape
    assert bsz == 1
    x32 = x.reshape(t, d)
    x16 = x32.astype(BF16)
    diff_bq, diff_bk = 1024, 1024
    bias_tiles, far = _diff_bias_tiles(t5_table, diff_bq, diff_bk)
    ml_col0 = (3 * NAT_WIDTH + 3 * DIFF_WIDTH) // MLSTM_WIDTH
    w_glu16, w_lin16, w_down16 = exp_w_glu.astype(BF16), exp_w_lin.astype(BF16), exp_w_down.astype(BF16)
    for l in range(DEPTH):
        w_main = w_in[l][:, :MAIN_WIDTH].astype(BF16)
        w_gate = jnp.pad(w_in[l][:, MAIN_WIDTH:], ((0, 0), (0, LANES - 4 * MLSTM_HEADS))).astype(BF16)
        u = _matmul(x16, w_main, bm=1024, bn=1024, out_dtype=BF16, name="inproj")
        gates = _matmul(x16, w_gate, bm=1024, bn=LANES, out_dtype=F32, name="inproj_gates")
        nat = _nat_attention(u, nat_rpb[l], nat_norm_g[l], blocks_per_step=16)
        lam_init = 0.8 - 0.6 * math.exp(-0.3 * l)
        dif = _diff_attention(u, bias_tiles, far, diff_lambda[l], diff_subln_g[l], lam_init,
                              bq=diff_bq, bk=diff_bk, rb=128, col0=3 * NAT_WIDTH // HEAD_DIM)
        ml = _mlstm_block(u, gates, ml_conv_w[l], ml_gate_b[l], ml_norm_g[l], lc=256, col0=ml_col0)
        x1_32, x1_16 = _proj_residual_ln([nat, dif, ml], w_out[l].astype(BF16), x32, ln_mix_g[l], ln_mix_b[l],
                                         bm=512, bn=512)
        x32, x16 = _moe_block(x1_32, x1_16, router_w[l], router_b[l], w_glu16, exp_b_glu, w_lin16, exp_b_lin,
                              w_down16, exp_b_down, ln_ffn_g[l], ln_ffn_b[l], l,
                              blk=256, bt=128, router_bm=512)
    return x32.reshape(bsz, t, d)
```
